```python
import jax, jax.numpy as jnp
from jax import lax
import numpy as np

D_MODEL = 1024
BATCH = 8
SEQ = 4096
DEPTH = 4
DEC_BATCH = 8
DEC_SEQ = 8192
PAST_LEN = 128

GRID_W = 64
HEAD_DIM = 64
NA_HEADS = 8
NA_WIN_H = 8
NA_WIN_W = 16
GQA_Q_HEADS = 8
GQA_KV_HEADS = 2
GQA_GROUP = GQA_Q_HEADS // GQA_KV_HEADS
Q_BLOCK = 128
ROPE_THETA = 10000.0
N_EXPERTS = 16
EC_CAPACITY = 2
D_EXPERT = 2048
NA_WIDTH = NA_HEADS * HEAD_DIM
GQA_WIDTH = GQA_Q_HEADS * HEAD_DIM
KV_WIDTH = GQA_KV_HEADS * HEAD_DIM
S_NA = 3 * NA_WIDTH
S_GQ = S_NA + GQA_WIDTH
S_GK = S_GQ + KV_WIDTH
S_GV = S_GK + KV_WIDTH
D_IN = S_GV + 2 * D_MODEL
DN_ALPHA = (2 * DEPTH) ** 0.25
DN_BETA = (8 * DEPTH) ** -0.25
LN_EPS = 1e-5
RMS_EPS = 1e-6

kernel_name = "hybrid_natten_gqa_ec_moe_encoder"

f32 = jnp.float32


def layer_norm(x, g, b):
    xf = x.astype(f32)
    mu = jnp.mean(xf, axis=-1, keepdims=True)
    xc = xf - mu
    var = jnp.mean(xc * xc, axis=-1, keepdims=True)
    return (xc * lax.rsqrt(var + LN_EPS) * g.astype(f32) + b.astype(f32)).astype(x.dtype)


def rms_norm_heads(x, g):
    xf = x.astype(f32)
    return (xf * lax.rsqrt(jnp.mean(xf * xf, axis=-1, keepdims=True) + RMS_EPS) * g.astype(f32)).astype(x.dtype)


def axial_rope_tables(S):
    t = jnp.arange(S)
    row = (t // GRID_W).astype(f32)
    col = (t % GRID_W).astype(f32)
    half = HEAD_DIM // 2
    inv_freq = ROPE_THETA ** (-jnp.arange(0, half, 2, dtype=f32) / half)
    ang_r = row[:, None] * inv_freq[None, :]
    ang_c = col[:, None] * inv_freq[None, :]
    ang = jnp.concatenate([ang_r, ang_r, ang_c, ang_c], axis=-1)
    return jnp.cos(ang), jnp.sin(ang)


def rotate_half(p):
    h = p.shape[-1] // 2
    return jnp.concatenate([-p[..., h:], p[..., :h]], axis=-1)


def apply_axial_rope(x, cos, sin):
    half = HEAD_DIM // 2
    xf = x.astype(f32)
    rot = jnp.concatenate([rotate_half(xf[..., :half]), rotate_half(xf[..., half:])], axis=-1)
    return (xf * cos[None, :, None, :] + rot * sin[None, :, None, :]).astype(x.dtype)


def neighbourhood_attention(q, k, v, rpb):
    B, S, H, dh = q.shape
    rows = S // GRID_W
    kh = min(NA_WIN_H, rows)
    grid = (B, rows, GRID_W, H, dh)
    qg = (q * dh ** -0.5).reshape(grid)
    kg = k.reshape(grid)
    vg = v.reshape(grid)
    r_all = jnp.arange(rows)
    row_start = jnp.clip(r_all - kh // 2, 0, rows - kh)
    c = jnp.arange(GRID_W)
    col_start = jnp.clip(c - NA_WIN_W // 2, 0, GRID_W - NA_WIN_W)
    col_idx = col_start[:, None] + jnp.arange(NA_WIN_W)[None, :]
    dc_idx = col_idx - c[:, None] + (NA_WIN_W - 1)
    rpb_col = rpb[:, :, dc_idx]

    def one_row(r):
        rs = row_start[r]
        q_r = lax.dynamic_index_in_dim(qg, r, axis=1, keepdims=False)
        k_rows = lax.dynamic_slice_in_dim(kg, rs, kh, axis=1)
        v_rows = lax.dynamic_slice_in_dim(vg, rs, kh, axis=1)
        k_win = k_rows[:, :, col_idx]
        v_win = v_rows[:, :, col_idx]
        dr_idx = rs + jnp.arange(kh) - r + (NA_WIN_H - 1)
        bias = jnp.take(rpb_col, dr_idx, axis=1).transpose(0, 2, 1, 3)
        s = jnp.einsum('bchd,bicjhd->bhcij', q_r, k_win).astype(f32) + bias.astype(f32)[None]
        p = jax.nn.softmax(s.reshape(B, H, GRID_W, kh * NA_WIN_W), axis=-1).reshape(s.shape).astype(v.dtype)
        return jnp.einsum('bhcij,bicjhd->bchd', p, v_win)

    out = lax.map(one_row, r_all)
    return out.transpose(1, 0, 2, 3, 4).reshape(B, S, H * dh)


def gqa_block_attention(q, k, v):
    B, S, Hq, dh = q.shape
    nb = S // Q_BLOCK
    qb = (q * dh ** -0.5).reshape(B, nb, Q_BLOCK, GQA_KV_HEADS, GQA_GROUP, dh).transpose(1, 0, 2, 3, 4, 5)

    def one_block(q_blk):
        s = jnp.einsum('bqkgd,bskd->bkgqs', q_blk, k).astype(f32)
        p = jax.nn.softmax(s, axis=-1).astype(v.dtype)
        return jnp.einsum('bkgqs,bskd->bqkgd', p, v)

    out = lax.map(one_block, qb)
    return out.transpose(1, 0, 2, 3, 4, 5).reshape(B, S, Hq * dh)


def expert_choice_ffn(x, w_router, w_e_gate, w_e_up, w_e_down):
    B, S, D = x.shape
    n = B * S
    cap = EC_CAPACITY * n // N_EXPERTS
    xt = x.reshape(n, D)
    affinity = jax.nn.softmax((xt @ w_router).astype(f32), axis=-1)
    gate_val, tok_idx = lax.top_k(affinity.T, cap)
    xe = jnp.take(xt, tok_idx, axis=0)
    h = jax.nn.silu(jnp.einsum('ecd,edf->ecf', xe, w_e_gate)) * jnp.einsum('ecd,edf->ecf', xe, w_e_up)
    ye = jnp.einsum('ecf,efd->ecd', h, w_e_down) * gate_val[..., None].astype(x.dtype)
    y = jnp.zeros_like(xt).at[tok_idx.reshape(-1)].add(ye.reshape(-1, D))
    return y.reshape(B, S, D)


def trunk_layer(x, w_in, na_rpb, q_norm, k_norm, w_br_na, w_br_gqa, w_out, ln1_g, ln1_b,
                w_router, w_e_gate, w_e_up, w_e_down, ln2_g, ln2_b):
    B, S, _ = x.shape
    proj = x @ w_in
    na_qkv, gq, gk, gv, gate_logits = jnp.split(proj, [S_NA, S_GQ, S_GK, S_GV], axis=-1)
    na_qkv = na_qkv.reshape(B, S, 3, NA_HEADS, HEAD_DIM)
    y_na = neighbourhood_attention(na_qkv[:, :, 0], na_qkv[:, :, 1], na_qkv[:, :, 2], na_rpb) @ w_br_na
    cos, sin = axial_rope_tables(S)
    q = apply_axial_rope(rms_norm_heads(gq.reshape(B, S, GQA_Q_HEADS, HEAD_DIM), q_norm), cos, sin)
    k = apply_axial_rope(rms_norm_heads(gk.reshape(B, S, GQA_KV_HEADS, HEAD_DIM), k_norm), cos, sin)
    v = gv.reshape(B, S, GQA_KV_HEADS, HEAD_DIM)
    y_gqa = gqa_block_attention(q, k, v) @ w_br_gqa
    gates = jax.nn.sigmoid(gate_logits)
    mix = (gates[..., :D_MODEL] * y_na + gates[..., D_MODEL:] * y_gqa) @ w_out
    x = layer_norm(DN_ALPHA * x + mix, ln1_g, ln1_b)
    x = layer_norm(DN_ALPHA * x + expert_choice_ffn(x, w_router, w_e_gate, w_e_up, w_e_down), ln2_g, ln2_b)
    return x


def setup_inputs(seed: int = 0) -> dict:
    key = jax.random.key(seed)
    ks = jax.random.split(key, 16)
    D, L, E, F = D_MODEL, DEPTH, N_EXPERTS, D_EXPERT
    nrm = lambda k, shape: jax.random.normal(k, shape, dtype=f32)
    col_scale = np.ones((D_IN,), np.float32)
    col_scale[2 * NA_WIDTH:3 * NA_WIDTH] = DN_BETA
    col_scale[S_GK:S_GV] = DN_BETA
    return {
        "x_prompt": nrm(ks[0], (BATCH, SEQ, D)),
        "x_sample": nrm(ks[1], (DEC_BATCH, DEC_SEQ, D)),
        "w_in": nrm(ks[2], (L, D, D_IN)) * D ** -0.5 * jnp.asarray(col_scale),
        "na_rpb": nrm(ks[3], (L, NA_HEADS, 2 * NA_WIN_H - 1, 2 * NA_WIN_W - 1)) * 0.1,
        "q_norm": 1.0 + 0.1 * nrm(ks[4], (L, HEAD_DIM)),
        "k_norm": 1.0 + 0.1 * nrm(ks[5], (L, HEAD_DIM)),
        "w_br_na": nrm(ks[6], (L, NA_WIDTH, D)) * NA_WIDTH ** -0.5,
        "w_br_gqa": nrm(ks[7], (L, GQA_WIDTH, D)) * GQA_WIDTH ** -0.5,
        "w_out": nrm(ks[8], (L, D, D)) * D ** -0.5 * DN_BETA,
        "ln1_g": 1.0 + 0.1 * nrm(ks[9], (L, D)),
        "ln1_b": 0.02 * nrm(ks[10], (L, D)),
        "w_router": nrm(ks[11], (L, D, E)) * D ** -0.5,
        "w_e_gate": nrm(ks[12], (L, E, D, F)) * D ** -0.5,
        "w_e_up": nrm(ks[13], (L, E, D, F)) * D ** -0.5,
        "w_e_down": nrm(ks[14], (L, E, F, D)) * F ** -0.5 * DN_BETA,
        "ln2_g": 1.0 + 0.1 * nrm(ks[15], (L, D)),
        "ln2_b": 0.02 * nrm(jax.random.fold_in(ks[15], 1), (L, D)),
    }


def reference(x_prompt, x_sample, w_in, na_rpb, q_norm, k_norm, w_br_na, w_br_gqa, w_out, ln1_g, ln1_b,
              w_router, w_e_gate, w_e_up, w_e_down, ln2_g, ln2_b):
    y_prompt = x_prompt
    y_sample = x_sample
    for l in range(DEPTH):
        p = (w_in[l], na_rpb[l], q_norm[l], k_norm[l], w_br_na[l], w_br_gqa[l], w_out[l], ln1_g[l], ln1_b[l],
             w_router[l], w_e_gate[l], w_e_up[l], w_e_down[l], ln2_g[l], ln2_b[l])
        y_prompt = trunk_layer(y_prompt, *p)
        y_sample = trunk_layer(y_sample, *p)
    return (y_prompt, y_sample)
```

```python
import functools

import jax
import jax.numpy as jnp
import numpy as np
from jax import lax
from jax.experimental import pallas as pl
from jax.experimental.pallas import tpu as pltpu

F32 = jnp.float32
BF16 = jnp.bfloat16
I32 = jnp.int32

GRID_W = 64
HEAD_DIM = 64
NA_HEADS = 8
NA_WIN_H = 8
NA_WIN_W = 16
GQA_Q_HEADS = 8
GQA_KV_HEADS = 2
GQA_GROUP = GQA_Q_HEADS // GQA_KV_HEADS
ROPE_THETA = 10000.0
EC_CAPACITY = 2
LN_EPS = 1e-5
RMS_EPS = 1e-6

LANES = 128
NA_WIDTH = NA_HEADS * HEAD_DIM
GQA_WIDTH = GQA_Q_HEADS * HEAD_DIM
KV_WIDTH = GQA_KV_HEADS * HEAD_DIM
HEAD_PAIRS = GQA_WIDTH // LANES
NEG_BIG = -1e30
VMEM_LIMIT_BYTES = 56 * 1024 * 1024

_NT = (((1,), (1,)), ((), ()))


def _params(*sem):
    return pltpu.CompilerParams(dimension_semantics=sem, vmem_limit_bytes=VMEM_LIMIT_BYTES)


def _inproj_kernel(x_ref, w_ref, cos_ref, sin_ref, qn_ref, kn_ref, gm_ref,
                   naq_ref, nak_ref, nav_ref, gq_ref, gk_ref, gv_ref, gate_ref, *, d_model):
    xb = x_ref[...].astype(BF16)
    s_na = 3 * NA_WIDTH
    s_gq = s_na + GQA_WIDTH
    s_gk = s_gq + KV_WIDTH
    s_gv = s_gk + KV_WIDTH

    def proj(c0, width):
        return jnp.dot(xb, w_ref[:, c0:c0 + width], preferred_element_type=F32)

    naq_ref[...] = (proj(0, NA_WIDTH) * HEAD_DIM ** -0.5).astype(BF16)
    nak_ref[...] = proj(NA_WIDTH, NA_WIDTH).astype(BF16)
    nav_ref[...] = proj(2 * NA_WIDTH, NA_WIDTH).astype(BF16)

    cos = cos_ref[...]
    sin = sin_ref[...]
    gm = gm_ref[...]
    lane = lax.broadcasted_iota(I32, cos.shape, 1)
    first_half = (lane % (HEAD_DIM // 2)) < (HEAD_DIM // 4)

    def norm_rope(a, gain):
        sq = a * a
        hi = sq.astype(BF16)
        lo = (sq - hi.astype(F32)).astype(BF16)
        ms = jnp.dot(hi, gm, preferred_element_type=F32) + jnp.dot(lo, gm, preferred_element_type=F32)
        an = a * lax.rsqrt(ms + RMS_EPS) * gain
        quarter = HEAD_DIM // 4
        rot = jnp.where(first_half, pltpu.roll(an, LANES - quarter, 1), pltpu.roll(an, quarter, 1))
        return an * cos + rot * sin

    qn = qn_ref[...]
    for p in range(HEAD_PAIRS):
        a = proj(s_na + p * LANES, LANES)
        gq_ref[:, p * LANES:(p + 1) * LANES] = (norm_rope(a, qn) * HEAD_DIM ** -0.5).astype(BF16)
    gk_ref[...] = norm_rope(proj(s_gq, KV_WIDTH), kn_ref[...]).astype(BF16)
    gv_ref[...] = proj(s_gk, KV_WIDTH).astype(BF16)
    gate_chunk = 512
    for c in range(2 * d_model // gate_chunk):
        g = proj(s_gv + c * gate_chunk, gate_chunk)
        gate_ref[:, c * gate_chunk:(c + 1) * gate_chunk] = jax.nn.sigmoid(g)


def _inproj(x2d, w_in_b, cos, sin, qn, kn, gm, *, seq, tm):
    n, d = x2d.shape
    d_in = w_in_b.shape[1]
    assert n % tm == 0 and seq % tm == 0
    sblocks = seq // tm
    row = lambda i: (i, 0)
    const = lambda i: (0, 0)
    pos = lambda i: (i % sblocks, 0)
    out_shape = [
        jax.ShapeDtypeStruct((n, NA_WIDTH), BF16), jax.ShapeDtypeStruct((n, NA_WIDTH), BF16),
        jax.ShapeDtypeStruct((n, NA_WIDTH), BF16), jax.ShapeDtypeStruct((n, GQA_WIDTH), BF16),
        jax.ShapeDtypeStruct((n, KV_WIDTH), BF16), jax.ShapeDtypeStruct((n, KV_WIDTH), BF16),
        jax.ShapeDtypeStruct((n, 2 * d), F32),
    ]
    out_specs = [
        pl.BlockSpec((tm, NA_WIDTH), row), pl.BlockSpec((tm, NA_WIDTH), row), pl.BlockSpec((tm, NA_WIDTH), row),
        pl.BlockSpec((tm, GQA_WIDTH), row), pl.BlockSpec((tm, KV_WIDTH), row), pl.BlockSpec((tm, KV_WIDTH), row),
        pl.BlockSpec((tm, 2 * d), row),
    ]
    return pl.pallas_call(
        functools.partial(_inproj_kernel, d_model=d),
        grid=(n // tm,),
        in_specs=[
            pl.BlockSpec((tm, d), row), pl.BlockSpec((d, d_in), const),
            pl.BlockSpec((tm, LANES), pos), pl.BlockSpec((tm, LANES), pos),
            pl.BlockSpec((1, LANES), const), pl.BlockSpec((1, LANES), const),
            pl.BlockSpec((LANES, LANES), const),
        ],
        out_specs=out_specs,
        out_shape=out_shape,
        compiler_params=_params("parallel"),
        name="inproj",
    )(x2d, w_in_b, cos, sin, qn, kn, gm)


def _na_kernel(q_ref, kp_ref, kc_ref, kn_ref, vp_ref, vc_ref, vn_ref, bias_ref, o_ref,
               kbuf, vbuf, *, rows, nrb):
    blk = NA_WIN_H * GRID_W
    j = pl.program_id(0) % nrb
    kbuf[0:blk] = kp_ref[...]
    kbuf[blk:2 * blk] = kc_ref[...]
    kbuf[2 * blk:3 * blk] = kn_ref[...]
    vbuf[0:blk] = vp_ref[...]
    vbuf[blk:2 * blk] = vc_ref[...]
    vbuf[2 * blk:3 * blk] = vn_ref[...]
    lane = lax.broadcasted_iota(I32, (GRID_W, LANES), 1)
    low = lane < HEAD_DIM

    def row_body(i, carry):
        r = j * NA_WIN_H + i
        rs = jnp.clip(r - NA_WIN_H // 2, 0, rows - NA_WIN_H)
        d0 = rs - r + (NA_WIN_H - 1)
        off = pl.multiple_of((rs - (j - 1) * NA_WIN_H) * GRID_W, GRID_W)
        qoff = pl.multiple_of(i * GRID_W, GRID_W)
        for p in range(NA_WIDTH // LANES):
            cols = slice(p * LANES, (p + 1) * LANES)
            q2 = q_ref[pl.ds(qoff, GRID_W), cols]
            k2 = kbuf[pl.ds(off, blk), cols]
            v2 = vbuf[pl.ds(off, blk), cols]
            outs = []
            for half in range(2):
                qm = jnp.where(low if half == 0 else jnp.logical_not(low), q2, jnp.zeros_like(q2))
                s = lax.dot_general(qm, k2, _NT, preferred_element_type=F32)
                s = s + bias_ref[d0, 2 * p + half]
                m = jnp.max(s, axis=-1, keepdims=True)
                e = jnp.exp(s - m)
                prob = e * (1.0 / jnp.sum(e, axis=-1, keepdims=True))
                outs.append(jnp.dot(prob.astype(BF16), v2, preferred_element_type=F32))
            o_ref[pl.ds(qoff, GRID_W), cols] = jnp.where(low, outs[0], outs[1]).astype(BF16)
        return carry

    lax.fori_loop(0, NA_WIN_H, row_body, 0)


def _na_bias_table(rpb):
    c = jnp.arange(GRID_W)
    cs = jnp.clip(c - NA_WIN_W // 2, 0, GRID_W - NA_WIN_W)
    cc = jnp.arange(GRID_W)
    inwin = (cc[None, :] >= cs[:, None]) & (cc[None, :] < cs[:, None] + NA_WIN_W)
    dc = jnp.clip(cc[None, :] - c[:, None] + (NA_WIN_W - 1), 0, 2 * NA_WIN_W - 2)
    full = jnp.where(inwin[None, None], rpb[:, :, dc].astype(F32), NEG_BIG)
    tbl = jnp.stack([full[:, d0:d0 + NA_WIN_H] for d0 in range(NA_WIN_H)])
    return tbl.transpose(0, 1, 3, 2, 4).reshape(NA_WIN_H, NA_HEADS, GRID_W, NA_WIN_H * GRID_W)


def _na_attention(q, k, v, bias_tbl, *, seq):
    n = q.shape[0]
    rows = seq // GRID_W
    assert rows % NA_WIN_H == 0 and rows >= 2 * NA_WIN_H
    nrb = rows // NA_WIN_H
    blk = NA_WIN_H * GRID_W

    def cur(g):
        return (g, 0)

    def prev(g):
        return (g - jnp.where(g % nrb == 0, 0, 1), 0)

    def nxt(g):
        return (g + jnp.where(g % nrb == nrb - 1, 0, 1), 0)

    spec = lambda f: pl.BlockSpec((blk, NA_WIDTH), f)
    return pl.pallas_call(
        functools.partial(_na_kernel, rows=rows, nrb=nrb),
        grid=(n // blk,),
        in_specs=[spec(cur), spec(prev), spec(cur), spec(nxt), spec(prev), spec(cur), spec(nxt),
                  pl.BlockSpec(bias_tbl.shape, lambda g: (0, 0, 0, 0))],
        out_specs=spec(cur),
        out_shape=jax.ShapeDtypeStruct((n, NA_WIDTH), BF16),
        scratch_shapes=[pltpu.VMEM((3 * blk, NA_WIDTH), BF16), pltpu.VMEM((3 * blk, NA_WIDTH), BF16)],
        compiler_params=_params("parallel"),
        name="na_attention",
    )(q, k, k, k, v, v, v, bias_tbl)


def _gqa_kernel(q_ref, k_ref, v_ref, o_ref, m_sc, l_sc, acc_sc, *, tq, tk, seq):
    lane = lax.broadcasted_iota(I32, (tq, LANES), 1)
    low = lane < HEAD_DIM
    for p in range(HEAD_PAIRS):
        cols = slice(p * LANES, (p + 1) * LANES)
        q2 = q_ref[:, cols]
        zero = jnp.zeros_like(q2)
        qs = jnp.concatenate([jnp.where(low, q2, zero), jnp.where(low, zero, q2)], axis=0)
        m_sc[...] = jnp.full(m_sc.shape, -jnp.inf, F32)
        l_sc[...] = jnp.zeros(l_sc.shape, F32)
        acc_sc[...] = jnp.zeros(acc_sc.shape, F32)

        def body(c, carry):
            koff = pl.multiple_of(c * tk, tk)
            kk = k_ref[pl.ds(koff, tk), :]
            vv = v_ref[pl.ds(koff, tk), :]
            s = lax.dot_general(qs, kk, _NT, preferred_element_type=F32)
            m_old = m_sc[...]
            m_new = jnp.maximum(m_old, jnp.max(s, axis=-1, keepdims=True))
            alpha = jnp.exp(m_old - m_new)
            e = jnp.exp(s - m_new)
            l_sc[...] = alpha * l_sc[...] + jnp.sum(e, axis=-1, keepdims=True)
            acc_sc[...] = alpha * acc_sc[...] + jnp.dot(e.astype(BF16), vv, preferred_element_type=F32)
            m_sc[...] = m_new
            return carry

        lax.fori_loop(0, seq // tk, body, 0)
        o = acc_sc[...] * (1.0 / l_sc[...])
        o_ref[:, cols] = jnp.where(low, o[:tq], o[tq:]).astype(BF16)


def _gqa_attention(q, k, v, *, seq, tq, tk):
    n = q.shape[0]
    assert seq % tq == 0 and seq % tk == 0
    qblocks = seq // tq
    return pl.pallas_call(
        functools.partial(_gqa_kernel, tq=tq, tk=tk, seq=seq),
        grid=(n // seq, qblocks),
        in_specs=[
            pl.BlockSpec((tq, GQA_WIDTH), lambda b, i: (b * qblocks + i, 0)),
            pl.BlockSpec((seq, KV_WIDTH), lambda b, i: (b, 0)),
            pl.BlockSpec((seq, KV_WIDTH), lambda b, i: (b, 0)),
        ],
        out_specs=pl.BlockSpec((tq, GQA_WIDTH), lambda b, i: (b * qblocks + i, 0)),
        out_shape=jax.ShapeDtypeStruct((n, GQA_WIDTH), BF16),
        scratch_shapes=[pltpu.VMEM((2 * tq, 1), F32), pltpu.VMEM((2 * tq, 1), F32),
                        pltpu.VMEM((2 * tq, LANES), F32)],
        compiler_params=_params("parallel", "parallel"),
        name="gqa_attention",
    )(q, k, v)


def _layer_norm(h, g, b):
    mu = jnp.mean(h, axis=-1, keepdims=True)
    hc = h - mu
    var = jnp.mean(hc * hc, axis=-1, keepdims=True)
    return hc * lax.rsqrt(var + LN_EPS) * g + b


def _merge_kernel(na_ref, gq_ref, gate_ref, x_ref, wna_ref, wgq_ref, wout_ref, g_ref, b_ref,
                  wrh_ref, wrl_ref, x1_ref, aff_ref, *, alpha, d_model):
    y_na = jnp.dot(na_ref[...], wna_ref[...], preferred_element_type=F32)
    y_gq = jnp.dot(gq_ref[...], wgq_ref[...], preferred_element_type=F32)
    mixin = gate_ref[:, :d_model] * y_na + gate_ref[:, d_model:] * y_gq
    mix = jnp.dot(mixin.astype(BF16), wout_ref[...], preferred_element_type=F32)
    x1 = _layer_norm(alpha * x_ref[...] + mix, g_ref[...], b_ref[...])
    x1_ref[...] = x1
    hi = x1.astype(BF16)
    lo = (x1 - hi.astype(F32)).astype(BF16)
    wh = wrh_ref[...]
    logits = (lax.dot_general(wh, hi, _NT, preferred_element_type=F32)
              + lax.dot_general(wh, lo, _NT, preferred_element_type=F32)
              + lax.dot_general(wrl_ref[...], hi, _NT, preferred_element_type=F32))
    m = jnp.max(logits, axis=0, keepdims=True)
    e = jnp.exp(logits - m)
    aff_ref[...] = e / jnp.sum(e, axis=0, keepdims=True)


def _merge(na, gq, gates, x2d, wna, wgq, wout, ln_g, ln_b, wr_hi, wr_lo, *, alpha, tm):
    n, d = x2d.shape
    n_exp = wr_hi.shape[0]
    row = lambda i: (i, 0)
    const = lambda i: (0, 0)
    return pl.pallas_call(
        functools.partial(_merge_kernel, alpha=alpha, d_model=d),
        grid=(n // tm,),
        in_specs=[
            pl.BlockSpec((tm, NA_WIDTH), row), pl.BlockSpec((tm, GQA_WIDTH), row),
            pl.BlockSpec((tm, 2 * d), row), pl.BlockSpec((tm, d), row),
            pl.BlockSpec((NA_WIDTH, d), const), pl.BlockSpec((GQA_WIDTH, d), const),
            pl.BlockSpec((d, d), const), pl.BlockSpec((1, d), const), pl.BlockSpec((1, d), const),
            pl.BlockSpec((n_exp, d), const), pl.BlockSpec((n_exp, d), const),
        ],
        out_specs=[pl.BlockSpec((tm, d), row), pl.BlockSpec((n_exp, tm), lambda i: (0, i))],
        out_shape=[jax.ShapeDtypeStruct((n, d), F32), jax.ShapeDtypeStruct((n_exp, n), F32)],
        compiler_params=_params("parallel"),
        name="merge_ln_router",
    )(na, gq, gates, x2d, wna, wgq, wout, ln_g, ln_b, wr_hi, wr_lo)


def _route_kernel(aff_ref, src_ref, gate_ref, cnt_ref, *, cap, count_chunk):
    n_exp, n = aff_ref.shape
    capf = float(cap)

    def bits_at(off, width):
        return lax.bitcast_convert_type(aff_ref[:, pl.ds(off, width)], I32)

    def count_ge(cand):
        def inner(c, acc):
            b = bits_at(pl.multiple_of(c * count_chunk, count_chunk), count_chunk)
            return acc + jnp.where(b >= cand, 1.0, 0.0)
        acc = lax.fori_loop(0, n // count_chunk, inner, jnp.zeros((n_exp, count_chunk), F32))
        return jnp.sum(acc, axis=1, keepdims=True)

    def bisect(i, prefix):
        cand = prefix | jnp.left_shift(jnp.int32(1), 30 - i)
        return jnp.where(count_ge(cand) >= capf, cand, prefix)

    thr = lax.fori_loop(0, 31, bisect, jnp.zeros((n_exp, 1), I32))
    need = capf - count_ge(thr + 1)

    ri = lax.broadcasted_iota(I32, (LANES, LANES), 0)
    ci = lax.broadcasted_iota(I32, (LANES, LANES), 1)
    upper = jnp.where(ri < ci, 1.0, 0.0).astype(BF16)
    ones = jnp.ones((LANES, LANES), BF16)
    er = lax.broadcasted_iota(I32, (n_exp, n_exp), 0)
    ec = lax.broadcasted_iota(I32, (n_exp, n_exp), 1)
    lower = jnp.where(ec < er, 1.0, 0.0).astype(BF16)
    rowbase = (lax.broadcasted_iota(I32, (n_exp, LANES), 0) * cap).astype(F32)

    def body(c, carry):
        ceq, csel = carry
        off = pl.multiple_of(c * LANES, LANES)
        a = aff_ref[:, pl.ds(off, LANES)]
        b = lax.bitcast_convert_type(a, I32)
        eq = b == thr
        eqb = jnp.where(eq, 1.0, 0.0).astype(BF16)
        eqrank = jnp.dot(eqb, upper, preferred_element_type=F32) + ceq
        sel = (b > thr) | (eq & (eqrank < need))
        selb = jnp.where(sel, 1.0, 0.0).astype(BF16)
        pos = jnp.dot(selb, upper, preferred_element_type=F32) + csel
        rank = jnp.dot(lower, selb, preferred_element_type=F32)
        flat = pos + rowbase
        src_rows, gate_rows = [], []
        for r in range(n_exp):
            hit = sel & (rank == float(r))
            src_rows.append(jnp.sum(jnp.where(hit, flat, 0.0), axis=0, keepdims=True))
            gate_rows.append(jnp.sum(jnp.where(hit, a, 0.0), axis=0, keepdims=True))
        src_ref[:, pl.ds(off, LANES)] = jnp.concatenate(src_rows, axis=0).astype(I32)
        gate_ref[:, pl.ds(off, LANES)] = jnp.concatenate(gate_rows, axis=0)
        cnt_ref[:, pl.ds(off, LANES)] = jnp.sum(jnp.where(sel, 1.0, 0.0), axis=0, keepdims=True).astype(I32)
        return (ceq + jnp.dot(eqb, ones, preferred_element_type=F32),
                csel + jnp.dot(selb, ones, preferred_element_type=F32))

    zero = jnp.zeros((n_exp, LANES), F32)
    lax.fori_loop(0, n // LANES, body, (zero, zero))


def _route(aff, *, cap):
    n_exp, n = aff.shape
    count_chunk = min(2048, n)
    assert n % count_chunk == 0 and n % LANES == 0
    full = lambda shape: pl.BlockSpec(shape, lambda i: (0, 0))
    return pl.pallas_call(
        functools.partial(_route_kernel, cap=cap, count_chunk=count_chunk),
        grid=(1,),
        in_specs=[full((n_exp, n))],
        out_specs=[full((n_exp, n)), full((n_exp, n)), full((1, n))],
        out_shape=[jax.ShapeDtypeStruct((n_exp, n), I32), jax.ShapeDtypeStruct((n_exp, n), F32),
                   jax.ShapeDtypeStruct((1, n), I32)],
        compiler_params=_params("arbitrary"),
        name="route",
    )(aff)


def _row_copy(src_ref, src_row, dst_ref, dst_row, sem):
    return pltpu.make_async_copy(src_ref.at[pl.ds(src_row, 1)], dst_ref.at[pl.ds(dst_row, 1)], sem)


def _dispatch_kernel(cnt_ref, src_ref, x_ref, xe_ref, sem, *, tile):
    def per_token(t, total):
        c = cnt_ref[0, t]

        def per_pick(r, carry):
            _row_copy(x_ref, t, xe_ref, src_ref[r, t], sem).start()
            return carry

        lax.fori_loop(0, c, per_pick, 0)
        return total + c

    total = lax.fori_loop(0, tile, per_token, jnp.int32(0))

    def wait_one(i, carry):
        _row_copy(x_ref, 0, xe_ref, 0, sem).wait()
        return carry

    lax.fori_loop(0, total, wait_one, 0)


def _dispatch(x1, cnt, src, *, n_slots, tile):
    n, d = x1.shape
    n_exp = src.shape[0]
    return pl.pallas_call(
        functools.partial(_dispatch_kernel, tile=tile),
        grid=(n // tile,),
        in_specs=[
            pl.BlockSpec((1, tile), lambda i: (0, i), memory_space=pltpu.SMEM),
            pl.BlockSpec((n_exp, tile), lambda i: (0, i), memory_space=pltpu.SMEM),
            pl.BlockSpec((tile, d), lambda i: (i, 0)),
        ],
        out_specs=pl.BlockSpec(memory_space=pl.ANY),
        out_shape=jax.ShapeDtypeStruct((n_slots, d), F32),
        scratch_shapes=[pltpu.SemaphoreType.DMA(())],
        compiler_params=_params("arbitrary"),
        name="dispatch",
    )(cnt, src, x1)


def _ffn_kernel(xe_ref, wg_ref, wu_ref, wd_ref, ye_ref, *, f_chunk):
    xb = xe_ref[...].astype(BF16)
    d_ff = wg_ref.shape[2]
    acc = jnp.zeros(ye_ref.shape, F32)
    for c in range(d_ff // f_chunk):
        cols = slice(c * f_chunk, (c + 1) * f_chunk)
        g = jnp.dot(xb, wg_ref[0, :, cols], preferred_element_type=F32)
        u = jnp.dot(xb, wu_ref[0, :, cols], preferred_element_type=F32)
        h = (g * jax.nn.sigmoid(g)) * u
        acc = acc + jnp.dot(h.astype(BF16), wd_ref[0, cols, :], preferred_element_type=F32)
    ye_ref[...] = acc


def _ffn(xe, wg, wu, wd, *, cap, tm, f_chunk):
    n_slots, d = xe.shape
    n_exp, _, d_ff = wg.shape
    assert cap % tm == 0 and d_ff % f_chunk == 0
    kblocks = cap // tm
    return pl.pallas_call(
        functools.partial(_ffn_kernel, f_chunk=f_chunk),
        grid=(n_exp, kblocks),
        in_specs=[
            pl.BlockSpec((tm, d), lambda e, k: (e * kblocks + k, 0)),
            pl.BlockSpec((1, d, d_ff), lambda e, k: (e, 0, 0)),
            pl.BlockSpec((1, d, d_ff), lambda e, k: (e, 0, 0)),
            pl.BlockSpec((1, d_ff, d), lambda e, k: (e, 0, 0)),
        ],
        out_specs=pl.BlockSpec((tm, d), lambda e, k: (e * kblocks + k, 0)),
        out_shape=jax.ShapeDtypeStruct((n_slots, d), F32),
        compiler_params=_params("parallel", "parallel"),
        name="expert_ffn",
    )(xe, wg, wu, wd)


def _combine_kernel(cnt_ref, src_ref, x_ref, gate_ref, cntc_ref, g_ref, b_ref, ye_ref, o_ref,
                    buf, sem, *, tile, alpha):
    n_exp = src_ref.shape[0]

    def per_token(t, carry):
        total, most = carry
        c = cnt_ref[0, t]

        def per_pick(r, inner):
            _row_copy(ye_ref, src_ref[r, t], buf.at[r], t, sem).start()
            return inner

        lax.fori_loop(0, c, per_pick, 0)
        return total + c, jnp.maximum(most, c)

    total, most = lax.fori_loop(0, tile, per_token, (jnp.int32(0), jnp.int32(0)))

    def wait_one(i, carry):
        _row_copy(ye_ref, 0, buf.at[0], 0, sem).wait()
        return carry

    lax.fori_loop(0, total, wait_one, 0)

    o_ref[...] = alpha * x_ref[...]
    cntc = cntc_ref[...]
    for r in range(n_exp):
        @pl.when(most > r)
        def _():
            contrib = buf[r] * gate_ref[:, r:r + 1]
            o_ref[...] += jnp.where(cntc > r, contrib, 0.0)

    o_ref[...] = _layer_norm(o_ref[...], g_ref[...], b_ref[...])


def _combine(x1, cnt, src, gate_t, cnt_col, ln_g, ln_b, ye, *, alpha, tile):
    n, d = x1.shape
    n_exp = src.shape[0]
    row = lambda i: (i, 0)
    const = lambda i: (0, 0)
    return pl.pallas_call(
        functools.partial(_combine_kernel, tile=tile, alpha=alpha),
        grid=(n // tile,),
        in_specs=[
            pl.BlockSpec((1, tile), lambda i: (0, i), memory_space=pltpu.SMEM),
            pl.BlockSpec((n_exp, tile), lambda i: (0, i), memory_space=pltpu.SMEM),
            pl.BlockSpec((tile, d), row), pl.BlockSpec((tile, n_exp), row), pl.BlockSpec((tile, 1), row),
            pl.BlockSpec((1, d), const), pl.BlockSpec((1, d), const),
            pl.BlockSpec(memory_space=pl.ANY),
        ],
        out_specs=pl.BlockSpec((tile, d), row),
        out_shape=jax.ShapeDtypeStruct((n, d), F32),
        scratch_shapes=[pltpu.VMEM((n_exp, tile, d), F32), pltpu.SemaphoreType.DMA(())],
        compiler_params=_params("arbitrary"),
        name="combine_ln",
    )(cnt, src, x1, gate_t, cnt_col, ln_g, ln_b, ye)


def _rope_tables(seq):
    t = jnp.arange(seq)
    row = (t // GRID_W).astype(F32)
    col = (t % GRID_W).astype(F32)
    half = HEAD_DIM // 2
    inv_freq = ROPE_THETA ** (-jnp.arange(0, half, 2, dtype=F32) / half)
    ang_r = row[:, None] * inv_freq[None, :]
    ang_c = col[:, None] * inv_freq[None, :]
    ang = jnp.concatenate([ang_r, ang_r, ang_c, ang_c], axis=-1)
    sign = jnp.where((jnp.arange(HEAD_DIM) % half) < half // 2, -1.0, 1.0).astype(F32)
    reps = LANES // HEAD_DIM
    return jnp.tile(jnp.cos(ang), (1, reps)), jnp.tile(jnp.sin(ang) * sign[None, :], (1, reps))


def _gqa_slot_columns():
    heads = [p + GQA_GROUP * half for p in range(HEAD_PAIRS) for half in range(2)]
    return np.concatenate([np.arange(HEAD_DIM) + HEAD_DIM * h for h in heads])


def _prep_layer(w_in, na_rpb, q_norm, k_norm, w_br_na, w_br_gqa, w_out, ln1_g, ln1_b,
                w_router, w_e_gate, w_e_up, w_e_down, ln2_g, ln2_b):
    d = w_in.shape[0]
    s_na = 3 * NA_WIDTH
    cols = _gqa_slot_columns()
    perm = np.concatenate([np.arange(s_na), s_na + cols, np.arange(s_na + GQA_WIDTH, w_in.shape[1])])
    reps = LANES // HEAD_DIM
    wr_t = w_router.T.astype(F32)
    wr_hi = wr_t.astype(BF16)
    gm = np.kron(np.eye(reps, dtype=np.float32), np.full((HEAD_DIM, HEAD_DIM), 1.0 / HEAD_DIM, np.float32))
    return dict(
        w_in=w_in[:, perm].astype(BF16),
        bias_tbl=_na_bias_table(na_rpb),
        qn=jnp.tile(q_norm.astype(F32), reps)[None, :], kn=jnp.tile(k_norm.astype(F32), reps)[None, :],
        gm=jnp.asarray(gm, BF16),
        wna=w_br_na.astype(BF16), wgq=w_br_gqa[cols].astype(BF16), wout=w_out.astype(BF16),
        ln1_g=ln1_g.astype(F32).reshape(1, d), ln1_b=ln1_b.astype(F32).reshape(1, d),
        wr_hi=wr_hi, wr_lo=(wr_t - wr_hi.astype(F32)).astype(BF16),
        wg=w_e_gate.astype(BF16), wu=w_e_up.astype(BF16), wd=w_e_down.astype(BF16),
        ln2_g=ln2_g.astype(F32).reshape(1, d), ln2_b=ln2_b.astype(F32).reshape(1, d),
    )


def _tiles(seq):
    return dict(tm_proj=512, tq=256, tk=512, tm_merge=256, t_dispatch=256, tm_ffn=512, f_chunk=512,
                t_combine=128)


def _trunk_layer(x2d, p, *, seq, alpha, rope):
    n, d = x2d.shape
    n_exp = p["wg"].shape[0]
    cap = EC_CAPACITY * n // n_exp
    tl = _tiles(seq)
    cos, sin = rope
    naq, nak, nav, gq, gk, gv, gates = _inproj(x2d, p["w_in"], cos, sin, p["qn"], p["kn"], p["gm"],
                                               seq=seq, tm=min(tl["tm_proj"], seq))
    na = _na_attention(naq, nak, nav, p["bias_tbl"], seq=seq)
    ga = _gqa_attention(gq, gk, gv, seq=seq, tq=min(tl["tq"], seq), tk=min(tl["tk"], seq))
    x1, aff = _merge(na, ga, gates, x2d, p["wna"], p["wgq"], p["wout"], p["ln1_g"], p["ln1_b"],
                     p["wr_hi"], p["wr_lo"], alpha=alpha, tm=tl["tm_merge"])
    src, gate, cnt = _route(aff, cap=cap)
    xe = _dispatch(x1, cnt, src, n_slots=n_exp * cap, tile=tl["t_dispatch"])
    ye = _ffn(xe, p["wg"], p["wu"], p["wd"], cap=cap, tm=min(tl["tm_ffn"], cap), f_chunk=tl["f_chunk"])
    return _combine(x1, cnt, src, gate.T, cnt.reshape(n, 1), p["ln2_g"], p["ln2_b"], ye,
                    alpha=alpha, tile=tl["t_combine"])


@jax.jit
def kernel(x_prompt, x_sample, w_in, na_rpb, q_norm, k_norm, w_br_na, w_br_gqa, w_out, ln1_g, ln1_b,
           w_router, w_e_gate, w_e_up, w_e_down, ln2_g, ln2_b):
    depth = w_in.shape[0]
    alpha = float((2 * depth) ** 0.25)
    d = x_prompt.shape[-1]
    groups = []
    for x in (x_prompt, x_sample):
        b, s, _ = x.shape
        groups.append(dict(x=x.reshape(b * s, d), shape=x.shape, seq=s, rope=_rope_tables(s)))
    for l in range(depth):
        p = _prep_layer(w_in[l], na_rpb[l], q_norm[l], k_norm[l], w_br_na[l], w_br_gqa[l], w_out[l],
                        ln1_g[l], ln1_b[l], w_router[l], w_e_gate[l], w_e_up[l], w_e_down[l],
                        ln2_g[l], ln2_b[l])
        for g in groups:
            g["x"] = _trunk_layer(g["x"], p, seq=g["seq"], alpha=alpha, rope=g["rope"])
    return tuple(g["x"].reshape(g["shape"]) for g in groups)
```

```python
import functools

import jax
import jax.numpy as jnp
import numpy as np
from jax import lax
from jax.experimental import pallas as pl
from jax.experimental.pallas import tpu as pltpu

F32 = jnp.float32
BF16 = jnp.bfloat16
I32 = jnp.int32

GRID_W = 64
HEAD_DIM = 64
NA_HEADS = 8
NA_WIN_H = 8
NA_WIN_W = 16
GQA_Q_HEADS = 8
GQA_KV_HEADS = 2
GQA_GROUP = GQA_Q_HEADS // GQA_KV_HEADS
ROPE_THETA = 10000.0
EC_CAPACITY = 2
LN_EPS = 1e-5
RMS_EPS = 1e-6

LANES = 128
SUBLANES = 8
NA_WIDTH = NA_HEADS * HEAD_DIM
GQA_WIDTH = GQA_Q_HEADS * HEAD_DIM
KV_WIDTH = GQA_KV_HEADS * HEAD_DIM
HEAD_PAIRS = GQA_WIDTH // LANES
UNROLLED_PICKS = 4
GQA_Q_SCALE = HEAD_DIM ** -0.5 * float(np.log2(np.e))
NEG_BIG = -1e30
VMEM_LIMIT_BYTES = 56 * 1024 * 1024

_NT = (((1,), (1,)), ((), ()))


def _params(*sem):
    return pltpu.CompilerParams(dimension_semantics=sem, vmem_limit_bytes=VMEM_LIMIT_BYTES)


def _inproj_kernel(x_ref, w_ref, cos_ref, sin_ref, qn_ref, kn_ref, gm_ref,
                   naq_ref, nak_ref, nav_ref, gq_ref, gk_ref, gv_ref, gate_ref, *, d_model):
    xb = x_ref[...].astype(BF16)
    s_na = 3 * NA_WIDTH
    s_gq = s_na + GQA_WIDTH
    s_gk = s_gq + KV_WIDTH
    s_gv = s_gk + KV_WIDTH

    def proj(c0, width):
        return jnp.dot(xb, w_ref[:, c0:c0 + width], preferred_element_type=F32)

    naq_ref[...] = (proj(0, NA_WIDTH) * HEAD_DIM ** -0.5).astype(BF16)
    nak_ref[...] = proj(NA_WIDTH, NA_WIDTH).astype(BF16)
    nav_ref[...] = proj(2 * NA_WIDTH, NA_WIDTH).astype(BF16)

    cos = cos_ref[...]
    sin = sin_ref[...]
    gm = gm_ref[...]
    lane = lax.broadcasted_iota(I32, cos.shape, 1)
    first_half = (lane % (HEAD_DIM // 2)) < (HEAD_DIM // 4)

    def norm_rope(a, gain):
        sq = a * a
        hi = sq.astype(BF16)
        lo = (sq - hi.astype(F32)).astype(BF16)
        ms = jnp.dot(hi, gm, preferred_element_type=F32) + jnp.dot(lo, gm, preferred_element_type=F32)
        an = a * lax.rsqrt(ms + RMS_EPS) * gain
        quarter = HEAD_DIM // 4
        rot = jnp.where(first_half, pltpu.roll(an, LANES - quarter, 1), pltpu.roll(an, quarter, 1))
        return an * cos + rot * sin

    qn = qn_ref[...]
    for p in range(HEAD_PAIRS):
        a = proj(s_na + p * LANES, LANES)
        gq_ref[:, p * LANES:(p + 1) * LANES] = (norm_rope(a, qn) * GQA_Q_SCALE).astype(BF16)
    gk_ref[...] = norm_rope(proj(s_gq, KV_WIDTH), kn_ref[...]).astype(BF16)
    gv_ref[...] = proj(s_gk, KV_WIDTH).astype(BF16)
    gate_chunk = 512
    for c in range(2 * d_model // gate_chunk):
        g = proj(s_gv + c * gate_chunk, gate_chunk)
        gate_ref[:, c * gate_chunk:(c + 1) * gate_chunk] = jax.nn.sigmoid(g)


def _inproj(x2d, w_in_b, cos, sin, qn, kn, gm, *, seq, tm):
    n, d = x2d.shape
    d_in = w_in_b.shape[1]
    assert n % tm == 0 and seq % tm == 0
    sblocks = seq // tm
    row = lambda i: (i, 0)
    const = lambda i: (0, 0)
    pos = lambda i: (i % sblocks, 0)
    out_shape = [
        jax.ShapeDtypeStruct((n, NA_WIDTH), BF16), jax.ShapeDtypeStruct((n, NA_WIDTH), BF16),
        jax.ShapeDtypeStruct((n, NA_WIDTH), BF16), jax.ShapeDtypeStruct((n, GQA_WIDTH), BF16),
        jax.ShapeDtypeStruct((n, KV_WIDTH), BF16), jax.ShapeDtypeStruct((n, KV_WIDTH), BF16),
        jax.ShapeDtypeStruct((n, 2 * d), F32),
    ]
    out_specs = [
        pl.BlockSpec((tm, NA_WIDTH), row), pl.BlockSpec((tm, NA_WIDTH), row), pl.BlockSpec((tm, NA_WIDTH), row),
        pl.BlockSpec((tm, GQA_WIDTH), row), pl.BlockSpec((tm, KV_WIDTH), row), pl.BlockSpec((tm, KV_WIDTH), row),
        pl.BlockSpec((tm, 2 * d), row),
    ]
    return pl.pallas_call(
        functools.partial(_inproj_kernel, d_model=d),
        grid=(n // tm,),
        in_specs=[
            pl.BlockSpec((tm, d), row), pl.BlockSpec((d, d_in), const),
            pl.BlockSpec((tm, LANES), pos), pl.BlockSpec((tm, LANES), pos),
            pl.BlockSpec((1, LANES), const), pl.BlockSpec((1, LANES), const),
            pl.BlockSpec((LANES, LANES), const),
        ],
        out_specs=out_specs,
        out_shape=out_shape,
        compiler_params=_params("parallel"),
        name="inproj",
    )(x2d, w_in_b, cos, sin, qn, kn, gm)


def _na_kernel(q_ref, kp_ref, kc_ref, kn_ref, vp_ref, vc_ref, vn_ref, bias_ref, o_ref,
               kbuf, vbuf, *, rows, nrb):
    blk = NA_WIN_H * GRID_W
    j = pl.program_id(0) % nrb
    kbuf[0:blk] = kp_ref[...]
    kbuf[blk:2 * blk] = kc_ref[...]
    kbuf[2 * blk:3 * blk] = kn_ref[...]
    vbuf[0:blk] = vp_ref[...]
    vbuf[blk:2 * blk] = vc_ref[...]
    vbuf[2 * blk:3 * blk] = vn_ref[...]
    lane = lax.broadcasted_iota(I32, (GRID_W, LANES), 1)
    low = lane < HEAD_DIM

    def row_body(i, carry):
        r = j * NA_WIN_H + i
        rs = jnp.clip(r - NA_WIN_H // 2, 0, rows - NA_WIN_H)
        d0 = rs - r + (NA_WIN_H - 1)
        off = pl.multiple_of((rs - (j - 1) * NA_WIN_H) * GRID_W, GRID_W)
        qoff = pl.multiple_of(i * GRID_W, GRID_W)
        for p in range(NA_WIDTH // LANES):
            cols = slice(p * LANES, (p + 1) * LANES)
            q2 = q_ref[pl.ds(qoff, GRID_W), cols]
            k2 = kbuf[pl.ds(off, blk), cols]
            v2 = vbuf[pl.ds(off, blk), cols]
            outs = []
            for half in range(2):
                qm = jnp.where(low if half == 0 else jnp.logical_not(low), q2, jnp.zeros_like(q2))
                s = lax.dot_general(qm, k2, _NT, preferred_element_type=F32)
                s = s + bias_ref[d0, 2 * p + half]
                m = jnp.max(s, axis=-1, keepdims=True)
                e = jnp.exp(s - m)
                prob = e * (1.0 / jnp.sum(e, axis=-1, keepdims=True))
                outs.append(jnp.dot(prob.astype(BF16), v2, preferred_element_type=F32))
            o_ref[pl.ds(qoff, GRID_W), cols] = jnp.where(low, outs[0], outs[1]).astype(BF16)
        return carry

    lax.fori_loop(0, NA_WIN_H, row_body, 0)


def _na_bias_table(rpb):
    c = jnp.arange(GRID_W)
    cs = jnp.clip(c - NA_WIN_W // 2, 0, GRID_W - NA_WIN_W)
    cc = jnp.arange(GRID_W)
    inwin = (cc[None, :] >= cs[:, None]) & (cc[None, :] < cs[:, None] + NA_WIN_W)
    dc = jnp.clip(cc[None, :] - c[:, None] + (NA_WIN_W - 1), 0, 2 * NA_WIN_W - 2)
    full = jnp.where(inwin[None, None], rpb[:, :, dc].astype(F32), NEG_BIG)
    tbl = jnp.stack([full[:, d0:d0 + NA_WIN_H] for d0 in range(NA_WIN_H)])
    return tbl.transpose(0, 1, 3, 2, 4).reshape(NA_WIN_H, NA_HEADS, GRID_W, NA_WIN_H * GRID_W)


def _na_attention(q, k, v, bias_tbl, *, seq):
    n = q.shape[0]
    rows = seq // GRID_W
    assert rows % NA_WIN_H == 0 and rows >= 2 * NA_WIN_H
    nrb = rows // NA_WIN_H
    blk = NA_WIN_H * GRID_W

    def cur(g):
        return (g, 0)

    def prev(g):
        return (g - jnp.where(g % nrb == 0, 0, 1), 0)

    def nxt(g):
        return (g + jnp.where(g % nrb == nrb - 1, 0, 1), 0)

    spec = lambda f: pl.BlockSpec((blk, NA_WIDTH), f)
    return pl.pallas_call(
        functools.partial(_na_kernel, rows=rows, nrb=nrb),
        grid=(n // blk,),
        in_specs=[spec(cur), spec(prev), spec(cur), spec(nxt), spec(prev), spec(cur), spec(nxt),
                  pl.BlockSpec(bias_tbl.shape, lambda g: (0, 0, 0, 0))],
        out_specs=spec(cur),
        out_shape=jax.ShapeDtypeStruct((n, NA_WIDTH), BF16),
        scratch_shapes=[pltpu.VMEM((3 * blk, NA_WIDTH), BF16), pltpu.VMEM((3 * blk, NA_WIDTH), BF16)],
        compiler_params=_params("parallel"),
        name="na_attention",
    )(q, k, k, k, v, v, v, bias_tbl)


def _gqa_kernel(q_ref, k_ref, vt_ref, o_ref, qs_sc, m_sc, l_sc, acc_sc, *, tq, tk, seq):
    lane = lax.broadcasted_iota(I32, (tq, LANES), 1)
    low = lane < HEAD_DIM
    for p in range(HEAD_PAIRS):
        q2 = q_ref[:, p * LANES:(p + 1) * LANES]
        zero = jnp.zeros_like(q2)
        qs_sc[(2 * p) * tq:(2 * p + 1) * tq, :] = jnp.where(low, q2, zero)
        qs_sc[(2 * p + 1) * tq:(2 * p + 2) * tq, :] = jnp.where(low, zero, q2)
    m_sc[...] = jnp.full(m_sc.shape, -jnp.inf, F32)
    l_sc[...] = jnp.zeros(l_sc.shape, F32)
    acc_sc[...] = jnp.zeros(acc_sc.shape, F32)

    def body(c, carry):
        koff = pl.multiple_of(c * tk, tk)
        st = lax.dot_general(k_ref[pl.ds(koff, tk), :], qs_sc[...], _NT, preferred_element_type=F32)
        m_old = m_sc[...]
        m_new = jnp.maximum(m_old, jnp.max(st, axis=0, keepdims=True))
        alpha = jnp.exp2(m_old - m_new)
        e = jnp.exp2(st - m_new)
        l_sc[...] = alpha * l_sc[...] + jnp.sum(e, axis=0, keepdims=True)
        pv = jnp.dot(vt_ref[:, pl.ds(koff, tk)], e.astype(BF16), preferred_element_type=F32)
        acc_sc[...] = alpha * acc_sc[...] + pv
        m_sc[...] = m_new
        return carry

    lax.fori_loop(0, seq // tk, body, 0)
    o_t = acc_sc[...] * (1.0 / l_sc[...])
    top = lax.broadcasted_iota(I32, (KV_WIDTH, tq), 0) < HEAD_DIM
    for p in range(HEAD_PAIRS):
        a = o_t[:, (2 * p) * tq:(2 * p + 1) * tq]
        b = o_t[:, (2 * p + 1) * tq:(2 * p + 2) * tq]
        o_ref[:, p * LANES:(p + 1) * LANES] = jnp.where(top, a, b).T.astype(BF16)


def _gqa_attention(q, k, vt, *, seq, tq, tk):
    n = q.shape[0]
    assert seq % tq == 0 and seq % tk == 0
    qblocks = seq // tq
    width = GQA_Q_HEADS * tq
    return pl.pallas_call(
        functools.partial(_gqa_kernel, tq=tq, tk=tk, seq=seq),
        grid=(n // seq, qblocks),
        in_specs=[
            pl.BlockSpec((tq, GQA_WIDTH), lambda b, i: (b * qblocks + i, 0)),
            pl.BlockSpec((seq, KV_WIDTH), lambda b, i: (b, 0)),
            pl.BlockSpec((KV_WIDTH, seq), lambda b, i: (b, 0)),
        ],
        out_specs=pl.BlockSpec((tq, GQA_WIDTH), lambda b, i: (b * qblocks + i, 0)),
        out_shape=jax.ShapeDtypeStruct((n, GQA_WIDTH), BF16),
        scratch_shapes=[pltpu.VMEM((width, KV_WIDTH), BF16), pltpu.VMEM((1, width), F32),
                        pltpu.VMEM((1, width), F32), pltpu.VMEM((KV_WIDTH, width), F32)],
        compiler_params=_params("parallel", "parallel"),
        name="gqa_attention",
    )(q, k, vt)


def _layer_norm(h, g, b):
    mu = jnp.mean(h, axis=-1, keepdims=True)
    hc = h - mu
    var = jnp.mean(hc * hc, axis=-1, keepdims=True)
    return hc * lax.rsqrt(var + LN_EPS) * g + b


def _merge_kernel(na_ref, gq_ref, gate_ref, x_ref, wna_ref, wgq_ref, wout_ref, g_ref, b_ref,
                  wrh_ref, wrl_ref, x1_ref, aff_ref, *, alpha, d_model):
    y_na = jnp.dot(na_ref[...], wna_ref[...], preferred_element_type=F32)
    y_gq = jnp.dot(gq_ref[...], wgq_ref[...], preferred_element_type=F32)
    mixin = gate_ref[:, :d_model] * y_na + gate_ref[:, d_model:] * y_gq
    mix = jnp.dot(mixin.astype(BF16), wout_ref[...], preferred_element_type=F32)
    x1 = _layer_norm(alpha * x_ref[...] + mix, g_ref[...], b_ref[...])
    x1_ref[...] = x1
    hi = x1.astype(BF16)
    lo = (x1 - hi.astype(F32)).astype(BF16)
    wh = wrh_ref[...]
    logits = (lax.dot_general(wh, hi, _NT, preferred_element_type=F32)
              + lax.dot_general(wh, lo, _NT, preferred_element_type=F32)
              + lax.dot_general(wrl_ref[...], hi, _NT, preferred_element_type=F32))
    m = jnp.max(logits, axis=0, keepdims=True)
    e = jnp.exp(logits - m)
    aff_ref[...] = e / jnp.sum(e, axis=0, keepdims=True)


def _merge(na, gq, gates, x2d, wna, wgq, wout, ln_g, ln_b, wr_hi, wr_lo, *, alpha, tm):
    n, d = x2d.shape
    n_exp = wr_hi.shape[0]
    row = lambda i: (i, 0)
    const = lambda i: (0, 0)
    return pl.pallas_call(
        functools.partial(_merge_kernel, alpha=alpha, d_model=d),
        grid=(n // tm,),
        in_specs=[
            pl.BlockSpec((tm, NA_WIDTH), row), pl.BlockSpec((tm, GQA_WIDTH), row),
            pl.BlockSpec((tm, 2 * d), row), pl.BlockSpec((tm, d), row),
            pl.BlockSpec((NA_WIDTH, d), const), pl.BlockSpec((GQA_WIDTH, d), const),
            pl.BlockSpec((d, d), const), pl.BlockSpec((1, d), const), pl.BlockSpec((1, d), const),
            pl.BlockSpec((n_exp, d), const), pl.BlockSpec((n_exp, d), const),
        ],
        out_specs=[pl.BlockSpec((tm, d), row), pl.BlockSpec((n_exp, tm), lambda i: (0, i))],
        out_shape=[jax.ShapeDtypeStruct((n, d), F32), jax.ShapeDtypeStruct((n_exp, n), F32)],
        compiler_params=_params("parallel"),
        name="merge_ln_router",
    )(na, gq, gates, x2d, wna, wgq, wout, ln_g, ln_b, wr_hi, wr_lo)


def _route_kernel(aff_ref, src_ref, gate_ref, cnt_ref, *, cap, count_chunk):
    n_exp, n = aff_ref.shape
    capf = float(cap)

    def bits_at(off, width):
        return lax.bitcast_convert_type(aff_ref[:, pl.ds(off, width)], I32)

    def count_ge(cand):
        def inner(c, acc):
            b = bits_at(pl.multiple_of(c * count_chunk, count_chunk), count_chunk)
            return acc + jnp.where(b >= cand, 1.0, 0.0)
        acc = lax.fori_loop(0, n // count_chunk, inner, jnp.zeros((n_exp, count_chunk), F32))
        return jnp.sum(acc, axis=1, keepdims=True)

    def bisect(i, prefix):
        cand = prefix | jnp.left_shift(jnp.int32(1), 30 - i)
        return jnp.where(count_ge(cand) >= capf, cand, prefix)

    thr = lax.fori_loop(0, 31, bisect, jnp.zeros((n_exp, 1), I32))
    need = capf - count_ge(thr + 1)

    ri = lax.broadcasted_iota(I32, (LANES, LANES), 0)
    ci = lax.broadcasted_iota(I32, (LANES, LANES), 1)
    upper = jnp.where(ri < ci, 1.0, 0.0).astype(BF16)
    ones = jnp.ones((LANES, LANES), BF16)
    er = lax.broadcasted_iota(I32, (n_exp, n_exp), 0)
    ec = lax.broadcasted_iota(I32, (n_exp, n_exp), 1)
    lower = jnp.where(ec < er, 1.0, 0.0).astype(BF16)
    rowbase = (lax.broadcasted_iota(I32, (n_exp, LANES), 0) * cap).astype(F32)

    def body(c, carry):
        ceq, csel = carry
        off = pl.multiple_of(c * LANES, LANES)
        a = aff_ref[:, pl.ds(off, LANES)]
        b = lax.bitcast_convert_type(a, I32)
        eq = b == thr
        eqb = jnp.where(eq, 1.0, 0.0).astype(BF16)
        eqrank = jnp.dot(eqb, upper, preferred_element_type=F32) + ceq
        sel = (b > thr) | (eq & (eqrank < need))
        selb = jnp.where(sel, 1.0, 0.0).astype(BF16)
        pos = jnp.dot(selb, upper, preferred_element_type=F32) + csel
        rank = jnp.dot(lower, selb, preferred_element_type=F32)
        flat = pos + rowbase
        src_rows, gate_rows = [], []
        for r in range(n_exp):
            hit = sel & (rank == float(r))
            src_rows.append(jnp.sum(jnp.where(hit, flat, 0.0), axis=0, keepdims=True))
            gate_rows.append(jnp.sum(jnp.where(hit, a, 0.0), axis=0, keepdims=True))
        src_ref[:, pl.ds(off, LANES)] = jnp.concatenate(src_rows, axis=0).astype(I32)
        gate_ref[:, pl.ds(off, LANES)] = jnp.concatenate(gate_rows, axis=0)
        cnt_ref[:, pl.ds(off, LANES)] = jnp.sum(jnp.where(sel, 1.0, 0.0), axis=0, keepdims=True).astype(I32)
        return (ceq + jnp.dot(eqb, ones, preferred_element_type=F32),
                csel + jnp.dot(selb, ones, preferred_element_type=F32))

    zero = jnp.zeros((n_exp, LANES), F32)
    lax.fori_loop(0, n // LANES, body, (zero, zero))


def _route(aff, *, cap):
    n_exp, n = aff.shape
    count_chunk = min(2048, n)
    assert n % count_chunk == 0 and n % LANES == 0
    full = lambda shape: pl.BlockSpec(shape, lambda i: (0, 0))
    return pl.pallas_call(
        functools.partial(_route_kernel, cap=cap, count_chunk=count_chunk),
        grid=(1,),
        in_specs=[full((n_exp, n))],
        out_specs=[full((n_exp, n)), full((n_exp, n)), full((1, n))],
        out_shape=[jax.ShapeDtypeStruct((n_exp, n), I32), jax.ShapeDtypeStruct((n_exp, n), F32),
                   jax.ShapeDtypeStruct((1, n), I32)],
        compiler_params=_params("arbitrary"),
        name="route",
    )(aff)


def _row_copy(src_ref, src_row, dst_ref, dst_row, sem):
    return pltpu.make_async_copy(src_ref.at[pl.ds(src_row, 1)], dst_ref.at[pl.ds(dst_row, 1)], sem)


def _wait_rows(hbm_ref, total, sem):
    d = hbm_ref.shape[1]
    assert d % SUBLANES == 0 and hbm_ref.dtype == F32

    @pl.when(total > 0)
    def _():
        view = hbm_ref.at[pl.ds(0, pl.multiple_of(total * SUBLANES, SUBLANES)), pl.ds(0, d // SUBLANES)]
        pltpu.make_async_copy(view, view, sem).wait()


def _for_each_pick(cnt_ref, tile, start_pick):
    def per_token(t, carry):
        total, most = carry
        c = cnt_ref[0, t]
        for r in range(UNROLLED_PICKS):
            @pl.when(c > r)
            def _():
                start_pick(t, r)

        def per_pick(r, inner):
            start_pick(t, r)
            return inner

        lax.fori_loop(UNROLLED_PICKS, c, per_pick, 0)
        return total + c, jnp.maximum(most, c)

    return lax.fori_loop(0, tile, per_token, (jnp.int32(0), jnp.int32(0)))


def _dispatch_kernel(cnt_ref, src_ref, x_ref, xe_ref, sem, *, tile):
    def start_pick(t, r):
        _row_copy(x_ref, t, xe_ref, src_ref[r, t], sem).start()

    total, _ = _for_each_pick(cnt_ref, tile, start_pick)
    _wait_rows(xe_ref, total, sem)


def _dispatch(x1, cnt, src, *, n_slots, tile):
    n, d = x1.shape
    n_exp = src.shape[0]
    return pl.pallas_call(
        functools.partial(_dispatch_kernel, tile=tile),
        grid=(n // tile,),
        in_specs=[
            pl.BlockSpec((1, tile), lambda i: (0, i), memory_space=pltpu.SMEM),
            pl.BlockSpec((n_exp, tile), lambda i: (0, i), memory_space=pltpu.SMEM),
            pl.BlockSpec((tile, d), lambda i: (i, 0)),
        ],
        out_specs=pl.BlockSpec(memory_space=pl.ANY),
        out_shape=jax.ShapeDtypeStruct((n_slots, d), F32),
        scratch_shapes=[pltpu.SemaphoreType.DMA(())],
        compiler_params=_params("arbitrary"),
        name="dispatch",
    )(cnt, src, x1)


def _ffn_kernel(xe_ref, wg_ref, wu_ref, wd_ref, ye_ref, *, f_chunk):
    xb = xe_ref[...].astype(BF16)
    d_ff = wg_ref.shape[2]
    acc = jnp.zeros(ye_ref.shape, F32)
    for c in range(d_ff // f_chunk):
        cols = slice(c * f_chunk, (c + 1) * f_chunk)
        g = jnp.dot(xb, wg_ref[0, :, cols], preferred_element_type=F32)
        u = jnp.dot(xb, wu_ref[0, :, cols], preferred_element_type=F32)
        h = (g * jax.nn.sigmoid(g)) * u
        acc = acc + jnp.dot(h.astype(BF16), wd_ref[0, cols, :], preferred_element_type=F32)
    ye_ref[...] = acc


def _ffn(xe, wg, wu, wd, *, cap, tm, f_chunk):
    n_slots, d = xe.shape
    n_exp, _, d_ff = wg.shape
    assert cap % tm == 0 and d_ff % f_chunk == 0
    kblocks = cap // tm
    return pl.pallas_call(
        functools.partial(_ffn_kernel, f_chunk=f_chunk),
        grid=(n_exp, kblocks),
        in_specs=[
            pl.BlockSpec((tm, d), lambda e, k: (e * kblocks + k, 0)),
            pl.BlockSpec((1, d, d_ff), lambda e, k: (e, 0, 0)),
            pl.BlockSpec((1, d, d_ff), lambda e, k: (e, 0, 0)),
            pl.BlockSpec((1, d_ff, d), lambda e, k: (e, 0, 0)),
        ],
        out_specs=pl.BlockSpec((tm, d), lambda e, k: (e * kblocks + k, 0)),
        out_shape=jax.ShapeDtypeStruct((n_slots, d), F32),
        compiler_params=_params("parallel", "parallel"),
        name="expert_ffn",
    )(xe, wg, wu, wd)


def _combine_kernel(cnt_ref, src_ref, x_ref, gate_ref, cntc_ref, g_ref, b_ref, ye_ref, o_ref,
                    buf, sem, *, tile, alpha):
    n_exp = src_ref.shape[0]

    def start_pick(t, r):
        _row_copy(ye_ref, src_ref[r, t], buf.at[r], t, sem).start()

    total, most = _for_each_pick(cnt_ref, tile, start_pick)
    _wait_rows(ye_ref, total, sem)

    o_ref[...] = alpha * x_ref[...]
    cntc = cntc_ref[...]
    for r in range(n_exp):
        @pl.when(most > r)
        def _():
            contrib = buf[r] * gate_ref[:, r:r + 1]
            o_ref[...] += jnp.where(cntc > r, contrib, 0.0)

    o_ref[...] = _layer_norm(o_ref[...], g_ref[...], b_ref[...])


def _combine(x1, cnt, src, gate_t, cnt_col, ln_g, ln_b, ye, *, alpha, tile):
    n, d = x1.shape
    n_exp = src.shape[0]
    row = lambda i: (i, 0)
    const = lambda i: (0, 0)
    return pl.pallas_call(
        functools.partial(_combine_kernel, tile=tile, alpha=alpha),
        grid=(n // tile,),
        in_specs=[
            pl.BlockSpec((1, tile), lambda i: (0, i), memory_space=pltpu.SMEM),
            pl.BlockSpec((n_exp, tile), lambda i: (0, i), memory_space=pltpu.SMEM),
            pl.BlockSpec((tile, d), row), pl.BlockSpec((tile, n_exp), row), pl.BlockSpec((tile, 1), row),
            pl.BlockSpec((1, d), const), pl.BlockSpec((1, d), const),
            pl.BlockSpec(memory_space=pl.ANY),
        ],
        out_specs=pl.BlockSpec((tile, d), row),
        out_shape=jax.ShapeDtypeStruct((n, d), F32),
        scratch_shapes=[pltpu.VMEM((n_exp, tile, d), F32), pltpu.SemaphoreType.DMA(())],
        compiler_params=_params("arbitrary"),
        name="combine_ln",
    )(cnt, src, x1, gate_t, cnt_col, ln_g, ln_b, ye)


def _rope_tables(seq):
    t = jnp.arange(seq)
    row = (t // GRID_W).astype(F32)
    col = (t % GRID_W).astype(F32)
    half = HEAD_DIM // 2
    inv_freq = ROPE_THETA ** (-jnp.arange(0, half, 2, dtype=F32) / half)
    ang_r = row[:, None] * inv_freq[None, :]
    ang_c = col[:, None] * inv_freq[None, :]
    ang = jnp.concatenate([ang_r, ang_r, ang_c, ang_c], axis=-1)
    sign = jnp.where((jnp.arange(HEAD_DIM) % half) < half // 2, -1.0, 1.0).astype(F32)
    reps = LANES // HEAD_DIM
    return jnp.tile(jnp.cos(ang), (1, reps)), jnp.tile(jnp.sin(ang) * sign[None, :], (1, reps))


def _gqa_slot_columns():
    heads = [p + GQA_GROUP * half for p in range(HEAD_PAIRS) for half in range(2)]
    return np.concatenate([np.arange(HEAD_DIM) + HEAD_DIM * h for h in heads])


def _prep_layer(w_in, na_rpb, q_norm, k_norm, w_br_na, w_br_gqa, w_out, ln1_g, ln1_b,
                w_router, w_e_gate, w_e_up, w_e_down, ln2_g, ln2_b):
    d = w_in.shape[0]
    s_na = 3 * NA_WIDTH
    cols = _gqa_slot_columns()
    perm = np.concatenate([np.arange(s_na), s_na + cols, np.arange(s_na + GQA_WIDTH, w_in.shape[1])])
    reps = LANES // HEAD_DIM
    wr_t = w_router.T.astype(F32)
    wr_hi = wr_t.astype(BF16)
    gm = np.kron(np.eye(reps, dtype=np.float32), np.full((HEAD_DIM, HEAD_DIM), 1.0 / HEAD_DIM, np.float32))
    return dict(
        w_in=w_in[:, perm].astype(BF16),
        bias_tbl=_na_bias_table(na_rpb),
        qn=jnp.tile(q_norm.astype(F32), reps)[None, :], kn=jnp.tile(k_norm.astype(F32), reps)[None, :],
        gm=jnp.asarray(gm, BF16),
        wna=w_br_na.astype(BF16), wgq=w_br_gqa[cols].astype(BF16), wout=w_out.astype(BF16),
        ln1_g=ln1_g.astype(F32).reshape(1, d), ln1_b=ln1_b.astype(F32).reshape(1, d),
        wr_hi=wr_hi, wr_lo=(wr_t - wr_hi.astype(F32)).astype(BF16),
        wg=w_e_gate.astype(BF16), wu=w_e_up.astype(BF16), wd=w_e_down.astype(BF16),
        ln2_g=ln2_g.astype(F32).reshape(1, d), ln2_b=ln2_b.astype(F32).reshape(1, d),
    )


def _tiles(seq):
    return dict(tm_proj=512, tq=256, tk=512, tm_merge=256, t_dispatch=256, tm_ffn=512, f_chunk=512,
                t_combine=128)


def _trunk_layer(x2d, p, *, seq, alpha, rope):
    n, d = x2d.shape
    n_exp = p["wg"].shape[0]
    cap = EC_CAPACITY * n // n_exp
    tl = _tiles(seq)
    cos, sin = rope
    naq, nak, nav, gq, gk, gv, gates = _inproj(x2d, p["w_in"], cos, sin, p["qn"], p["kn"], p["gm"],
                                               seq=seq, tm=min(tl["tm_proj"], seq))
    na = _na_attention(naq, nak, nav, p["bias_tbl"], seq=seq)
    gvt = gv.reshape(n // seq, seq, KV_WIDTH).transpose(0, 2, 1).reshape(n // seq * KV_WIDTH, seq)
    ga = _gqa_attention(gq, gk, gvt, seq=seq, tq=min(tl["tq"], seq), tk=min(tl["tk"], seq))
    x1, aff = _merge(na, ga, gates, x2d, p["wna"], p["wgq"], p["wout"], p["ln1_g"], p["ln1_b"],
                     p["wr_hi"], p["wr_lo"], alpha=alpha, tm=tl["tm_merge"])
    src, gate, cnt = _route(aff, cap=cap)
    xe = _dispatch(x1, cnt, src, n_slots=n_exp * cap, tile=tl["t_dispatch"])
    ye = _ffn(xe, p["wg"], p["wu"], p["wd"], cap=cap, tm=min(tl["tm_ffn"], cap), f_chunk=tl["f_chunk"])
    return _combine(x1, cnt, src, gate.T, cnt.reshape(n, 1), p["ln2_g"], p["ln2_b"], ye,
                    alpha=alpha, tile=tl["t_combine"])


@jax.jit
def kernel(x_prompt, x_sample, w_in, na_rpb, q_norm, k_norm, w_br_na, w_br_gqa, w_out, ln1_g, ln1_b,
           w_router, w_e_gate, w_e_up, w_e_down, ln2_g, ln2_b):
    depth = w_in.shape[0]
    alpha = float((2 * depth) ** 0.25)
    d = x_prompt.shape[-1]
    groups = []
    for x in (x_prompt, x_sample):
        b, s, _ = x.shape
        groups.append(dict(x=x.reshape(b * s, d), shape=x.shape, seq=s, rope=_rope_tables(s)))
    for l in range(depth):
        p = _prep_layer(w_in[l], na_rpb[l], q_norm[l], k_norm[l], w_br_na[l], w_br_gqa[l], w_out[l],
                        ln1_g[l], ln1_b[l], w_router[l], w_e_gate[l], w_e_up[l], w_e_down[l],
                        ln2_g[l], ln2_b[l])
        for g in groups:
            g["x"] = _trunk_layer(g["x"], p, seq=g["seq"], alpha=alpha, rope=g["rope"])
    return tuple(g["x"].reshape(g["shape"]) for g in groups)
```

```python
import functools

import jax
import jax.numpy as jnp
import numpy as np
from jax import lax
from jax.experimental import pallas as pl
from jax.experimental.pallas import tpu as pltpu

F32 = jnp.float32
BF16 = jnp.bfloat16
I32 = jnp.int32

GRID_W = 64
HEAD_DIM = 64
NA_HEADS = 8
NA_WIN_H = 8
NA_WIN_W = 16
GQA_Q_HEADS = 8
GQA_KV_HEADS = 2
GQA_GROUP = GQA_Q_HEADS // GQA_KV_HEADS
ROPE_THETA = 10000.0
EC_CAPACITY = 2
LN_EPS = 1e-5
RMS_EPS = 1e-6

LANES = 128
SUBLANES = 8
NA_WIDTH = NA_HEADS * HEAD_DIM
GQA_WIDTH = GQA_Q_HEADS * HEAD_DIM
KV_WIDTH = GQA_KV_HEADS * HEAD_DIM
HEAD_PAIRS = GQA_WIDTH // LANES
BF16_TILE_ROWS = 2 * SUBLANES
VT_ROWS = KV_WIDTH + BF16_TILE_ROWS
UNROLLED_PICKS = 4
GQA_Q_SCALE = HEAD_DIM ** -0.5 * float(np.log2(np.e))
NEG_BIG = -1e30
VMEM_LIMIT_BYTES = 56 * 1024 * 1024

_NT = (((1,), (1,)), ((), ()))


def _params(*sem):
    return pltpu.CompilerParams(dimension_semantics=sem, vmem_limit_bytes=VMEM_LIMIT_BYTES)


def _inproj_kernel(x_ref, w_ref, cos_ref, sin_ref, qn_ref, kn_ref, gm_ref,
                   naq_ref, nak_ref, nav_ref, gq_ref, gk_ref, gv_ref, gate_ref, *, d_model):
    xb = x_ref[...].astype(BF16)
    s_na = 3 * NA_WIDTH
    s_gq = s_na + GQA_WIDTH
    s_gk = s_gq + KV_WIDTH
    s_gv = s_gk + KV_WIDTH

    def proj(c0, width):
        return jnp.dot(xb, w_ref[:, c0:c0 + width], preferred_element_type=F32)

    naq_ref[...] = (proj(0, NA_WIDTH) * HEAD_DIM ** -0.5).astype(BF16)
    nak_ref[...] = proj(NA_WIDTH, NA_WIDTH).astype(BF16)
    nav_ref[...] = proj(2 * NA_WIDTH, NA_WIDTH).astype(BF16)

    cos = cos_ref[...]
    sin = sin_ref[...]
    gm = gm_ref[...]
    lane = lax.broadcasted_iota(I32, cos.shape, 1)
    first_half = (lane % (HEAD_DIM // 2)) < (HEAD_DIM // 4)

    def norm_rope(a, gain):
        sq = a * a
        hi = sq.astype(BF16)
        lo = (sq - hi.astype(F32)).astype(BF16)
        ms = jnp.dot(hi, gm, preferred_element_type=F32) + jnp.dot(lo, gm, preferred_element_type=F32)
        an = a * lax.rsqrt(ms + RMS_EPS) * gain
        quarter = HEAD_DIM // 4
        rot = jnp.where(first_half, pltpu.roll(an, LANES - quarter, 1), pltpu.roll(an, quarter, 1))
        return an * cos + rot * sin

    qn = qn_ref[...]
    for p in range(HEAD_PAIRS):
        a = proj(s_na + p * LANES, LANES)
        gq_ref[:, p * LANES:(p + 1) * LANES] = (norm_rope(a, qn) * GQA_Q_SCALE).astype(BF16)
    gk_ref[...] = norm_rope(proj(s_gq, KV_WIDTH), kn_ref[...]).astype(BF16)
    gv_ref[...] = proj(s_gk, KV_WIDTH).astype(BF16)
    gate_chunk = 512
    for c in range(2 * d_model // gate_chunk):
        g = proj(s_gv + c * gate_chunk, gate_chunk)
        gate_ref[:, c * gate_chunk:(c + 1) * gate_chunk] = jax.nn.sigmoid(g)


def _inproj(x2d, w_in_b, cos, sin, qn, kn, gm, *, seq, tm):
    n, d = x2d.shape
    d_in = w_in_b.shape[1]
    assert n % tm == 0 and seq % tm == 0
    sblocks = seq // tm
    row = lambda i: (i, 0)
    const = lambda i: (0, 0)
    pos = lambda i: (i % sblocks, 0)
    out_shape = [
        jax.ShapeDtypeStruct((n, NA_WIDTH), BF16), jax.ShapeDtypeStruct((n, NA_WIDTH), BF16),
        jax.ShapeDtypeStruct((n, NA_WIDTH), BF16), jax.ShapeDtypeStruct((n, GQA_WIDTH), BF16),
        jax.ShapeDtypeStruct((n, KV_WIDTH), BF16), jax.ShapeDtypeStruct((n, KV_WIDTH), BF16),
        jax.ShapeDtypeStruct((n, 2 * d), F32),
    ]
    out_specs = [
        pl.BlockSpec((tm, NA_WIDTH), row), pl.BlockSpec((tm, NA_WIDTH), row), pl.BlockSpec((tm, NA_WIDTH), row),
        pl.BlockSpec((tm, GQA_WIDTH), row), pl.BlockSpec((tm, KV_WIDTH), row), pl.BlockSpec((tm, KV_WIDTH), row),
        pl.BlockSpec((tm, 2 * d), row),
    ]
    return pl.pallas_call(
        functools.partial(_inproj_kernel, d_model=d),
        grid=(n // tm,),
        in_specs=[
            pl.BlockSpec((tm, d), row), pl.BlockSpec((d, d_in), const),
            pl.BlockSpec((tm, LANES), pos), pl.BlockSpec((tm, LANES), pos),
            pl.BlockSpec((1, LANES), const), pl.BlockSpec((1, LANES), const),
            pl.BlockSpec((LANES, LANES), const),
        ],
        out_specs=out_specs,
        out_shape=out_shape,
        compiler_params=_params("parallel"),
        name="inproj",
    )(x2d, w_in_b, cos, sin, qn, kn, gm)


def _na_kernel(q_ref, kp_ref, kc_ref, kn_ref, vp_ref, vc_ref, vn_ref, bias_ref, o_ref,
               kbuf, vbuf, *, rows, nrb):
    blk = NA_WIN_H * GRID_W
    j = pl.program_id(0) % nrb
    kbuf[0:blk] = kp_ref[...]
    kbuf[blk:2 * blk] = kc_ref[...]
    kbuf[2 * blk:3 * blk] = kn_ref[...]
    vbuf[0:blk] = vp_ref[...]
    vbuf[blk:2 * blk] = vc_ref[...]
    vbuf[2 * blk:3 * blk] = vn_ref[...]
    lane = lax.broadcasted_iota(I32, (GRID_W, LANES), 1)
    low = lane < HEAD_DIM

    def row_body(i, carry):
        r = j * NA_WIN_H + i
        rs = jnp.clip(r - NA_WIN_H // 2, 0, rows - NA_WIN_H)
        d0 = rs - r + (NA_WIN_H - 1)
        off = pl.multiple_of((rs - (j - 1) * NA_WIN_H) * GRID_W, GRID_W)
        qoff = pl.multiple_of(i * GRID_W, GRID_W)
        for p in range(NA_WIDTH // LANES):
            cols = slice(p * LANES, (p + 1) * LANES)
            q2 = q_ref[pl.ds(qoff, GRID_W), cols]
            k2 = kbuf[pl.ds(off, blk), cols]
            v2 = vbuf[pl.ds(off, blk), cols]
            zero = jnp.zeros_like(q2)
            qs = jnp.concatenate([jnp.where(low, q2, zero), jnp.where(low, zero, q2)], axis=0)
            st = lax.dot_general(k2, qs, _NT, preferred_element_type=F32) + bias_ref[d0, p]
            e = jnp.exp(st - jnp.max(st, axis=0, keepdims=True))
            prob = e * (1.0 / jnp.sum(e, axis=0, keepdims=True))
            o = jnp.dot(prob.T.astype(BF16), v2, preferred_element_type=F32)
            o_ref[pl.ds(qoff, GRID_W), cols] = jnp.where(low, o[:GRID_W], o[GRID_W:]).astype(BF16)
        return carry

    lax.fori_loop(0, NA_WIN_H, row_body, 0)


def _na_bias_table(rpb):
    c = jnp.arange(GRID_W)
    cs = jnp.clip(c - NA_WIN_W // 2, 0, GRID_W - NA_WIN_W)
    cc = jnp.arange(GRID_W)
    inwin = (cc[None, :] >= cs[:, None]) & (cc[None, :] < cs[:, None] + NA_WIN_W)
    dc = jnp.clip(cc[None, :] - c[:, None] + (NA_WIN_W - 1), 0, 2 * NA_WIN_W - 2)
    full = jnp.where(inwin[None, None], rpb[:, :, dc].astype(F32), NEG_BIG)
    tbl = jnp.stack([full[:, d0:d0 + NA_WIN_H] for d0 in range(NA_WIN_H)])
    tbl = tbl.reshape(NA_WIN_H, NA_HEADS // 2, 2, NA_WIN_H, GRID_W, GRID_W)
    return tbl.transpose(0, 1, 3, 5, 2, 4).reshape(NA_WIN_H, NA_HEADS // 2, NA_WIN_H * GRID_W, 2 * GRID_W)


def _na_attention(q, k, v, bias_tbl, *, seq):
    n = q.shape[0]
    rows = seq // GRID_W
    assert rows % NA_WIN_H == 0 and rows >= 2 * NA_WIN_H
    nrb = rows // NA_WIN_H
    blk = NA_WIN_H * GRID_W

    def cur(g):
        return (g, 0)

    def prev(g):
        return (g - jnp.where(g % nrb == 0, 0, 1), 0)

    def nxt(g):
        return (g + jnp.where(g % nrb == nrb - 1, 0, 1), 0)

    spec = lambda f: pl.BlockSpec((blk, NA_WIDTH), f)
    return pl.pallas_call(
        functools.partial(_na_kernel, rows=rows, nrb=nrb),
        grid=(n // blk,),
        in_specs=[spec(cur), spec(prev), spec(cur), spec(nxt), spec(prev), spec(cur), spec(nxt),
                  pl.BlockSpec(bias_tbl.shape, lambda g: (0, 0, 0, 0))],
        out_specs=spec(cur),
        out_shape=jax.ShapeDtypeStruct((n, NA_WIDTH), BF16),
        scratch_shapes=[pltpu.VMEM((3 * blk, NA_WIDTH), BF16), pltpu.VMEM((3 * blk, NA_WIDTH), BF16)],
        compiler_params=_params("parallel"),
        name="na_attention",
    )(q, k, k, k, v, v, v, bias_tbl)


def _gqa_kernel(q_ref, k_ref, vt_ref, o_ref, qs_sc, m_sc, acc_sc, *, tq, tk, seq):
    lane = lax.broadcasted_iota(I32, (tq, LANES), 1)
    low = lane < HEAD_DIM
    for p in range(HEAD_PAIRS):
        q2 = q_ref[:, p * LANES:(p + 1) * LANES]
        zero = jnp.zeros_like(q2)
        qs_sc[(2 * p) * tq:(2 * p + 1) * tq, :] = jnp.where(low, q2, zero)
        qs_sc[(2 * p + 1) * tq:(2 * p + 2) * tq, :] = jnp.where(low, zero, q2)
    m_sc[...] = jnp.full(m_sc.shape, -jnp.inf, F32)
    acc_sc[...] = jnp.zeros(acc_sc.shape, F32)

    def body(c, carry):
        koff = pl.multiple_of(c * tk, tk)
        st = lax.dot_general(k_ref[pl.ds(koff, tk), :], qs_sc[...], _NT, preferred_element_type=F32)
        m_old = m_sc[...]
        m_new = jnp.maximum(m_old, jnp.max(st, axis=0, keepdims=True))
        alpha = jnp.exp2(m_old - m_new)
        e = jnp.exp2(st - m_new).astype(BF16)
        pv = jnp.dot(vt_ref[:, pl.ds(koff, tk)], e, preferred_element_type=F32)
        acc_sc[...] = alpha * acc_sc[...] + pv
        m_sc[...] = m_new
        return carry

    lax.fori_loop(0, seq // tk, body, 0)
    o_t = acc_sc[0:KV_WIDTH, :] * (1.0 / acc_sc[KV_WIDTH:KV_WIDTH + 1, :])
    top = lax.broadcasted_iota(I32, (KV_WIDTH, tq), 0) < HEAD_DIM
    for p in range(HEAD_PAIRS):
        a = o_t[:, (2 * p) * tq:(2 * p + 1) * tq]
        b = o_t[:, (2 * p + 1) * tq:(2 * p + 2) * tq]
        o_ref[:, p * LANES:(p + 1) * LANES] = jnp.where(top, a, b).T.astype(BF16)


def _gqa_attention(q, k, vt, *, seq, tq, tk):
    n = q.shape[0]
    assert seq % tq == 0 and seq % tk == 0
    qblocks = seq // tq
    width = GQA_Q_HEADS * tq
    return pl.pallas_call(
        functools.partial(_gqa_kernel, tq=tq, tk=tk, seq=seq),
        grid=(n // seq, qblocks),
        in_specs=[
            pl.BlockSpec((tq, GQA_WIDTH), lambda b, i: (b * qblocks + i, 0)),
            pl.BlockSpec((seq, KV_WIDTH), lambda b, i: (b, 0)),
            pl.BlockSpec((VT_ROWS, seq), lambda b, i: (b, 0)),
        ],
        out_specs=pl.BlockSpec((tq, GQA_WIDTH), lambda b, i: (b * qblocks + i, 0)),
        out_shape=jax.ShapeDtypeStruct((n, GQA_WIDTH), BF16),
        scratch_shapes=[pltpu.VMEM((width, KV_WIDTH), BF16), pltpu.VMEM((1, width), F32),
                        pltpu.VMEM((VT_ROWS, width), F32)],
        compiler_params=_params("parallel", "parallel"),
        name="gqa_attention",
    )(q, k, vt)


def _layer_norm(h, g, b):
    mu = jnp.mean(h, axis=-1, keepdims=True)
    hc = h - mu
    var = jnp.mean(hc * hc, axis=-1, keepdims=True)
    return hc * lax.rsqrt(var + LN_EPS) * g + b


def _merge_kernel(na_ref, gq_ref, gate_ref, x_ref, wna_ref, wgq_ref, wout_ref, g_ref, b_ref,
                  wrh_ref, wrl_ref, x1_ref, aff_ref, *, alpha, d_model):
    y_na = jnp.dot(na_ref[...], wna_ref[...], preferred_element_type=F32)
    y_gq = jnp.dot(gq_ref[...], wgq_ref[...], preferred_element_type=F32)
    mixin = gate_ref[:, :d_model] * y_na + gate_ref[:, d_model:] * y_gq
    mix = jnp.dot(mixin.astype(BF16), wout_ref[...], preferred_element_type=F32)
    x1 = _layer_norm(alpha * x_ref[...] + mix, g_ref[...], b_ref[...])
    x1_ref[...] = x1
    hi = x1.astype(BF16)
    lo = (x1 - hi.astype(F32)).astype(BF16)
    wh = wrh_ref[...]
    logits = (lax.dot_general(wh, hi, _NT, preferred_element_type=F32)
              + lax.dot_general(wh, lo, _NT, preferred_element_type=F32)
              + lax.dot_general(wrl_ref[...], hi, _NT, preferred_element_type=F32))
    m = jnp.max(logits, axis=0, keepdims=True)
    e = jnp.exp(logits - m)
    aff_ref[...] = e / jnp.sum(e, axis=0, keepdims=True)


def _merge(na, gq, gates, x2d, wna, wgq, wout, ln_g, ln_b, wr_hi, wr_lo, *, alpha, tm):
    n, d = x2d.shape
    n_exp = wr_hi.shape[0]
    row = lambda i: (i, 0)
    const = lambda i: (0, 0)
    return pl.pallas_call(
        functools.partial(_merge_kernel, alpha=alpha, d_model=d),
        grid=(n // tm,),
        in_specs=[
            pl.BlockSpec((tm, NA_WIDTH), row), pl.BlockSpec((tm, GQA_WIDTH), row),
            pl.BlockSpec((tm, 2 * d), row), pl.BlockSpec((tm, d), row),
            pl.BlockSpec((NA_WIDTH, d), const), pl.BlockSpec((GQA_WIDTH, d), const),
            pl.BlockSpec((d, d), const), pl.BlockSpec((1, d), const), pl.BlockSpec((1, d), const),
            pl.BlockSpec((n_exp, d), const), pl.BlockSpec((n_exp, d), const),
        ],
        out_specs=[pl.BlockSpec((tm, d), row), pl.BlockSpec((n_exp, tm), lambda i: (0, i))],
        out_shape=[jax.ShapeDtypeStruct((n, d), F32), jax.ShapeDtypeStruct((n_exp, n), F32)],
        compiler_params=_params("parallel"),
        name="merge_ln_router",
    )(na, gq, gates, x2d, wna, wgq, wout, ln_g, ln_b, wr_hi, wr_lo)


def _route_kernel(aff_ref, src_ref, gate_ref, cnt_ref, *, cap, count_chunk):
    n_exp, n = aff_ref.shape
    capf = float(cap)

    def bits_at(off, width):
        return lax.bitcast_convert_type(aff_ref[:, pl.ds(off, width)], I32)

    def count_ge(cand):
        def inner(c, acc):
            b = bits_at(pl.multiple_of(c * count_chunk, count_chunk), count_chunk)
            return acc + jnp.where(b >= cand, 1.0, 0.0)
        acc = lax.fori_loop(0, n // count_chunk, inner, jnp.zeros((n_exp, count_chunk), F32))
        return jnp.sum(acc, axis=1, keepdims=True)

    def bisect(i, prefix):
        cand = prefix | jnp.left_shift(jnp.int32(1), 30 - i)
        return jnp.where(count_ge(cand) >= capf, cand, prefix)

    thr = lax.fori_loop(0, 31, bisect, jnp.zeros((n_exp, 1), I32))
    need = capf - count_ge(thr + 1)

    ri = lax.broadcasted_iota(I32, (LANES, LANES), 0)
    ci = lax.broadcasted_iota(I32, (LANES, LANES), 1)
    upper = jnp.where(ri < ci, 1.0, 0.0).astype(BF16)
    ones = jnp.ones((LANES, LANES), BF16)
    er = lax.broadcasted_iota(I32, (n_exp, n_exp), 0)
    ec = lax.broadcasted_iota(I32, (n_exp, n_exp), 1)
    lower = jnp.where(ec < er, 1.0, 0.0).astype(BF16)
    rowbase = (lax.broadcasted_iota(I32, (n_exp, LANES), 0) * cap).astype(F32)

    def body(c, carry):
        ceq, csel = carry
        off = pl.multiple_of(c * LANES, LANES)
        a = aff_ref[:, pl.ds(off, LANES)]
        b = lax.bitcast_convert_type(a, I32)
        eq = b == thr
        eqb = jnp.where(eq, 1.0, 0.0).astype(BF16)
        eqrank = jnp.dot(eqb, upper, preferred_element_type=F32) + ceq
        sel = (b > thr) | (eq & (eqrank < need))
        selb = jnp.where(sel, 1.0, 0.0).astype(BF16)
        pos = jnp.dot(selb, upper, preferred_element_type=F32) + csel
        rank = jnp.dot(lower, selb, preferred_element_type=F32)
        flat = pos + rowbase
        src_rows, gate_rows = [], []
        for r in range(n_exp):
            hit = sel & (rank == float(r))
            src_rows.append(jnp.sum(jnp.where(hit, flat, 0.0), axis=0, keepdims=True))
            gate_rows.append(jnp.sum(jnp.where(hit, a, 0.0), axis=0, keepdims=True))
        src_ref[:, pl.ds(off, LANES)] = jnp.concatenate(src_rows, axis=0).astype(I32)
        gate_ref[:, pl.ds(off, LANES)] = jnp.concatenate(gate_rows, axis=0)
        cnt_ref[:, pl.ds(off, LANES)] = jnp.sum(jnp.where(sel, 1.0, 0.0), axis=0, keepdims=True).astype(I32)
        return (ceq + jnp.dot(eqb, ones, preferred_element_type=F32),
                csel + jnp.dot(selb, ones, preferred_element_type=F32))

    zero = jnp.zeros((n_exp, LANES), F32)
    lax.fori_loop(0, n // LANES, body, (zero, zero))


def _route(aff, *, cap):
    n_exp, n = aff.shape
    count_chunk = min(2048, n)
    assert n % count_chunk == 0 and n % LANES == 0
    full = lambda shape: pl.BlockSpec(shape, lambda i: (0, 0))
    return pl.pallas_call(
        functools.partial(_route_kernel, cap=cap, count_chunk=count_chunk),
        grid=(1,),
        in_specs=[full((n_exp, n))],
        out_specs=[full((n_exp, n)), full((n_exp, n)), full((1, n))],
        out_shape=[jax.ShapeDtypeStruct((n_exp, n), I32), jax.ShapeDtypeStruct((n_exp, n), F32),
                   jax.ShapeDtypeStruct((1, n), I32)],
        compiler_params=_params("arbitrary"),
        name="route",
    )(aff)


def _row_copy(src_ref, src_row, dst_ref, dst_row, sem):
    return pltpu.make_async_copy(src_ref.at[pl.ds(src_row, 1)], dst_ref.at[pl.ds(dst_row, 1)], sem)


def _wait_rows(hbm_ref, total, sem):
    d = hbm_ref.shape[1]
    assert d % SUBLANES == 0 and hbm_ref.dtype == F32

    @pl.when(total > 0)
    def _():
        view = hbm_ref.at[pl.ds(0, pl.multiple_of(total * SUBLANES, SUBLANES)), pl.ds(0, d // SUBLANES)]
        pltpu.make_async_copy(view, view, sem).wait()


def _for_each_pick(cnt_ref, tile, start_pick):
    def per_token(t, carry):
        total, most = carry
        c = cnt_ref[0, t]
        for r in range(UNROLLED_PICKS):
            @pl.when(c > r)
            def _():
                start_pick(t, r)

        def per_pick(r, inner):
            start_pick(t, r)
            return inner

        lax.fori_loop(UNROLLED_PICKS, c, per_pick, 0)
        return total + c, jnp.maximum(most, c)

    return lax.fori_loop(0, tile, per_token, (jnp.int32(0), jnp.int32(0)))


def _dispatch_kernel(cnt_ref, src_ref, x_ref, xe_ref, sem, *, tile):
    def start_pick(t, r):
        _row_copy(x_ref, t, xe_ref, src_ref[r, t], sem).start()

    total, _ = _for_each_pick(cnt_ref, tile, start_pick)
    _wait_rows(xe_ref, total, sem)


def _dispatch(x1, cnt, src, *, n_slots, tile):
    n, d = x1.shape
    n_exp = src.shape[0]
    return pl.pallas_call(
        functools.partial(_dispatch_kernel, tile=tile),
        grid=(n // tile,),
        in_specs=[
            pl.BlockSpec((1, tile), lambda i: (0, i), memory_space=pltpu.SMEM),
            pl.BlockSpec((n_exp, tile), lambda i: (0, i), memory_space=pltpu.SMEM),
            pl.BlockSpec((tile, d), lambda i: (i, 0)),
        ],
        out_specs=pl.BlockSpec(memory_space=pl.ANY),
        out_shape=jax.ShapeDtypeStruct((n_slots, d), F32),
        scratch_shapes=[pltpu.SemaphoreType.DMA(())],
        compiler_params=_params("arbitrary"),
        name="dispatch",
    )(cnt, src, x1)


def _ffn_kernel(xe_ref, wg_ref, wu_ref, wd_ref, ye_ref, *, f_chunk):
    xb = xe_ref[...].astype(BF16)
    d_ff = wg_ref.shape[2]
    acc = jnp.zeros(ye_ref.shape, F32)
    for c in range(d_ff // f_chunk):
        cols = slice(c * f_chunk, (c + 1) * f_chunk)
        g = jnp.dot(xb, wg_ref[0, :, cols], preferred_element_type=F32)
        u = jnp.dot(xb, wu_ref[0, :, cols], preferred_element_type=F32)
        h = (g * jax.nn.sigmoid(g)) * u
        acc = acc + jnp.dot(h.astype(BF16), wd_ref[0, cols, :], preferred_element_type=F32)
    ye_ref[...] = acc


def _ffn(xe, wg, wu, wd, *, cap, tm, f_chunk):
    n_slots, d = xe.shape
    n_exp, _, d_ff = wg.shape
    assert cap % tm == 0 and d_ff % f_chunk == 0
    kblocks = cap // tm
    return pl.pallas_call(
        functools.partial(_ffn_kernel, f_chunk=f_chunk),
        grid=(n_exp, kblocks),
        in_specs=[
            pl.BlockSpec((tm, d), lambda e, k: (e * kblocks + k, 0)),
            pl.BlockSpec((1, d, d_ff), lambda e, k: (e, 0, 0)),
            pl.BlockSpec((1, d, d_ff), lambda e, k: (e, 0, 0)),
            pl.BlockSpec((1, d_ff, d), lambda e, k: (e, 0, 0)),
        ],
        out_specs=pl.BlockSpec((tm, d), lambda e, k: (e * kblocks + k, 0)),
        out_shape=jax.ShapeDtypeStruct((n_slots, d), F32),
        compiler_params=_params("parallel", "parallel"),
        name="expert_ffn",
    )(xe, wg, wu, wd)


def _combine_kernel(cnt_ref, src_ref, x_ref, gate_ref, cntc_ref, g_ref, b_ref, ye_ref, o_ref,
                    buf, sem, *, tile, alpha):
    n_exp = src_ref.shape[0]

    def start_pick(t, r):
        _row_copy(ye_ref, src_ref[r, t], buf.at[r], t, sem).start()

    total, most = _for_each_pick(cnt_ref, tile, start_pick)
    _wait_rows(ye_ref, total, sem)

    o_ref[...] = alpha * x_ref[...]
    cntc = cntc_ref[...]
    for r in range(n_exp):
        @pl.when(most > r)
        def _():
            contrib = buf[r] * gate_ref[:, r:r + 1]
            o_ref[...] += jnp.where(cntc > r, contrib, 0.0)

    o_ref[...] = _layer_norm(o_ref[...], g_ref[...], b_ref[...])


def _combine(x1, cnt, src, gate_t, cnt_col, ln_g, ln_b, ye, *, alpha, tile):
    n, d = x1.shape
    n_exp = src.shape[0]
    row = lambda i: (i, 0)
    const = lambda i: (0, 0)
    return pl.pallas_call(
        functools.partial(_combine_kernel, tile=tile, alpha=alpha),
        grid=(n // tile,),
        in_specs=[
            pl.BlockSpec((1, tile), lambda i: (0, i), memory_space=pltpu.SMEM),
            pl.BlockSpec((n_exp, tile), lambda i: (0, i), memory_space=pltpu.SMEM),
            pl.BlockSpec((tile, d), row), pl.BlockSpec((tile, n_exp), row), pl.BlockSpec((tile, 1), row),
            pl.BlockSpec((1, d), const), pl.BlockSpec((1, d), const),
            pl.BlockSpec(memory_space=pl.ANY),
        ],
        out_specs=pl.BlockSpec((tile, d), row),
        out_shape=jax.ShapeDtypeStruct((n, d), F32),
        scratch_shapes=[pltpu.VMEM((n_exp, tile, d), F32), pltpu.SemaphoreType.DMA(())],
        compiler_params=_params("arbitrary"),
        name="combine_ln",
    )(cnt, src, x1, gate_t, cnt_col, ln_g, ln_b, ye)


def _rope_tables(seq):
    t = jnp.arange(seq)
    row = (t // GRID_W).astype(F32)
    col = (t % GRID_W).astype(F32)
    half = HEAD_DIM // 2
    inv_freq = ROPE_THETA ** (-jnp.arange(0, half, 2, dtype=F32) / half)
    ang_r = row[:, None] * inv_freq[None, :]
    ang_c = col[:, None] * inv_freq[None, :]
    ang = jnp.concatenate([ang_r, ang_r, ang_c, ang_c], axis=-1)
    sign = jnp.where((jnp.arange(HEAD_DIM) % half) < half // 2, -1.0, 1.0).astype(F32)
    reps = LANES // HEAD_DIM
    return jnp.tile(jnp.cos(ang), (1, reps)), jnp.tile(jnp.sin(ang) * sign[None, :], (1, reps))


def _gqa_slot_columns():
    heads = [p + GQA_GROUP * half for p in range(HEAD_PAIRS) for half in range(2)]
    return np.concatenate([np.arange(HEAD_DIM) + HEAD_DIM * h for h in heads])


def _prep_layer(w_in, na_rpb, q_norm, k_norm, w_br_na, w_br_gqa, w_out, ln1_g, ln1_b,
                w_router, w_e_gate, w_e_up, w_e_down, ln2_g, ln2_b):
    d = w_in.shape[0]
    s_na = 3 * NA_WIDTH
    cols = _gqa_slot_columns()
    perm = np.concatenate([np.arange(s_na), s_na + cols, np.arange(s_na + GQA_WIDTH, w_in.shape[1])])
    reps = LANES // HEAD_DIM
    wr_t = w_router.T.astype(F32)
    wr_hi = wr_t.astype(BF16)
    gm = np.kron(np.eye(reps, dtype=np.float32), np.full((HEAD_DIM, HEAD_DIM), 1.0 / HEAD_DIM, np.float32))
    return dict(
        w_in=w_in[:, perm].astype(BF16),
        bias_tbl=_na_bias_table(na_rpb),
        qn=jnp.tile(q_norm.astype(F32), reps)[None, :], kn=jnp.tile(k_norm.astype(F32), reps)[None, :],
        gm=jnp.asarray(gm, BF16),
        wna=w_br_na.astype(BF16), wgq=w_br_gqa[cols].astype(BF16), wout=w_out.astype(BF16),
        ln1_g=ln1_g.astype(F32).reshape(1, d), ln1_b=ln1_b.astype(F32).reshape(1, d),
        wr_hi=wr_hi, wr_lo=(wr_t - wr_hi.astype(F32)).astype(BF16),
        wg=w_e_gate.astype(BF16), wu=w_e_up.astype(BF16), wd=w_e_down.astype(BF16),
        ln2_g=ln2_g.astype(F32).reshape(1, d), ln2_b=ln2_b.astype(F32).reshape(1, d),
    )


def _tiles(seq):
    return dict(tm_proj=512, tq=256, tk=1024, tm_merge=256, t_dispatch=256, tm_ffn=512, f_chunk=512,
                t_combine=128)


def _trunk_layer(x2d, p, *, seq, alpha, rope):
    n, d = x2d.shape
    n_exp = p["wg"].shape[0]
    cap = EC_CAPACITY * n // n_exp
    tl = _tiles(seq)
    cos, sin = rope
    naq, nak, nav, gq, gk, gv, gates = _inproj(x2d, p["w_in"], cos, sin, p["qn"], p["kn"], p["gm"],
                                               seq=seq, tm=min(tl["tm_proj"], seq))
    na = _na_attention(naq, nak, nav, p["bias_tbl"], seq=seq)
    nb = n // seq
    ones_rows = jnp.zeros((nb, VT_ROWS - KV_WIDTH, seq), BF16).at[:, 0, :].set(1.0)
    gvt = jnp.concatenate([gv.reshape(nb, seq, KV_WIDTH).transpose(0, 2, 1), ones_rows], axis=1)
    gvt = gvt.reshape(nb * VT_ROWS, seq)
    ga = _gqa_attention(gq, gk, gvt, seq=seq, tq=min(tl["tq"], seq), tk=min(tl["tk"], seq))
    x1, aff = _merge(na, ga, gates, x2d, p["wna"], p["wgq"], p["wout"], p["ln1_g"], p["ln1_b"],
                     p["wr_hi"], p["wr_lo"], alpha=alpha, tm=tl["tm_merge"])
    src, gate, cnt = _route(aff, cap=cap)
    xe = _dispatch(x1, cnt, src, n_slots=n_exp * cap, tile=tl["t_dispatch"])
    ye = _ffn(xe, p["wg"], p["wu"], p["wd"], cap=cap, tm=min(tl["tm_ffn"], cap), f_chunk=tl["f_chunk"])
    return _combine(x1, cnt, src, gate.T, cnt.reshape(n, 1), p["ln2_g"], p["ln2_b"], ye,
                    alpha=alpha, tile=tl["t_combine"])


@jax.jit
def kernel(x_prompt, x_sample, w_in, na_rpb, q_norm, k_norm, w_br_na, w_br_gqa, w_out, ln1_g, ln1_b,
           w_router, w_e_gate, w_e_up, w_e_down, ln2_g, ln2_b):
    depth = w_in.shape[0]
    alpha = float((2 * depth) ** 0.25)
    d = x_prompt.shape[-1]
    groups = []
    for x in (x_prompt, x_sample):
        b, s, _ = x.shape
        groups.append(dict(x=x.reshape(b * s, d), shape=x.shape, seq=s, rope=_rope_tables(s)))
    for l in range(depth):
        p = _prep_layer(w_in[l], na_rpb[l], q_norm[l], k_norm[l], w_br_na[l], w_br_gqa[l], w_out[l],
                        ln1_g[l], ln1_b[l], w_router[l], w_e_gate[l], w_e_up[l], w_e_down[l],
                        ln2_g[l], ln2_b[l])
        for g in groups:
            g["x"] = _trunk_layer(g["x"], p, seq=g["seq"], alpha=alpha, rope=g["rope"])
    return tuple(g["x"].reshape(g["shape"]) for g in groups)
```

```python
import functools

import jax
import jax.numpy as jnp
import numpy as np
from jax import lax
from jax.experimental import pallas as pl
from jax.experimental.pallas import tpu as pltpu

F32 = jnp.float32
BF16 = jnp.bfloat16
I32 = jnp.int32

GRID_W = 64
HEAD_DIM = 64
NA_HEADS = 8
NA_WIN_H = 8
NA_WIN_W = 16
GQA_Q_HEADS = 8
GQA_KV_HEADS = 2
GQA_GROUP = GQA_Q_HEADS // GQA_KV_HEADS
ROPE_THETA = 10000.0
EC_CAPACITY = 2
LN_EPS = 1e-5
RMS_EPS = 1e-6

LANES = 128
SUBLANES = 8
NA_WIDTH = NA_HEADS * HEAD_DIM
GQA_WIDTH = GQA_Q_HEADS * HEAD_DIM
KV_WIDTH = GQA_KV_HEADS * HEAD_DIM
HEAD_PAIRS = GQA_WIDTH // LANES
BF16_TILE_ROWS = 2 * SUBLANES
VT_ROWS = KV_WIDTH + BF16_TILE_ROWS
UNPICKED = -(1 << 30)
DISPATCH_FEW_ROWS = 64
GQA_Q_SCALE = HEAD_DIM ** -0.5 * float(np.log2(np.e))
NEG_BIG = -1e30
VMEM_LIMIT_BYTES = 56 * 1024 * 1024

_NT = (((1,), (1,)), ((), ()))


def _params(*sem):
    return pltpu.CompilerParams(dimension_semantics=sem, vmem_limit_bytes=VMEM_LIMIT_BYTES)


def _inproj_kernel(x_ref, w_ref, cos_ref, sin_ref, qn_ref, kn_ref, gm_ref,
                   naq_ref, nak_ref, nav_ref, gq_ref, gk_ref, gv_ref, gate_ref, *, d_model):
    xb = x_ref[...].astype(BF16)
    s_na = 3 * NA_WIDTH
    s_gq = s_na + GQA_WIDTH
    s_gk = s_gq + KV_WIDTH
    s_gv = s_gk + KV_WIDTH

    def proj(c0, width):
        return jnp.dot(xb, w_ref[:, c0:c0 + width], preferred_element_type=F32)

    naq_ref[...] = (proj(0, NA_WIDTH) * HEAD_DIM ** -0.5).astype(BF16)
    nak_ref[...] = proj(NA_WIDTH, NA_WIDTH).astype(BF16)
    nav_ref[...] = proj(2 * NA_WIDTH, NA_WIDTH).astype(BF16)

    cos = cos_ref[...]
    sin = sin_ref[...]
    gm = gm_ref[...]
    lane = lax.broadcasted_iota(I32, cos.shape, 1)
    first_half = (lane % (HEAD_DIM // 2)) < (HEAD_DIM // 4)

    def norm_rope(a, gain):
        sq = a * a
        hi = sq.astype(BF16)
        lo = (sq - hi.astype(F32)).astype(BF16)
        ms = jnp.dot(hi, gm, preferred_element_type=F32) + jnp.dot(lo, gm, preferred_element_type=F32)
        an = a * lax.rsqrt(ms + RMS_EPS) * gain
        quarter = HEAD_DIM // 4
        rot = jnp.where(first_half, pltpu.roll(an, LANES - quarter, 1), pltpu.roll(an, quarter, 1))
        return an * cos + rot * sin

    qn = qn_ref[...]
    for p in range(HEAD_PAIRS):
        a = proj(s_na + p * LANES, LANES)
        gq_ref[:, p * LANES:(p + 1) * LANES] = (norm_rope(a, qn) * GQA_Q_SCALE).astype(BF16)
    gk_ref[...] = norm_rope(proj(s_gq, KV_WIDTH), kn_ref[...]).astype(BF16)
    gv_ref[...] = proj(s_gk, KV_WIDTH).astype(BF16)
    gate_chunk = 512
    for c in range(2 * d_model // gate_chunk):
        g = proj(s_gv + c * gate_chunk, gate_chunk)
        gate_ref[:, c * gate_chunk:(c + 1) * gate_chunk] = jax.nn.sigmoid(g)


def _inproj(x2d, w_in_b, cos, sin, qn, kn, gm, *, seq, tm):
    n, d = x2d.shape
    d_in = w_in_b.shape[1]
    assert n % tm == 0 and seq % tm == 0
    sblocks = seq // tm
    row = lambda i: (i, 0)
    const = lambda i: (0, 0)
    pos = lambda i: (i % sblocks, 0)
    out_shape = [
        jax.ShapeDtypeStruct((n, NA_WIDTH), BF16), jax.ShapeDtypeStruct((n, NA_WIDTH), BF16),
        jax.ShapeDtypeStruct((n, NA_WIDTH), BF16), jax.ShapeDtypeStruct((n, GQA_WIDTH), BF16),
        jax.ShapeDtypeStruct((n, KV_WIDTH), BF16), jax.ShapeDtypeStruct((n, KV_WIDTH), BF16),
        jax.ShapeDtypeStruct((n, 2 * d), F32),
    ]
    out_specs = [
        pl.BlockSpec((tm, NA_WIDTH), row), pl.BlockSpec((tm, NA_WIDTH), row), pl.BlockSpec((tm, NA_WIDTH), row),
        pl.BlockSpec((tm, GQA_WIDTH), row), pl.BlockSpec((tm, KV_WIDTH), row), pl.BlockSpec((tm, KV_WIDTH), row),
        pl.BlockSpec((tm, 2 * d), row),
    ]
    return pl.pallas_call(
        functools.partial(_inproj_kernel, d_model=d),
        grid=(n // tm,),
        in_specs=[
            pl.BlockSpec((tm, d), row), pl.BlockSpec((d, d_in), const),
            pl.BlockSpec((tm, LANES), pos), pl.BlockSpec((tm, LANES), pos),
            pl.BlockSpec((1, LANES), const), pl.BlockSpec((1, LANES), const),
            pl.BlockSpec((LANES, LANES), const),
        ],
        out_specs=out_specs,
        out_shape=out_shape,
        compiler_params=_params("parallel"),
        name="inproj",
    )(x2d, w_in_b, cos, sin, qn, kn, gm)


def _na_kernel(q_ref, kp_ref, kc_ref, kn_ref, vp_ref, vc_ref, vn_ref, bias_ref, o_ref,
               kbuf, vbuf, *, rows, nrb):
    blk = NA_WIN_H * GRID_W
    j = pl.program_id(0) % nrb
    kbuf[0:blk] = kp_ref[...]
    kbuf[blk:2 * blk] = kc_ref[...]
    kbuf[2 * blk:3 * blk] = kn_ref[...]
    vbuf[0:blk] = vp_ref[...]
    vbuf[blk:2 * blk] = vc_ref[...]
    vbuf[2 * blk:3 * blk] = vn_ref[...]
    lane = lax.broadcasted_iota(I32, (GRID_W, LANES), 1)
    low = lane < HEAD_DIM

    def row_body(i, carry):
        r = j * NA_WIN_H + i
        rs = jnp.clip(r - NA_WIN_H // 2, 0, rows - NA_WIN_H)
        d0 = rs - r + (NA_WIN_H - 1)
        off = pl.multiple_of((rs - (j - 1) * NA_WIN_H) * GRID_W, GRID_W)
        qoff = pl.multiple_of(i * GRID_W, GRID_W)
        for p in range(NA_WIDTH // LANES):
            cols = slice(p * LANES, (p + 1) * LANES)
            q2 = q_ref[pl.ds(qoff, GRID_W), cols]
            k2 = kbuf[pl.ds(off, blk), cols]
            v2 = vbuf[pl.ds(off, blk), cols]
            zero = jnp.zeros_like(q2)
            qs = jnp.concatenate([jnp.where(low, q2, zero), jnp.where(low, zero, q2)], axis=0)
            st = lax.dot_general(k2, qs, _NT, preferred_element_type=F32) + bias_ref[d0, p]
            e = jnp.exp(st - jnp.max(st, axis=0, keepdims=True))
            prob = e * (1.0 / jnp.sum(e, axis=0, keepdims=True))
            o = jnp.dot(prob.T.astype(BF16), v2, preferred_element_type=F32)
            o_ref[pl.ds(qoff, GRID_W), cols] = jnp.where(low, o[:GRID_W], o[GRID_W:]).astype(BF16)
        return carry

    lax.fori_loop(0, NA_WIN_H, row_body, 0)


def _na_bias_table(rpb):
    c = jnp.arange(GRID_W)
    cs = jnp.clip(c - NA_WIN_W // 2, 0, GRID_W - NA_WIN_W)
    cc = jnp.arange(GRID_W)
    inwin = (cc[None, :] >= cs[:, None]) & (cc[None, :] < cs[:, None] + NA_WIN_W)
    dc = jnp.clip(cc[None, :] - c[:, None] + (NA_WIN_W - 1), 0, 2 * NA_WIN_W - 2)
    full = jnp.where(inwin[None, None], rpb[:, :, dc].astype(F32), NEG_BIG)
    tbl = jnp.stack([full[:, d0:d0 + NA_WIN_H] for d0 in range(NA_WIN_H)])
    tbl = tbl.reshape(NA_WIN_H, NA_HEADS // 2, 2, NA_WIN_H, GRID_W, GRID_W)
    return tbl.transpose(0, 1, 3, 5, 2, 4).reshape(NA_WIN_H, NA_HEADS // 2, NA_WIN_H * GRID_W, 2 * GRID_W)


def _na_attention(q, k, v, bias_tbl, *, seq):
    n = q.shape[0]
    rows = seq // GRID_W
    assert rows % NA_WIN_H == 0 and rows >= 2 * NA_WIN_H
    nrb = rows // NA_WIN_H
    blk = NA_WIN_H * GRID_W

    def cur(g):
        return (g, 0)

    def prev(g):
        return (g - jnp.where(g % nrb == 0, 0, 1), 0)

    def nxt(g):
        return (g + jnp.where(g % nrb == nrb - 1, 0, 1), 0)

    spec = lambda f: pl.BlockSpec((blk, NA_WIDTH), f)
    return pl.pallas_call(
        functools.partial(_na_kernel, rows=rows, nrb=nrb),
        grid=(n // blk,),
        in_specs=[spec(cur), spec(prev), spec(cur), spec(nxt), spec(prev), spec(cur), spec(nxt),
                  pl.BlockSpec(bias_tbl.shape, lambda g: (0, 0, 0, 0))],
        out_specs=spec(cur),
        out_shape=jax.ShapeDtypeStruct((n, NA_WIDTH), BF16),
        scratch_shapes=[pltpu.VMEM((3 * blk, NA_WIDTH), BF16), pltpu.VMEM((3 * blk, NA_WIDTH), BF16)],
        compiler_params=_params("parallel"),
        name="na_attention",
    )(q, k, k, k, v, v, v, bias_tbl)


def _gqa_kernel(q_ref, k_ref, vt_ref, o_ref, qs_sc, m_sc, acc_sc, *, tq, tk, seq):
    lane = lax.broadcasted_iota(I32, (tq, LANES), 1)
    low = lane < HEAD_DIM
    for p in range(HEAD_PAIRS):
        q2 = q_ref[:, p * LANES:(p + 1) * LANES]
        zero = jnp.zeros_like(q2)
        qs_sc[(2 * p) * tq:(2 * p + 1) * tq, :] = jnp.where(low, q2, zero)
        qs_sc[(2 * p + 1) * tq:(2 * p + 2) * tq, :] = jnp.where(low, zero, q2)
    m_sc[...] = jnp.full(m_sc.shape, -jnp.inf, F32)
    acc_sc[...] = jnp.zeros(acc_sc.shape, F32)

    def body(c, carry):
        koff = pl.multiple_of(c * tk, tk)
        st = lax.dot_general(k_ref[pl.ds(koff, tk), :], qs_sc[...], _NT, preferred_element_type=F32)
        m_old = m_sc[...]
        m_new = jnp.maximum(m_old, jnp.max(st, axis=0, keepdims=True))
        alpha = jnp.exp2(m_old - m_new)
        e = jnp.exp2(st - m_new).astype(BF16)
        pv = jnp.dot(vt_ref[:, pl.ds(koff, tk)], e, preferred_element_type=F32)
        acc_sc[...] = alpha * acc_sc[...] + pv
        m_sc[...] = m_new
        return carry

    lax.fori_loop(0, seq // tk, body, 0)
    o_t = acc_sc[0:KV_WIDTH, :] * (1.0 / acc_sc[KV_WIDTH:KV_WIDTH + 1, :])
    top = lax.broadcasted_iota(I32, (KV_WIDTH, tq), 0) < HEAD_DIM
    for p in range(HEAD_PAIRS):
        a = o_t[:, (2 * p) * tq:(2 * p + 1) * tq]
        b = o_t[:, (2 * p + 1) * tq:(2 * p + 2) * tq]
        o_ref[:, p * LANES:(p + 1) * LANES] = jnp.where(top, a, b).T.astype(BF16)


def _gqa_attention(q, k, vt, *, seq, tq, tk):
    n = q.shape[0]
    assert seq % tq == 0 and seq % tk == 0
    qblocks = seq // tq
    width = GQA_Q_HEADS * tq
    return pl.pallas_call(
        functools.partial(_gqa_kernel, tq=tq, tk=tk, seq=seq),
        grid=(n // seq, qblocks),
        in_specs=[
            pl.BlockSpec((tq, GQA_WIDTH), lambda b, i: (b * qblocks + i, 0)),
            pl.BlockSpec((seq, KV_WIDTH), lambda b, i: (b, 0)),
            pl.BlockSpec((VT_ROWS, seq), lambda b, i: (b, 0)),
        ],
        out_specs=pl.BlockSpec((tq, GQA_WIDTH), lambda b, i: (b * qblocks + i, 0)),
        out_shape=jax.ShapeDtypeStruct((n, GQA_WIDTH), BF16),
        scratch_shapes=[pltpu.VMEM((width, KV_WIDTH), BF16), pltpu.VMEM((1, width), F32),
                        pltpu.VMEM((VT_ROWS, width), F32)],
        compiler_params=_params("parallel", "parallel"),
        name="gqa_attention",
    )(q, k, vt)


def _layer_norm(h, g, b):
    mu = jnp.mean(h, axis=-1, keepdims=True)
    hc = h - mu
    var = jnp.mean(hc * hc, axis=-1, keepdims=True)
    return hc * lax.rsqrt(var + LN_EPS) * g + b


def _merge_kernel(na_ref, gq_ref, gate_ref, x_ref, wna_ref, wgq_ref, wout_ref, g_ref, b_ref,
                  wrh_ref, wrl_ref, x1_ref, aff_ref, *, alpha, d_model):
    y_na = jnp.dot(na_ref[...], wna_ref[...], preferred_element_type=F32)
    y_gq = jnp.dot(gq_ref[...], wgq_ref[...], preferred_element_type=F32)
    mixin = gate_ref[:, :d_model] * y_na + gate_ref[:, d_model:] * y_gq
    mix = jnp.dot(mixin.astype(BF16), wout_ref[...], preferred_element_type=F32)
    x1 = _layer_norm(alpha * x_ref[...] + mix, g_ref[...], b_ref[...])
    x1_ref[...] = x1
    hi = x1.astype(BF16)
    lo = (x1 - hi.astype(F32)).astype(BF16)
    wh = wrh_ref[...]
    logits = (lax.dot_general(wh, hi, _NT, preferred_element_type=F32)
              + lax.dot_general(wh, lo, _NT, preferred_element_type=F32)
              + lax.dot_general(wrl_ref[...], hi, _NT, preferred_element_type=F32))
    m = jnp.max(logits, axis=0, keepdims=True)
    e = jnp.exp(logits - m)
    aff_ref[...] = e / jnp.sum(e, axis=0, keepdims=True)


def _merge(na, gq, gates, x2d, wna, wgq, wout, ln_g, ln_b, wr_hi, wr_lo, *, alpha, tm):
    n, d = x2d.shape
    n_exp = wr_hi.shape[0]
    row = lambda i: (i, 0)
    const = lambda i: (0, 0)
    return pl.pallas_call(
        functools.partial(_merge_kernel, alpha=alpha, d_model=d),
        grid=(n // tm,),
        in_specs=[
            pl.BlockSpec((tm, NA_WIDTH), row), pl.BlockSpec((tm, GQA_WIDTH), row),
            pl.BlockSpec((tm, 2 * d), row), pl.BlockSpec((tm, d), row),
            pl.BlockSpec((NA_WIDTH, d), const), pl.BlockSpec((GQA_WIDTH, d), const),
            pl.BlockSpec((d, d), const), pl.BlockSpec((1, d), const), pl.BlockSpec((1, d), const),
            pl.BlockSpec((n_exp, d), const), pl.BlockSpec((n_exp, d), const),
        ],
        out_specs=[pl.BlockSpec((tm, d), row), pl.BlockSpec((n_exp, tm), lambda i: (0, i))],
        out_shape=[jax.ShapeDtypeStruct((n, d), F32), jax.ShapeDtypeStruct((n_exp, n), F32)],
        compiler_params=_params("parallel"),
        name="merge_ln_router",
    )(na, gq, gates, x2d, wna, wgq, wout, ln_g, ln_b, wr_hi, wr_lo)


def _route_kernel(aff_ref, pos_ref, base_ref, cnt_ref, *, cap, tile, count_chunk):
    n_exp, n = aff_ref.shape
    n_tiles = n // tile
    capf = float(cap)

    def bits_at(off, width):
        return lax.bitcast_convert_type(aff_ref[:, pl.ds(off, width)], I32)

    def count_ge(cand):
        def inner(c, acc):
            b = bits_at(pl.multiple_of(c * count_chunk, count_chunk), count_chunk)
            return acc + jnp.where(b >= cand, 1.0, 0.0)
        acc = lax.fori_loop(0, n // count_chunk, inner, jnp.zeros((n_exp, count_chunk), F32))
        return jnp.sum(acc, axis=1, keepdims=True)

    def bisect(i, prefix):
        cand = prefix | jnp.left_shift(jnp.int32(1), 30 - i)
        return jnp.where(count_ge(cand) >= capf, cand, prefix)

    thr = lax.fori_loop(0, 31, bisect, jnp.zeros((n_exp, 1), I32))
    need = capf - count_ge(thr + 1)

    ri = lax.broadcasted_iota(I32, (tile, tile), 0)
    ci = lax.broadcasted_iota(I32, (tile, tile), 1)
    upper = jnp.where(ri < ci, 1.0, 0.0).astype(BF16)
    ones = jnp.ones((tile, tile), BF16)
    tbl_lane = lax.broadcasted_iota(I32, base_ref.shape, 1)

    base_ref[...] = jnp.zeros(base_ref.shape, I32)
    cnt_ref[...] = jnp.zeros(cnt_ref.shape, I32)

    def body(i, carry):
        ceq, base = carry
        off = pl.multiple_of(i * tile, tile)
        b = lax.bitcast_convert_type(aff_ref[:, pl.ds(off, tile)], I32)
        eq = b == thr
        eqb = jnp.where(eq, 1.0, 0.0).astype(BF16)
        eqrank = jnp.dot(eqb, upper, preferred_element_type=F32) + ceq
        sel = (b > thr) | (eq & (eqrank < need))
        selb = jnp.where(sel, 1.0, 0.0).astype(BF16)
        rank_in_tile = jnp.dot(selb, upper, preferred_element_type=F32)
        cnt = jnp.dot(selb, ones, preferred_element_type=F32)
        pos_ref[:, pl.ds(off, tile)] = jnp.where(sel, base + rank_in_tile, float(UNPICKED)).astype(I32)
        base_ref[...] = jnp.where(tbl_lane == i, base[:, :1].astype(I32), base_ref[...])
        cnt_ref[...] = jnp.where(tbl_lane == i, cnt[:, :1].astype(I32), cnt_ref[...])
        padded = jnp.floor((cnt + (SUBLANES - 1.0)) * (1.0 / SUBLANES)) * SUBLANES
        return ceq + jnp.dot(eqb, ones, preferred_element_type=F32), base + padded

    zero = jnp.zeros((n_exp, tile), F32)
    _, used = lax.fori_loop(0, n_tiles, body, (zero, zero))
    base_ref[...] = jnp.where(tbl_lane == n_tiles, used[:, :1].astype(I32), base_ref[...])


def _route(aff, *, cap, tile):
    n_exp, n = aff.shape
    count_chunk = min(2048, n)
    assert n % count_chunk == 0 and n % tile == 0
    tbl_w = pl.cdiv(n // tile + 1, LANES) * LANES
    full = lambda shape: pl.BlockSpec(shape, lambda i: (0, 0))
    return pl.pallas_call(
        functools.partial(_route_kernel, cap=cap, tile=tile, count_chunk=count_chunk),
        grid=(1,),
        in_specs=[full((n_exp, n))],
        out_specs=[full((n_exp, n)), full((n_exp, tbl_w)), full((n_exp, tbl_w))],
        out_shape=[jax.ShapeDtypeStruct((n_exp, n), I32), jax.ShapeDtypeStruct((n_exp, tbl_w), I32),
                   jax.ShapeDtypeStruct((n_exp, tbl_w), I32)],
        compiler_params=_params("arbitrary"),
        name="route",
    )(aff)


def _pad_rows(c):
    return ((c + (SUBLANES - 1)) >> 3) << 3


def _start_pieces(rows, max_rows, make_copy):
    k = 3
    assert SUBLANES == 1 << k
    while (1 << k) <= max_rows:
        size = 1 << k

        @pl.when(((rows >> k) & 1) == 1)
        def _():
            make_copy(pl.multiple_of((rows >> (k + 1)) << (k + 1), SUBLANES), size).start()

        k += 1


def _wait_rows(hbm_ref, rows, sem):
    @pl.when(rows > 0)
    def _():
        view = hbm_ref.at[pl.ds(0, pl.multiple_of(rows, SUBLANES))]
        pltpu.make_async_copy(view, view, sem).wait()


def _dispatch_kernel(base_ref, cnt_ref, x_ref, pos_ref, xe_ref, stg, sems, *, tile, cap_pad, n_tiles):
    i = pl.program_id(0)
    n_exp = pos_ref.shape[0]
    xb = x_ref[...].astype(BF16)
    few = DISPATCH_FEW_ROWS
    padded = []
    for e in range(n_exp):
        slot = e % 2
        if e >= 2:
            _wait_rows(xe_ref, padded[e - 2], sems.at[slot])
        base = base_ref[e, i]
        rows = _pad_rows(cnt_ref[e, i])
        padded.append(rows)
        rel = pos_ref[e:e + 1, :] - base

        def fill(r0, r1):
            row_id = lax.broadcasted_iota(I32, (r1 - r0, tile), 0) + r0
            onehot = jnp.where(row_id == rel, 1.0, 0.0).astype(BF16)
            stg[slot, r0:r1, :] = jnp.dot(onehot, xb, preferred_element_type=F32)

        @pl.when(rows > 0)
        def _():
            fill(0, few)

        @pl.when(rows > few)
        def _():
            fill(few, tile)

        dst0 = e * cap_pad + base

        def make_copy(off, size):
            return pltpu.make_async_copy(stg.at[slot, pl.ds(off, size)],
                                         xe_ref.at[pl.ds(pl.multiple_of(dst0 + off, SUBLANES), size)], sems.at[slot])

        _start_pieces(rows, tile, make_copy)
    for e in range(n_exp - 2, n_exp):
        _wait_rows(xe_ref, padded[e], sems.at[e % 2])

    @pl.when(i == n_tiles - 1)
    def _():
        stg[0] = jnp.zeros((tile, stg.shape[2]), F32)
        for e in range(n_exp):
            used = base_ref[e, n_tiles]
            tail = cap_pad - used
            dst0 = e * cap_pad + used

            def zero_copy(off, size):
                return pltpu.make_async_copy(stg.at[0, pl.ds(0, size)],
                                             xe_ref.at[pl.ds(pl.multiple_of(dst0 + off, SUBLANES), size)], sems.at[0])

            def whole(j, carry):
                cp = zero_copy(j * tile, tile)
                cp.start()
                cp.wait()
                return carry

            n_whole = tail // tile
            lax.fori_loop(0, n_whole, whole, 0)
            rest = tail - n_whole * tile
            _start_pieces(rest, tile, lambda off, size: zero_copy(n_whole * tile + off, size))
            _wait_rows(xe_ref, rest, sems.at[0])


def _dispatch(x1, pos, base_tbl, cnt_tbl, *, cap_pad, tile):
    n, d = x1.shape
    n_exp = pos.shape[0]
    n_tiles = n // tile
    return pl.pallas_call(
        functools.partial(_dispatch_kernel, tile=tile, cap_pad=cap_pad, n_tiles=n_tiles),
        grid_spec=pltpu.PrefetchScalarGridSpec(
            num_scalar_prefetch=2,
            grid=(n_tiles,),
            in_specs=[pl.BlockSpec((tile, d), lambda i, b, c: (i, 0)),
                      pl.BlockSpec((n_exp, tile), lambda i, b, c: (0, i))],
            out_specs=pl.BlockSpec(memory_space=pl.ANY),
            scratch_shapes=[pltpu.VMEM((2, tile, d), F32), pltpu.SemaphoreType.DMA((2,))],
        ),
        out_shape=jax.ShapeDtypeStruct((n_exp * cap_pad, d), F32),
        compiler_params=_params("arbitrary"),
        name="dispatch",
    )(base_tbl, cnt_tbl, x1, pos)


def _ffn_kernel(base_ref, xe_ref, wg_ref, wu_ref, wd_ref, ye_ref, *, f_chunk, n_tiles):
    tm = xe_ref.shape[0]
    used = base_ref[pl.program_id(0), n_tiles]
    live = pl.program_id(1) * tm < used

    @pl.when(live)
    def _():
        xb = xe_ref[...].astype(BF16)
        d_ff = wg_ref.shape[2]
        acc = jnp.zeros(ye_ref.shape, F32)
        for c in range(d_ff // f_chunk):
            cols = slice(c * f_chunk, (c + 1) * f_chunk)
            g = jnp.dot(xb, wg_ref[0, :, cols], preferred_element_type=F32)
            u = jnp.dot(xb, wu_ref[0, :, cols], preferred_element_type=F32)
            h = (g * jax.nn.sigmoid(g)) * u
            acc = acc + jnp.dot(h.astype(BF16), wd_ref[0, cols, :], preferred_element_type=F32)
        ye_ref[...] = acc

    @pl.when(jnp.logical_not(live))
    def _():
        ye_ref[...] = jnp.zeros(ye_ref.shape, F32)


def _ffn(xe, base_tbl, wg, wu, wd, *, cap_pad, tm, f_chunk, n_tiles):
    n_slots, d = xe.shape
    n_exp, _, d_ff = wg.shape
    assert cap_pad % tm == 0 and d_ff % f_chunk == 0
    kblocks = cap_pad // tm
    return pl.pallas_call(
        functools.partial(_ffn_kernel, f_chunk=f_chunk, n_tiles=n_tiles),
        grid_spec=pltpu.PrefetchScalarGridSpec(
            num_scalar_prefetch=1,
            grid=(n_exp, kblocks),
            in_specs=[
                pl.BlockSpec((tm, d), lambda e, k, b: (e * kblocks + k, 0)),
                pl.BlockSpec((1, d, d_ff), lambda e, k, b: (e, 0, 0)),
                pl.BlockSpec((1, d, d_ff), lambda e, k, b: (e, 0, 0)),
                pl.BlockSpec((1, d_ff, d), lambda e, k, b: (e, 0, 0)),
            ],
            out_specs=pl.BlockSpec((tm, d), lambda e, k, b: (e * kblocks + k, 0)),
        ),
        out_shape=jax.ShapeDtypeStruct((n_slots, d), F32),
        compiler_params=_params("parallel", "parallel"),
        name="expert_ffn",
    )(base_tbl, xe, wg, wu, wd)


def _combine_kernel(base_ref, cnt_ref, x_ref, pos_ref, aff_ref, g_ref, b_ref, ye_ref, o_ref,
                    stg, sems, *, tile, cap_pad, alpha):
    i = pl.program_id(0)
    n_exp = pos_ref.shape[1]

    @pl.when(i == 0)
    def _():
        stg[...] = jnp.zeros(stg.shape, F32)

    bases = [base_ref[e, i] for e in range(n_exp)]
    counts = [cnt_ref[e, i] for e in range(n_exp)]
    padded = [_pad_rows(c) for c in counts]

    def fetch(e):
        slot = e % 2
        src0 = e * cap_pad + bases[e]

        def make_copy(off, size):
            return pltpu.make_async_copy(ye_ref.at[pl.ds(pl.multiple_of(src0 + off, SUBLANES), size)],
                                         stg.at[slot, pl.ds(off, size)], sems.at[slot])

        _start_pieces(padded[e], tile, make_copy)

    fetch(0)
    o_ref[...] = alpha * x_ref[...]
    slot_id = lax.broadcasted_iota(I32, (tile, tile), 1)
    for e in range(n_exp):
        slot = e % 2
        if e + 1 < n_exp:
            fetch(e + 1)
        _wait_rows(ye_ref, padded[e], sems.at[slot])

        @pl.when(counts[e] > 0)
        def _():
            rel = pos_ref[:, e:e + 1] - bases[e]
            onehot = jnp.where(slot_id == rel, 1.0, 0.0).astype(BF16)
            picked = jnp.dot(onehot, stg[slot].astype(BF16), preferred_element_type=F32)
            o_ref[...] += picked * aff_ref[:, e:e + 1]

    o_ref[...] = _layer_norm(o_ref[...], g_ref[...], b_ref[...])


def _combine(x1, pos_t, aff_t, base_tbl, cnt_tbl, ln_g, ln_b, ye, *, cap_pad, alpha, tile):
    n, d = x1.shape
    n_exp = pos_t.shape[1]
    row = lambda i, b, c: (i, 0)
    const = lambda i, b, c: (0, 0)
    return pl.pallas_call(
        functools.partial(_combine_kernel, tile=tile, cap_pad=cap_pad, alpha=alpha),
        grid_spec=pltpu.PrefetchScalarGridSpec(
            num_scalar_prefetch=2,
            grid=(n // tile,),
            in_specs=[
                pl.BlockSpec((tile, d), row), pl.BlockSpec((tile, n_exp), row), pl.BlockSpec((tile, n_exp), row),
                pl.BlockSpec((1, d), const), pl.BlockSpec((1, d), const),
                pl.BlockSpec(memory_space=pl.ANY),
            ],
            out_specs=pl.BlockSpec((tile, d), row),
            scratch_shapes=[pltpu.VMEM((2, tile, d), F32), pltpu.SemaphoreType.DMA((2,))],
        ),
        out_shape=jax.ShapeDtypeStruct((n, d), F32),
        compiler_params=_params("arbitrary"),
        name="combine_ln",
    )(base_tbl, cnt_tbl, x1, pos_t, aff_t, ln_g, ln_b, ye)


def _rope_tables(seq):
    t = jnp.arange(seq)
    row = (t // GRID_W).astype(F32)
    col = (t % GRID_W).astype(F32)
    half = HEAD_DIM // 2
    inv_freq = ROPE_THETA ** (-jnp.arange(0, half, 2, dtype=F32) / half)
    ang_r = row[:, None] * inv_freq[None, :]
    ang_c = col[:, None] * inv_freq[None, :]
    ang = jnp.concatenate([ang_r, ang_r, ang_c, ang_c], axis=-1)
    sign = jnp.where((jnp.arange(HEAD_DIM) % half) < half // 2, -1.0, 1.0).astype(F32)
    reps = LANES // HEAD_DIM
    return jnp.tile(jnp.cos(ang), (1, reps)), jnp.tile(jnp.sin(ang) * sign[None, :], (1, reps))


def _gqa_slot_columns():
    heads = [p + GQA_GROUP * half for p in range(HEAD_PAIRS) for half in range(2)]
    return np.concatenate([np.arange(HEAD_DIM) + HEAD_DIM * h for h in heads])


def _prep_layer(w_in, na_rpb, q_norm, k_norm, w_br_na, w_br_gqa, w_out, ln1_g, ln1_b,
                w_router, w_e_gate, w_e_up, w_e_down, ln2_g, ln2_b):
    d = w_in.shape[0]
    s_na = 3 * NA_WIDTH
    cols = _gqa_slot_columns()
    perm = np.concatenate([np.arange(s_na), s_na + cols, np.arange(s_na + GQA_WIDTH, w_in.shape[1])])
    reps = LANES // HEAD_DIM
    wr_t = w_router.T.astype(F32)
    wr_hi = wr_t.astype(BF16)
    gm = np.kron(np.eye(reps, dtype=np.float32), np.full((HEAD_DIM, HEAD_DIM), 1.0 / HEAD_DIM, np.float32))
    return dict(
        w_in=w_in[:, perm].astype(BF16),
        bias_tbl=_na_bias_table(na_rpb),
        qn=jnp.tile(q_norm.astype(F32), reps)[None, :], kn=jnp.tile(k_norm.astype(F32), reps)[None, :],
        gm=jnp.asarray(gm, BF16),
        wna=w_br_na.astype(BF16), wgq=w_br_gqa[cols].astype(BF16), wout=w_out.astype(BF16),
        ln1_g=ln1_g.astype(F32).reshape(1, d), ln1_b=ln1_b.astype(F32).reshape(1, d),
        wr_hi=wr_hi, wr_lo=(wr_t - wr_hi.astype(F32)).astype(BF16),
        wg=w_e_gate.astype(BF16), wu=w_e_up.astype(BF16), wd=w_e_down.astype(BF16),
        ln2_g=ln2_g.astype(F32).reshape(1, d), ln2_b=ln2_b.astype(F32).reshape(1, d),
    )


def _tiles(seq):
    return dict(tm_proj=512, tq=256, tk=1024, tm_merge=256, t_moe=256, tm_ffn=512, f_chunk=512)


def _trunk_layer(x2d, p, *, seq, alpha, rope):
    n, d = x2d.shape
    n_exp = p["wg"].shape[0]
    cap = EC_CAPACITY * n // n_exp
    tl = _tiles(seq)
    cos, sin = rope
    naq, nak, nav, gq, gk, gv, gates = _inproj(x2d, p["w_in"], cos, sin, p["qn"], p["kn"], p["gm"],
                                               seq=seq, tm=min(tl["tm_proj"], seq))
    na = _na_attention(naq, nak, nav, p["bias_tbl"], seq=seq)
    nb = n // seq
    ones_rows = jnp.zeros((nb, VT_ROWS - KV_WIDTH, seq), BF16).at[:, 0, :].set(1.0)
    gvt = jnp.concatenate([gv.reshape(nb, seq, KV_WIDTH).transpose(0, 2, 1), ones_rows], axis=1)
    gvt = gvt.reshape(nb * VT_ROWS, seq)
    ga = _gqa_attention(gq, gk, gvt, seq=seq, tq=min(tl["tq"], seq), tk=min(tl["tk"], seq))
    x1, aff = _merge(na, ga, gates, x2d, p["wna"], p["wgq"], p["wout"], p["ln1_g"], p["ln1_b"],
                     p["wr_hi"], p["wr_lo"], alpha=alpha, tm=tl["tm_merge"])
    tile = tl["t_moe"]
    n_tiles = n // tile
    tm_ffn = min(tl["tm_ffn"], cap)
    cap_pad = pl.cdiv(cap + (SUBLANES - 1) * n_tiles, tm_ffn) * tm_ffn
    pos, base_tbl, cnt_tbl = _route(aff, cap=cap, tile=tile)
    xe = _dispatch(x1, pos, base_tbl, cnt_tbl, cap_pad=cap_pad, tile=tile)
    ye = _ffn(xe, base_tbl, p["wg"], p["wu"], p["wd"], cap_pad=cap_pad, tm=tm_ffn, f_chunk=tl["f_chunk"],
              n_tiles=n_tiles)
    return _combine(x1, pos.T, aff.T, base_tbl, cnt_tbl, p["ln2_g"], p["ln2_b"], ye,
                    cap_pad=cap_pad, alpha=alpha, tile=tile)


@jax.jit
def kernel(x_prompt, x_sample, w_in, na_rpb, q_norm, k_norm, w_br_na, w_br_gqa, w_out, ln1_g, ln1_b,
           w_router, w_e_gate, w_e_up, w_e_down, ln2_g, ln2_b):
    depth = w_in.shape[0]
    alpha = float((2 * depth) ** 0.25)
    d = x_prompt.shape[-1]
    groups = []
    for x in (x_prompt, x_sample):
        b, s, _ = x.shape
        groups.append(dict(x=x.reshape(b * s, d), shape=x.shape, seq=s, rope=_rope_tables(s)))
    for l in range(depth):
        p = _prep_layer(w_in[l], na_rpb[l], q_norm[l], k_norm[l], w_br_na[l], w_br_gqa[l], w_out[l],
                        ln1_g[l], ln1_b[l], w_router[l], w_e_gate[l], w_e_up[l], w_e_down[l],
                        ln2_g[l], ln2_b[l])
        for g in groups:
            g["x"] = _trunk_layer(g["x"], p, seq=g["seq"], alpha=alpha, rope=g["rope"])
    return tuple(g["x"].reshape(g["shape"]) for g in groups)
```

```python
import functools

import jax
import jax.numpy as jnp
import numpy as np
from jax import lax
from jax.experimental import pallas as pl
from jax.experimental.pallas import tpu as pltpu

F32 = jnp.float32
BF16 = jnp.bfloat16
I32 = jnp.int32

GRID_W = 64
HEAD_DIM = 64
NA_HEADS = 8
NA_WIN_H = 8
NA_WIN_W = 16
GQA_Q_HEADS = 8
GQA_KV_HEADS = 2
GQA_GROUP = GQA_Q_HEADS // GQA_KV_HEADS
ROPE_THETA = 10000.0
EC_CAPACITY = 2
LN_EPS = 1e-5
RMS_EPS = 1e-6

LANES = 128
SUBLANES = 8
NA_WIDTH = NA_HEADS * HEAD_DIM
GQA_WIDTH = GQA_Q_HEADS * HEAD_DIM
KV_WIDTH = GQA_KV_HEADS * HEAD_DIM
HEAD_PAIRS = GQA_WIDTH // LANES
BF16_TILE_ROWS = 2 * SUBLANES
VT_ROWS = KV_WIDTH + BF16_TILE_ROWS
UNPICKED = -(1 << 30)
DISPATCH_FEW_ROWS = 64
GQA_Q_SCALE = HEAD_DIM ** -0.5 * float(np.log2(np.e))
NEG_BIG = -1e30
VMEM_LIMIT_BYTES = 56 * 1024 * 1024

_NT = (((1,), (1,)), ((), ()))


def _params(*sem):
    return pltpu.CompilerParams(dimension_semantics=sem, vmem_limit_bytes=VMEM_LIMIT_BYTES)


def _inproj_kernel(x_ref, w_ref, cos_ref, sin_ref, qn_ref, kn_ref, gm_ref,
                   naq_ref, nak_ref, nav_ref, gq_ref, gk_ref, gv_ref, gate_ref, *, d_model):
    xb = x_ref[...].astype(BF16)
    s_na = 3 * NA_WIDTH
    s_gq = s_na + GQA_WIDTH
    s_gk = s_gq + KV_WIDTH
    s_gv = s_gk + KV_WIDTH

    def proj(c0, width):
        return jnp.dot(xb, w_ref[:, c0:c0 + width], preferred_element_type=F32)

    naq_ref[...] = (proj(0, NA_WIDTH) * HEAD_DIM ** -0.5).astype(BF16)
    nak_ref[...] = proj(NA_WIDTH, NA_WIDTH).astype(BF16)
    nav_ref[...] = proj(2 * NA_WIDTH, NA_WIDTH).astype(BF16)

    cos = cos_ref[...]
    sin = sin_ref[...]
    gm = gm_ref[...]
    lane = lax.broadcasted_iota(I32, cos.shape, 1)
    first_half = (lane % (HEAD_DIM // 2)) < (HEAD_DIM // 4)

    def norm_rope(a, gain):
        sq = a * a
        hi = sq.astype(BF16)
        lo = (sq - hi.astype(F32)).astype(BF16)
        ms = jnp.dot(hi, gm, preferred_element_type=F32) + jnp.dot(lo, gm, preferred_element_type=F32)
        an = a * lax.rsqrt(ms + RMS_EPS) * gain
        quarter = HEAD_DIM // 4
        rot = jnp.where(first_half, pltpu.roll(an, LANES - quarter, 1), pltpu.roll(an, quarter, 1))
        return an * cos + rot * sin

    qn = qn_ref[...]
    for p in range(HEAD_PAIRS):
        a = proj(s_na + p * LANES, LANES)
        gq_ref[:, p * LANES:(p + 1) * LANES] = (norm_rope(a, qn) * GQA_Q_SCALE).astype(BF16)
    gk_ref[...] = norm_rope(proj(s_gq, KV_WIDTH), kn_ref[...]).astype(BF16)
    gv_ref[...] = proj(s_gk, KV_WIDTH).astype(BF16)
    gate_chunk = 512
    for c in range(2 * d_model // gate_chunk):
        g = proj(s_gv + c * gate_chunk, gate_chunk)
        gate_ref[:, c * gate_chunk:(c + 1) * gate_chunk] = jax.nn.sigmoid(g)


def _inproj(x2d, w_in_b, cos, sin, qn, kn, gm, *, seq, tm):
    n, d = x2d.shape
    d_in = w_in_b.shape[1]
    assert n % tm == 0 and seq % tm == 0
    sblocks = seq // tm
    row = lambda i: (i, 0)
    const = lambda i: (0, 0)
    pos = lambda i: (i % sblocks, 0)
    out_shape = [
        jax.ShapeDtypeStruct((n, NA_WIDTH), BF16), jax.ShapeDtypeStruct((n, NA_WIDTH), BF16),
        jax.ShapeDtypeStruct((n, NA_WIDTH), BF16), jax.ShapeDtypeStruct((n, GQA_WIDTH), BF16),
        jax.ShapeDtypeStruct((n, KV_WIDTH), BF16), jax.ShapeDtypeStruct((n, KV_WIDTH), BF16),
        jax.ShapeDtypeStruct((n, 2 * d), F32),
    ]
    out_specs = [
        pl.BlockSpec((tm, NA_WIDTH), row), pl.BlockSpec((tm, NA_WIDTH), row), pl.BlockSpec((tm, NA_WIDTH), row),
        pl.BlockSpec((tm, GQA_WIDTH), row), pl.BlockSpec((tm, KV_WIDTH), row), pl.BlockSpec((tm, KV_WIDTH), row),
        pl.BlockSpec((tm, 2 * d), row),
    ]
    return pl.pallas_call(
        functools.partial(_inproj_kernel, d_model=d),
        grid=(n // tm,),
        in_specs=[
            pl.BlockSpec((tm, d), row), pl.BlockSpec((d, d_in), const),
            pl.BlockSpec((tm, LANES), pos), pl.BlockSpec((tm, LANES), pos),
            pl.BlockSpec((1, LANES), const), pl.BlockSpec((1, LANES), const),
            pl.BlockSpec((LANES, LANES), const),
        ],
        out_specs=out_specs,
        out_shape=out_shape,
        compiler_params=_params("parallel"),
        name="inproj",
    )(x2d, w_in_b, cos, sin, qn, kn, gm)


def _na_kernel(q_ref, kp_ref, kc_ref, kn_ref, vp_ref, vc_ref, vn_ref, bias_ref, o_ref,
               kbuf, vbuf, *, rows, nrb):
    blk = NA_WIN_H * GRID_W
    j = pl.program_id(0) % nrb
    kbuf[0:blk] = kp_ref[...]
    kbuf[blk:2 * blk] = kc_ref[...]
    kbuf[2 * blk:3 * blk] = kn_ref[...]
    vbuf[0:blk] = vp_ref[...]
    vbuf[blk:2 * blk] = vc_ref[...]
    vbuf[2 * blk:3 * blk] = vn_ref[...]
    lane = lax.broadcasted_iota(I32, (GRID_W, LANES), 1)
    low = lane < HEAD_DIM

    def row_body(i, carry):
        r = j * NA_WIN_H + i
        rs = jnp.clip(r - NA_WIN_H // 2, 0, rows - NA_WIN_H)
        d0 = rs - r + (NA_WIN_H - 1)
        off = pl.multiple_of((rs - (j - 1) * NA_WIN_H) * GRID_W, GRID_W)
        qoff = pl.multiple_of(i * GRID_W, GRID_W)
        for p in range(NA_WIDTH // LANES):
            cols = slice(p * LANES, (p + 1) * LANES)
            q2 = q_ref[pl.ds(qoff, GRID_W), cols]
            k2 = kbuf[pl.ds(off, blk), cols]
            v2 = vbuf[pl.ds(off, blk), cols]
            zero = jnp.zeros_like(q2)
            qs = jnp.concatenate([jnp.where(low, q2, zero), jnp.where(low, zero, q2)], axis=0)
            st = lax.dot_general(k2, qs, _NT, preferred_element_type=F32) + bias_ref[d0, p]
            e = jnp.exp(st - jnp.max(st, axis=0, keepdims=True))
            prob = e * (1.0 / jnp.sum(e, axis=0, keepdims=True))
            o = jnp.dot(prob.T.astype(BF16), v2, preferred_element_type=F32)
            o_ref[pl.ds(qoff, GRID_W), cols] = jnp.where(low, o[:GRID_W], o[GRID_W:]).astype(BF16)
        return carry

    lax.fori_loop(0, NA_WIN_H, row_body, 0)


def _na_bias_table(rpb):
    c = jnp.arange(GRID_W)
    cs = jnp.clip(c - NA_WIN_W // 2, 0, GRID_W - NA_WIN_W)
    cc = jnp.arange(GRID_W)
    inwin = (cc[None, :] >= cs[:, None]) & (cc[None, :] < cs[:, None] + NA_WIN_W)
    dc = jnp.clip(cc[None, :] - c[:, None] + (NA_WIN_W - 1), 0, 2 * NA_WIN_W - 2)
    full = jnp.where(inwin[None, None], rpb[:, :, dc].astype(F32), NEG_BIG)
    tbl = jnp.stack([full[:, d0:d0 + NA_WIN_H] for d0 in range(NA_WIN_H)])
    tbl = tbl.reshape(NA_WIN_H, NA_HEADS // 2, 2, NA_WIN_H, GRID_W, GRID_W)
    return tbl.transpose(0, 1, 3, 5, 2, 4).reshape(NA_WIN_H, NA_HEADS // 2, NA_WIN_H * GRID_W, 2 * GRID_W)


def _na_attention(q, k, v, bias_tbl, *, seq):
    n = q.shape[0]
    rows = seq // GRID_W
    assert rows % NA_WIN_H == 0 and rows >= 2 * NA_WIN_H
    nrb = rows // NA_WIN_H
    blk = NA_WIN_H * GRID_W

    def cur(g):
        return (g, 0)

    def prev(g):
        return (g - jnp.where(g % nrb == 0, 0, 1), 0)

    def nxt(g):
        return (g + jnp.where(g % nrb == nrb - 1, 0, 1), 0)

    spec = lambda f: pl.BlockSpec((blk, NA_WIDTH), f)
    return pl.pallas_call(
        functools.partial(_na_kernel, rows=rows, nrb=nrb),
        grid=(n // blk,),
        in_specs=[spec(cur), spec(prev), spec(cur), spec(nxt), spec(prev), spec(cur), spec(nxt),
                  pl.BlockSpec(bias_tbl.shape, lambda g: (0, 0, 0, 0))],
        out_specs=spec(cur),
        out_shape=jax.ShapeDtypeStruct((n, NA_WIDTH), BF16),
        scratch_shapes=[pltpu.VMEM((3 * blk, NA_WIDTH), BF16), pltpu.VMEM((3 * blk, NA_WIDTH), BF16)],
        compiler_params=_params("parallel"),
        name="na_attention",
    )(q, k, k, k, v, v, v, bias_tbl)


def _gqa_kernel(q_ref, k_ref, vt_ref, o_ref, qs_sc, m_sc, acc_sc, *, tq, tk, seq):
    lane = lax.broadcasted_iota(I32, (tq, LANES), 1)
    low = lane < HEAD_DIM
    for p in range(HEAD_PAIRS):
        q2 = q_ref[:, p * LANES:(p + 1) * LANES]
        zero = jnp.zeros_like(q2)
        qs_sc[(2 * p) * tq:(2 * p + 1) * tq, :] = jnp.where(low, q2, zero)
        qs_sc[(2 * p + 1) * tq:(2 * p + 2) * tq, :] = jnp.where(low, zero, q2)
    m_sc[...] = jnp.full(m_sc.shape, -jnp.inf, F32)
    acc_sc[...] = jnp.zeros(acc_sc.shape, F32)

    def body(c, carry):
        koff = pl.multiple_of(c * tk, tk)
        st = lax.dot_general(k_ref[pl.ds(koff, tk), :], qs_sc[...], _NT, preferred_element_type=F32)
        m_old = m_sc[...]
        m_new = jnp.maximum(m_old, jnp.max(st, axis=0, keepdims=True))
        alpha = jnp.exp2(m_old - m_new)
        e = jnp.exp2(st - m_new).astype(BF16)
        pv = jnp.dot(vt_ref[:, pl.ds(koff, tk)], e, preferred_element_type=F32)
        acc_sc[...] = alpha * acc_sc[...] + pv
        m_sc[...] = m_new
        return carry

    lax.fori_loop(0, seq // tk, body, 0)
    o_t = acc_sc[0:KV_WIDTH, :] * (1.0 / acc_sc[KV_WIDTH:KV_WIDTH + 1, :])
    top = lax.broadcasted_iota(I32, (KV_WIDTH, tq), 0) < HEAD_DIM
    for p in range(HEAD_PAIRS):
        a = o_t[:, (2 * p) * tq:(2 * p + 1) * tq]
        b = o_t[:, (2 * p + 1) * tq:(2 * p + 2) * tq]
        o_ref[:, p * LANES:(p + 1) * LANES] = jnp.where(top, a, b).T.astype(BF16)


def _gqa_attention(q, k, vt, *, seq, tq, tk):
    n = q.shape[0]
    assert seq % tq == 0 and seq % tk == 0
    qblocks = seq // tq
    width = GQA_Q_HEADS * tq
    return pl.pallas_call(
        functools.partial(_gqa_kernel, tq=tq, tk=tk, seq=seq),
        grid=(n // seq, qblocks),
        in_specs=[
            pl.BlockSpec((tq, GQA_WIDTH), lambda b, i: (b * qblocks + i, 0)),
            pl.BlockSpec((seq, KV_WIDTH), lambda b, i: (b, 0)),
            pl.BlockSpec((VT_ROWS, seq), lambda b, i: (b, 0)),
        ],
        out_specs=pl.BlockSpec((tq, GQA_WIDTH), lambda b, i: (b * qblocks + i, 0)),
        out_shape=jax.ShapeDtypeStruct((n, GQA_WIDTH), BF16),
        scratch_shapes=[pltpu.VMEM((width, KV_WIDTH), BF16), pltpu.VMEM((1, width), F32),
                        pltpu.VMEM((VT_ROWS, width), F32)],
        compiler_params=_params("parallel", "parallel"),
        name="gqa_attention",
    )(q, k, vt)


def _layer_norm(h, g, b):
    mu = jnp.mean(h, axis=-1, keepdims=True)
    hc = h - mu
    var = jnp.mean(hc * hc, axis=-1, keepdims=True)
    return hc * lax.rsqrt(var + LN_EPS) * g + b


def _merge_kernel(na_ref, gq_ref, gate_ref, x_ref, wna_ref, wgq_ref, wout_ref, g_ref, b_ref,
                  wrh_ref, wrl_ref, x1_ref, aff_ref, *, alpha, d_model):
    y_na = jnp.dot(na_ref[...], wna_ref[...], preferred_element_type=F32)
    y_gq = jnp.dot(gq_ref[...], wgq_ref[...], preferred_element_type=F32)
    mixin = gate_ref[:, :d_model] * y_na + gate_ref[:, d_model:] * y_gq
    mix = jnp.dot(mixin.astype(BF16), wout_ref[...], preferred_element_type=F32)
    x1 = _layer_norm(alpha * x_ref[...] + mix, g_ref[...], b_ref[...])
    x1_ref[...] = x1
    hi = x1.astype(BF16)
    lo = (x1 - hi.astype(F32)).astype(BF16)
    wh = wrh_ref[...]
    logits = (lax.dot_general(wh, hi, _NT, preferred_element_type=F32)
              + lax.dot_general(wh, lo, _NT, preferred_element_type=F32)
              + lax.dot_general(wrl_ref[...], hi, _NT, preferred_element_type=F32))
    m = jnp.max(logits, axis=0, keepdims=True)
    e = jnp.exp(logits - m)
    aff_ref[...] = e / jnp.sum(e, axis=0, keepdims=True)


def _merge(na, gq, gates, x2d, wna, wgq, wout, ln_g, ln_b, wr_hi, wr_lo, *, alpha, tm):
    n, d = x2d.shape
    n_exp = wr_hi.shape[0]
    row = lambda i: (i, 0)
    const = lambda i: (0, 0)
    return pl.pallas_call(
        functools.partial(_merge_kernel, alpha=alpha, d_model=d),
        grid=(n // tm,),
        in_specs=[
            pl.BlockSpec((tm, NA_WIDTH), row), pl.BlockSpec((tm, GQA_WIDTH), row),
            pl.BlockSpec((tm, 2 * d), row), pl.BlockSpec((tm, d), row),
            pl.BlockSpec((NA_WIDTH, d), const), pl.BlockSpec((GQA_WIDTH, d), const),
            pl.BlockSpec((d, d), const), pl.BlockSpec((1, d), const), pl.BlockSpec((1, d), const),
            pl.BlockSpec((n_exp, d), const), pl.BlockSpec((n_exp, d), const),
        ],
        out_specs=[pl.BlockSpec((tm, d), row), pl.BlockSpec((n_exp, tm), lambda i: (0, i))],
        out_shape=[jax.ShapeDtypeStruct((n, d), F32), jax.ShapeDtypeStruct((n_exp, n), F32)],
        compiler_params=_params("parallel"),
        name="merge_ln_router",
    )(na, gq, gates, x2d, wna, wgq, wout, ln_g, ln_b, wr_hi, wr_lo)


def _route_kernel(aff_ref, pos_ref, base_ref, cnt_ref, *, cap, tile, count_chunk):
    n_exp, n = aff_ref.shape
    n_tiles = n // tile
    capf = float(cap)

    def bits_at(off, width):
        return lax.bitcast_convert_type(aff_ref[:, pl.ds(off, width)], I32)

    def count_ge(cand):
        def inner(c, acc):
            b = bits_at(pl.multiple_of(c * count_chunk, count_chunk), count_chunk)
            return acc + jnp.where(b >= cand, 1.0, 0.0)
        acc = lax.fori_loop(0, n // count_chunk, inner, jnp.zeros((n_exp, count_chunk), F32))
        return jnp.sum(acc, axis=1, keepdims=True)

    def bisect(i, prefix):
        cand = prefix | jnp.left_shift(jnp.int32(1), 30 - i)
        return jnp.where(count_ge(cand) >= capf, cand, prefix)

    thr = lax.fori_loop(0, 31, bisect, jnp.zeros((n_exp, 1), I32))
    need = capf - count_ge(thr + 1)

    ri = lax.broadcasted_iota(I32, (tile, tile), 0)
    ci = lax.broadcasted_iota(I32, (tile, tile), 1)
    upper = jnp.where(ri < ci, 1.0, 0.0).astype(BF16)
    ones = jnp.ones((tile, tile), BF16)
    tbl_lane = lax.broadcasted_iota(I32, base_ref.shape, 1)

    base_ref[...] = jnp.zeros(base_ref.shape, I32)
    cnt_ref[...] = jnp.zeros(cnt_ref.shape, I32)

    def body(i, carry):
        ceq, base = carry
        off = pl.multiple_of(i * tile, tile)
        b = lax.bitcast_convert_type(aff_ref[:, pl.ds(off, tile)], I32)
        eq = b == thr
        eqb = jnp.where(eq, 1.0, 0.0).astype(BF16)
        eqrank = jnp.dot(eqb, upper, preferred_element_type=F32) + ceq
        sel = (b > thr) | (eq & (eqrank < need))
        selb = jnp.where(sel, 1.0, 0.0).astype(BF16)
        rank_in_tile = jnp.dot(selb, upper, preferred_element_type=F32)
        cnt = jnp.dot(selb, ones, preferred_element_type=F32)
        pos_ref[:, pl.ds(off, tile)] = jnp.where(sel, base + rank_in_tile, float(UNPICKED)).astype(I32)
        base_ref[...] = jnp.where(tbl_lane == i, base[:, :1].astype(I32), base_ref[...])
        cnt_ref[...] = jnp.where(tbl_lane == i, cnt[:, :1].astype(I32), cnt_ref[...])
        padded = jnp.floor((cnt + (SUBLANES - 1.0)) * (1.0 / SUBLANES)) * SUBLANES
        return ceq + jnp.dot(eqb, ones, preferred_element_type=F32), base + padded

    zero = jnp.zeros((n_exp, tile), F32)
    _, used = lax.fori_loop(0, n_tiles, body, (zero, zero))
    base_ref[...] = jnp.where(tbl_lane == n_tiles, used[:, :1].astype(I32), base_ref[...])


def _route(aff, *, cap, tile):
    n_exp, n = aff.shape
    count_chunk = min(2048, n)
    assert n % count_chunk == 0 and n % tile == 0
    tbl_w = pl.cdiv(n // tile + 1, LANES) * LANES
    full = lambda shape: pl.BlockSpec(shape, lambda i: (0, 0))
    return pl.pallas_call(
        functools.partial(_route_kernel, cap=cap, tile=tile, count_chunk=count_chunk),
        grid=(1,),
        in_specs=[full((n_exp, n))],
        out_specs=[full((n_exp, n)), full((n_exp, tbl_w)), full((n_exp, tbl_w))],
        out_shape=[jax.ShapeDtypeStruct((n_exp, n), I32), jax.ShapeDtypeStruct((n_exp, tbl_w), I32),
                   jax.ShapeDtypeStruct((n_exp, tbl_w), I32)],
        compiler_params=_params("arbitrary"),
        name="route",
    )(aff)


def _pad_rows(c):
    return ((c + (SUBLANES - 1)) >> 3) << 3


def _start_pieces(rows, max_rows, make_copy):
    k = 3
    assert SUBLANES == 1 << k
    while (1 << k) <= max_rows:
        size = 1 << k

        @pl.when(((rows >> k) & 1) == 1)
        def _():
            make_copy(pl.multiple_of((rows >> (k + 1)) << (k + 1), SUBLANES), size).start()

        k += 1


def _wait_rows(hbm_ref, rows, sem):
    @pl.when(rows > 0)
    def _():
        view = hbm_ref.at[pl.ds(0, pl.multiple_of(rows, SUBLANES))]
        pltpu.make_async_copy(view, view, sem).wait()


def _dispatch_kernel(base_ref, cnt_ref, x_ref, aff_ref, pos_ref, xe_ref, stg, sems, *, tile, cap_pad, n_tiles):
    i = pl.program_id(0)
    n_exp = pos_ref.shape[0]
    d = x_ref.shape[1]
    xb = x_ref[...].astype(BF16)
    a = aff_ref[...]
    hi = a.astype(BF16)
    rest1 = a - hi.astype(F32)
    mid = rest1.astype(BF16)
    lo = (rest1 - mid.astype(F32)).astype(BF16)
    pr = lax.broadcasted_iota(I32, (n_exp, LANES), 0)
    pc = lax.broadcasted_iota(I32, (n_exp, LANES), 1)
    gates = sum(jnp.dot(piece, jnp.where(pc == k * n_exp + pr, 1.0, 0.0).astype(BF16), preferred_element_type=F32)
                for k, piece in enumerate((hi, mid, lo))).astype(BF16)
    few = DISPATCH_FEW_ROWS
    padded = []
    for e in range(n_exp):
        @pl.when(i > 0)
        def _():
            _wait_rows(xe_ref, _pad_rows(cnt_ref[e, jnp.maximum(i - 1, 0)]), sems.at[e])

        base = base_ref[e, i]
        rows = _pad_rows(cnt_ref[e, i])
        padded.append(rows)
        rel = pos_ref[e:e + 1, :] - base

        def fill(r0, r1):
            row_id = lax.broadcasted_iota(I32, (r1 - r0, tile), 0) + r0
            onehot = jnp.where(row_id == rel, 1.0, 0.0).astype(BF16)
            stg[e, r0:r1, 0:d] = jnp.dot(onehot, xb, preferred_element_type=F32)
            stg[e, r0:r1, d:d + LANES] = jnp.dot(onehot, gates, preferred_element_type=F32)

        @pl.when(rows > 0)
        def _():
            fill(0, few)

        @pl.when(rows > few)
        def _():
            fill(few, tile)

        dst0 = e * cap_pad + base

        def make_copy(off, size):
            return pltpu.make_async_copy(stg.at[e, pl.ds(off, size)],
                                         xe_ref.at[pl.ds(pl.multiple_of(dst0 + off, SUBLANES), size)], sems.at[e])

        _start_pieces(rows, tile, make_copy)

    @pl.when(i == n_tiles - 1)
    def _():
        for e in range(n_exp):
            _wait_rows(xe_ref, padded[e], sems.at[e])
        stg[0] = jnp.zeros((tile, stg.shape[2]), F32)
        for e in range(n_exp):
            used = base_ref[e, n_tiles]
            tail = cap_pad - used
            dst0 = e * cap_pad + used

            def zero_copy(off, size):
                return pltpu.make_async_copy(stg.at[0, pl.ds(0, size)],
                                             xe_ref.at[pl.ds(pl.multiple_of(dst0 + off, SUBLANES), size)], sems.at[0])

            def whole(j, carry):
                cp = zero_copy(j * tile, tile)
                cp.start()
                cp.wait()
                return carry

            n_whole = tail // tile
            lax.fori_loop(0, n_whole, whole, 0)
            rest = tail - n_whole * tile
            _start_pieces(rest, tile, lambda off, size: zero_copy(n_whole * tile + off, size))
            _wait_rows(xe_ref, rest, sems.at[0])


def _dispatch(x1, aff_t, pos, base_tbl, cnt_tbl, *, cap_pad, tile):
    n, d = x1.shape
    n_exp = pos.shape[0]
    assert 3 * n_exp <= LANES
    n_tiles = n // tile
    return pl.pallas_call(
        functools.partial(_dispatch_kernel, tile=tile, cap_pad=cap_pad, n_tiles=n_tiles),
        grid_spec=pltpu.PrefetchScalarGridSpec(
            num_scalar_prefetch=2,
            grid=(n_tiles,),
            in_specs=[pl.BlockSpec((tile, d), lambda i, b, c: (i, 0)),
                      pl.BlockSpec((tile, n_exp), lambda i, b, c: (i, 0)),
                      pl.BlockSpec((n_exp, tile), lambda i, b, c: (0, i))],
            out_specs=pl.BlockSpec(memory_space=pl.ANY),
            scratch_shapes=[pltpu.VMEM((n_exp, tile, d + LANES), F32), pltpu.SemaphoreType.DMA((n_exp,))],
        ),
        out_shape=jax.ShapeDtypeStruct((n_exp * cap_pad, d + LANES), F32),
        compiler_params=_params("arbitrary"),
        name="dispatch",
    )(base_tbl, cnt_tbl, x1, aff_t, pos)


def _ffn_kernel(base_ref, xe_ref, wg_ref, wu_ref, wd_ref, ye_ref, *, f_chunk, n_tiles):
    tm = xe_ref.shape[0]
    d = ye_ref.shape[1]
    n_exp = pl.num_programs(0)
    e = pl.program_id(0)
    used = base_ref[e, n_tiles]
    live = pl.program_id(1) * tm < used

    @pl.when(live)
    def _():
        xb = xe_ref[:, 0:d].astype(BF16)
        lane = lax.broadcasted_iota(I32, (tm, LANES), 1)
        mine = (lane % n_exp == e) & (lane < 3 * n_exp)
        gate = jnp.sum(jnp.where(mine, xe_ref[:, d:d + LANES], 0.0), axis=1, keepdims=True)
        d_ff = wg_ref.shape[2]
        acc = jnp.zeros(ye_ref.shape, F32)
        for c in range(d_ff // f_chunk):
            cols = slice(c * f_chunk, (c + 1) * f_chunk)
            g = jnp.dot(xb, wg_ref[0, :, cols], preferred_element_type=F32)
            u = jnp.dot(xb, wu_ref[0, :, cols], preferred_element_type=F32)
            h = (g * jax.nn.sigmoid(g)) * u
            acc = acc + jnp.dot(h.astype(BF16), wd_ref[0, cols, :], preferred_element_type=F32)
        ye_ref[...] = acc * gate

    @pl.when(jnp.logical_not(live))
    def _():
        ye_ref[...] = jnp.zeros(ye_ref.shape, F32)


def _ffn(xe, base_tbl, wg, wu, wd, *, cap_pad, tm, f_chunk, n_tiles):
    n_slots, d_ext = xe.shape
    n_exp, d, d_ff = wg.shape
    assert cap_pad % tm == 0 and d_ff % f_chunk == 0 and d_ext == d + LANES
    kblocks = cap_pad // tm
    return pl.pallas_call(
        functools.partial(_ffn_kernel, f_chunk=f_chunk, n_tiles=n_tiles),
        grid_spec=pltpu.PrefetchScalarGridSpec(
            num_scalar_prefetch=1,
            grid=(n_exp, kblocks),
            in_specs=[
                pl.BlockSpec((tm, d_ext), lambda e, k, b: (e * kblocks + k, 0)),
                pl.BlockSpec((1, d, d_ff), lambda e, k, b: (e, 0, 0)),
                pl.BlockSpec((1, d, d_ff), lambda e, k, b: (e, 0, 0)),
                pl.BlockSpec((1, d_ff, d), lambda e, k, b: (e, 0, 0)),
            ],
            out_specs=pl.BlockSpec((tm, d), lambda e, k, b: (e * kblocks + k, 0)),
        ),
        out_shape=jax.ShapeDtypeStruct((n_slots, d), F32),
        compiler_params=_params("parallel", "parallel"),
        name="expert_ffn",
    )(base_tbl, xe, wg, wu, wd)


def _combine_kernel(base_ref, cnt_ref, x_ref, pos_ref, g_ref, b_ref, ye_ref, o_ref,
                    stg, sems, *, tile, cap_pad, n_tiles, alpha):
    i = pl.program_id(0)
    n_exp = pos_ref.shape[1]
    half = stg.shape[0] // 2

    @pl.when(i == 0)
    def _():
        stg[...] = jnp.zeros(stg.shape, F32)

    def layout(j):
        segs, off = [], jnp.int32(0)
        for e in range(n_exp):
            rows = _pad_rows(cnt_ref[e, j])
            segs.append((base_ref[e, j], rows, off))
            off = off + rows
        return segs, off

    def first_row(j, total):
        return jnp.where(total > half, 0, (j % 2) * half)

    def fetch(j, segs, row0):
        for e, (base, rows, off) in enumerate(segs):
            src0 = e * cap_pad + base
            dst0 = row0 + off

            def make_copy(o, size):
                return pltpu.make_async_copy(ye_ref.at[pl.ds(pl.multiple_of(src0 + o, SUBLANES), size)],
                                             stg.at[pl.ds(pl.multiple_of(dst0 + o, SUBLANES), size)], sems.at[j % 2])

            _start_pieces(rows, tile, make_copy)

    nxt = jnp.minimum(i + 1, n_tiles - 1)
    segs, total = layout(i)
    segs_next, total_next = layout(nxt)
    _, total_prev = layout(jnp.maximum(i - 1, 0))
    row0 = first_row(i, total)
    small, small_next, small_prev = total <= half, total_next <= half, total_prev <= half

    @pl.when(jnp.logical_not((i > 0) & small_prev & small))
    def _():
        fetch(i, segs, row0)

    _wait_rows(ye_ref, total, sems.at[i % 2])

    @pl.when((i + 1 < n_tiles) & small & small_next)
    def _():
        fetch(nxt, segs_next, first_row(nxt, total_next))

    o_ref[...] = alpha * x_ref[...]
    slot_id = lax.broadcasted_iota(I32, (tile, tile), 1)

    def block(b, carry):
        chunk = stg[pl.ds(pl.multiple_of(row0 + b * tile, tile), tile), :].astype(BF16)
        hit = None
        for e, (base, rows, off) in enumerate(segs):
            match = slot_id == (pos_ref[:, e:e + 1] - (base - off + b * tile))
            hit = match if hit is None else hit | match
        onehot = jnp.where(hit, 1.0, 0.0).astype(BF16)
        o_ref[...] += jnp.dot(onehot, chunk, preferred_element_type=F32)
        return carry

    lax.fori_loop(0, (total + (tile - 1)) // tile, block, 0)
    o_ref[...] = _layer_norm(o_ref[...], g_ref[...], b_ref[...])


def _combine(x1, pos_t, base_tbl, cnt_tbl, ln_g, ln_b, ye, *, cap_pad, alpha, tile):
    n, d = x1.shape
    n_exp = pos_t.shape[1]
    row = lambda i, b, c: (i, 0)
    const = lambda i, b, c: (0, 0)
    stg_rows = n_exp * tile
    return pl.pallas_call(
        functools.partial(_combine_kernel, tile=tile, cap_pad=cap_pad, n_tiles=n // tile, alpha=alpha),
        grid_spec=pltpu.PrefetchScalarGridSpec(
            num_scalar_prefetch=2,
            grid=(n // tile,),
            in_specs=[
                pl.BlockSpec((tile, d), row), pl.BlockSpec((tile, n_exp), row),
                pl.BlockSpec((1, d), const), pl.BlockSpec((1, d), const),
                pl.BlockSpec(memory_space=pl.ANY),
            ],
            out_specs=pl.BlockSpec((tile, d), row),
            scratch_shapes=[pltpu.VMEM((stg_rows, d), F32), pltpu.SemaphoreType.DMA((2,))],
        ),
        out_shape=jax.ShapeDtypeStruct((n, d), F32),
        compiler_params=_params("arbitrary"),
        name="combine_ln",
    )(base_tbl, cnt_tbl, x1, pos_t, ln_g, ln_b, ye)


def _rope_tables(seq):
    t = jnp.arange(seq)
    row = (t // GRID_W).astype(F32)
    col = (t % GRID_W).astype(F32)
    half = HEAD_DIM // 2
    inv_freq = ROPE_THETA ** (-jnp.arange(0, half, 2, dtype=F32) / half)
    ang_r = row[:, None] * inv_freq[None, :]
    ang_c = col[:, None] * inv_freq[None, :]
    ang = jnp.concatenate([ang_r, ang_r, ang_c, ang_c], axis=-1)
    sign = jnp.where((jnp.arange(HEAD_DIM) % half) < half // 2, -1.0, 1.0).astype(F32)
    reps = LANES // HEAD_DIM
    return jnp.tile(jnp.cos(ang), (1, reps)), jnp.tile(jnp.sin(ang) * sign[None, :], (1, reps))


def _gqa_slot_columns():
    heads = [p + GQA_GROUP * half for p in range(HEAD_PAIRS) for half in range(2)]
    return np.concatenate([np.arange(HEAD_DIM) + HEAD_DIM * h for h in heads])


def _prep_layer(w_in, na_rpb, q_norm, k_norm, w_br_na, w_br_gqa, w_out, ln1_g, ln1_b,
                w_router, w_e_gate, w_e_up, w_e_down, ln2_g, ln2_b):
    d = w_in.shape[0]
    s_na = 3 * NA_WIDTH
    cols = _gqa_slot_columns()
    perm = np.concatenate([np.arange(s_na), s_na + cols, np.arange(s_na + GQA_WIDTH, w_in.shape[1])])
    reps = LANES // HEAD_DIM
    wr_t = w_router.T.astype(F32)
    wr_hi = wr_t.astype(BF16)
    gm = np.kron(np.eye(reps, dtype=np.float32), np.full((HEAD_DIM, HEAD_DIM), 1.0 / HEAD_DIM, np.float32))
    return dict(
        w_in=w_in[:, perm].astype(BF16),
        bias_tbl=_na_bias_table(na_rpb),
        qn=jnp.tile(q_norm.astype(F32), reps)[None, :], kn=jnp.tile(k_norm.astype(F32), reps)[None, :],
        gm=jnp.asarray(gm, BF16),
        wna=w_br_na.astype(BF16), wgq=w_br_gqa[cols].astype(BF16), wout=w_out.astype(BF16),
        ln1_g=ln1_g.astype(F32).reshape(1, d), ln1_b=ln1_b.astype(F32).reshape(1, d),
        wr_hi=wr_hi, wr_lo=(wr_t - wr_hi.astype(F32)).astype(BF16),
        wg=w_e_gate.astype(BF16), wu=w_e_up.astype(BF16), wd=w_e_down.astype(BF16),
        ln2_g=ln2_g.astype(F32).reshape(1, d), ln2_b=ln2_b.astype(F32).reshape(1, d),
    )


def _tiles(seq):
    return dict(tm_proj=512, tq=256, tk=1024, tm_merge=256, t_moe=256, tm_ffn=512, f_chunk=512)


def _trunk_layer(x2d, p, *, seq, alpha, rope):
    n, d = x2d.shape
    n_exp = p["wg"].shape[0]
    cap = EC_CAPACITY * n // n_exp
    tl = _tiles(seq)
    cos, sin = rope
    naq, nak, nav, gq, gk, gv, gates = _inproj(x2d, p["w_in"], cos, sin, p["qn"], p["kn"], p["gm"],
                                               seq=seq, tm=min(tl["tm_proj"], seq))
    na = _na_attention(naq, nak, nav, p["bias_tbl"], seq=seq)
    nb = n // seq
    ones_rows = jnp.zeros((nb, VT_ROWS - KV_WIDTH, seq), BF16).at[:, 0, :].set(1.0)
    gvt = jnp.concatenate([gv.reshape(nb, seq, KV_WIDTH).transpose(0, 2, 1), ones_rows], axis=1)
    gvt = gvt.reshape(nb * VT_ROWS, seq)
    ga = _gqa_attention(gq, gk, gvt, seq=seq, tq=min(tl["tq"], seq), tk=min(tl["tk"], seq))
    x1, aff = _merge(na, ga, gates, x2d, p["wna"], p["wgq"], p["wout"], p["ln1_g"], p["ln1_b"],
                     p["wr_hi"], p["wr_lo"], alpha=alpha, tm=tl["tm_merge"])
    tile = tl["t_moe"]
    n_tiles = n // tile
    tm_ffn = min(tl["tm_ffn"], cap)
    cap_pad = pl.cdiv(cap + (SUBLANES - 1) * n_tiles, tm_ffn) * tm_ffn
    pos, base_tbl, cnt_tbl = _route(aff, cap=cap, tile=tile)
    xe = _dispatch(x1, aff.T, pos, base_tbl, cnt_tbl, cap_pad=cap_pad, tile=tile)
    ye = _ffn(xe, base_tbl, p["wg"], p["wu"], p["wd"], cap_pad=cap_pad, tm=tm_ffn, f_chunk=tl["f_chunk"],
              n_tiles=n_tiles)
    return _combine(x1, pos.T, base_tbl, cnt_tbl, p["ln2_g"], p["ln2_b"], ye,
                    cap_pad=cap_pad, alpha=alpha, tile=tile)


@jax.jit
def kernel(x_prompt, x_sample, w_in, na_rpb, q_norm, k_norm, w_br_na, w_br_gqa, w_out, ln1_g, ln1_b,
           w_router, w_e_gate, w_e_up, w_e_down, ln2_g, ln2_b):
    depth = w_in.shape[0]
    alpha = float((2 * depth) ** 0.25)
    d = x_prompt.shape[-1]
    groups = []
    for x in (x_prompt, x_sample):
        b, s, _ = x.shape
        groups.append(dict(x=x.reshape(b * s, d), shape=x.shape, seq=s, rope=_rope_tables(s)))
    for l in range(depth):
        p = _prep_layer(w_in[l], na_rpb[l], q_norm[l], k_norm[l], w_br_na[l], w_br_gqa[l], w_out[l],
                        ln1_g[l], ln1_b[l], w_router[l], w_e_gate[l], w_e_up[l], w_e_down[l],
                        ln2_g[l], ln2_b[l])
        for g in groups:
            g["x"] = _trunk_layer(g["x"], p, seq=g["seq"], alpha=alpha, rope=g["rope"])
    return tuple(g["x"].reshape(g["shape"]) for g in groups)
```

```python
import functools

import jax
import jax.numpy as jnp
import numpy as np
from jax import lax
from jax.experimental import pallas as pl
from jax.experimental.pallas import tpu as pltpu

F32 = jnp.float32
BF16 = jnp.bfloat16
I32 = jnp.int32

GRID_W = 64
HEAD_DIM = 64
NA_HEADS = 8
NA_WIN_H = 8
NA_WIN_W = 16
GQA_Q_HEADS = 8
GQA_KV_HEADS = 2
GQA_GROUP = GQA_Q_HEADS // GQA_KV_HEADS
ROPE_THETA = 10000.0
EC_CAPACITY = 2
LN_EPS = 1e-5
RMS_EPS = 1e-6

LANES = 128
SUBLANES = 8
NA_WIDTH = NA_HEADS * HEAD_DIM
GQA_WIDTH = GQA_Q_HEADS * HEAD_DIM
KV_WIDTH = GQA_KV_HEADS * HEAD_DIM
HEAD_PAIRS = GQA_WIDTH // LANES
BF16_TILE_ROWS = 2 * SUBLANES
VT_ROWS = KV_WIDTH + BF16_TILE_ROWS
UNPICKED = -(1 << 30)
DISPATCH_FEW_ROWS = 64
GQA_Q_SCALE = HEAD_DIM ** -0.5 * float(np.log2(np.e))
NEG_BIG = -1e30
VMEM_LIMIT_BYTES = 56 * 1024 * 1024

_NT = (((1,), (1,)), ((), ()))


def _params(*sem):
    return pltpu.CompilerParams(dimension_semantics=sem, vmem_limit_bytes=VMEM_LIMIT_BYTES)


def _inproj_kernel(x_ref, w_ref, cos_ref, sin_ref, qn_ref, kn_ref, gm_ref,
                   naq_ref, nak_ref, nav_ref, gq_ref, gk_ref, gv_ref, gate_ref, *, d_model):
    xb = x_ref[...].astype(BF16)
    s_na = 3 * NA_WIDTH
    s_gq = s_na + GQA_WIDTH
    s_gk = s_gq + KV_WIDTH
    s_gv = s_gk + KV_WIDTH

    def proj(c0, width):
        return jnp.dot(xb, w_ref[:, c0:c0 + width], preferred_element_type=F32)

    naq_ref[...] = (proj(0, NA_WIDTH) * HEAD_DIM ** -0.5).astype(BF16)
    nak_ref[...] = proj(NA_WIDTH, NA_WIDTH).astype(BF16)
    nav_ref[...] = proj(2 * NA_WIDTH, NA_WIDTH).astype(BF16)

    cos = cos_ref[...]
    sin = sin_ref[...]
    gm = gm_ref[...]
    lane = lax.broadcasted_iota(I32, cos.shape, 1)
    first_half = (lane % (HEAD_DIM // 2)) < (HEAD_DIM // 4)

    def norm_rope(a, gain):
        sq = a * a
        hi = sq.astype(BF16)
        lo = (sq - hi.astype(F32)).astype(BF16)
        ms = jnp.dot(hi, gm, preferred_element_type=F32) + jnp.dot(lo, gm, preferred_element_type=F32)
        an = a * lax.rsqrt(ms + RMS_EPS) * gain
        quarter = HEAD_DIM // 4
        rot = jnp.where(first_half, pltpu.roll(an, LANES - quarter, 1), pltpu.roll(an, quarter, 1))
        return an * cos + rot * sin

    qn = qn_ref[...]
    for p in range(HEAD_PAIRS):
        a = proj(s_na + p * LANES, LANES)
        gq_ref[:, p * LANES:(p + 1) * LANES] = (norm_rope(a, qn) * GQA_Q_SCALE).astype(BF16)
    gk_ref[...] = norm_rope(proj(s_gq, KV_WIDTH), kn_ref[...]).astype(BF16)
    gv_ref[...] = proj(s_gk, KV_WIDTH).astype(BF16)
    gate_chunk = 512
    for c in range(2 * d_model // gate_chunk):
        g = proj(s_gv + c * gate_chunk, gate_chunk)
        gate_ref[:, c * gate_chunk:(c + 1) * gate_chunk] = jax.nn.sigmoid(g)


def _inproj(x2d, w_in_b, cos, sin, qn, kn, gm, *, seq, tm):
    n, d = x2d.shape
    d_in = w_in_b.shape[1]
    assert n % tm == 0 and seq % tm == 0
    sblocks = seq // tm
    row = lambda i: (i, 0)
    const = lambda i: (0, 0)
    pos = lambda i: (i % sblocks, 0)
    out_shape = [
        jax.ShapeDtypeStruct((n, NA_WIDTH), BF16), jax.ShapeDtypeStruct((n, NA_WIDTH), BF16),
        jax.ShapeDtypeStruct((n, NA_WIDTH), BF16), jax.ShapeDtypeStruct((n, GQA_WIDTH), BF16),
        jax.ShapeDtypeStruct((n, KV_WIDTH), BF16), jax.ShapeDtypeStruct((n, KV_WIDTH), BF16),
        jax.ShapeDtypeStruct((n, 2 * d), F32),
    ]
    out_specs = [
        pl.BlockSpec((tm, NA_WIDTH), row), pl.BlockSpec((tm, NA_WIDTH), row), pl.BlockSpec((tm, NA_WIDTH), row),
        pl.BlockSpec((tm, GQA_WIDTH), row), pl.BlockSpec((tm, KV_WIDTH), row), pl.BlockSpec((tm, KV_WIDTH), row),
        pl.BlockSpec((tm, 2 * d), row),
    ]
    return pl.pallas_call(
        functools.partial(_inproj_kernel, d_model=d),
        grid=(n // tm,),
        in_specs=[
            pl.BlockSpec((tm, d), row), pl.BlockSpec((d, d_in), const),
            pl.BlockSpec((tm, LANES), pos), pl.BlockSpec((tm, LANES), pos),
            pl.BlockSpec((1, LANES), const), pl.BlockSpec((1, LANES), const),
            pl.BlockSpec((LANES, LANES), const),
        ],
        out_specs=out_specs,
        out_shape=out_shape,
        compiler_params=_params("parallel"),
        name="inproj",
    )(x2d, w_in_b, cos, sin, qn, kn, gm)


def _na_kernel(q_ref, kp_ref, kc_ref, kn_ref, vp_ref, vc_ref, vn_ref, bias_ref, o_ref,
               kbuf, vbuf, *, rows, nrb):
    blk = NA_WIN_H * GRID_W
    j = pl.program_id(0) % nrb
    kbuf[0:blk] = kp_ref[...]
    kbuf[blk:2 * blk] = kc_ref[...]
    kbuf[2 * blk:3 * blk] = kn_ref[...]
    vbuf[0:blk] = vp_ref[...]
    vbuf[blk:2 * blk] = vc_ref[...]
    vbuf[2 * blk:3 * blk] = vn_ref[...]
    lane = lax.broadcasted_iota(I32, (GRID_W, LANES), 1)
    low = lane < HEAD_DIM

    def row_body(i, carry):
        r = j * NA_WIN_H + i
        rs = jnp.clip(r - NA_WIN_H // 2, 0, rows - NA_WIN_H)
        d0 = rs - r + (NA_WIN_H - 1)
        off = pl.multiple_of((rs - (j - 1) * NA_WIN_H) * GRID_W, GRID_W)
        qoff = pl.multiple_of(i * GRID_W, GRID_W)
        for p in range(NA_WIDTH // LANES):
            cols = slice(p * LANES, (p + 1) * LANES)
            q2 = q_ref[pl.ds(qoff, GRID_W), cols]
            k2 = kbuf[pl.ds(off, blk), cols]
            v2 = vbuf[pl.ds(off, blk), cols]
            zero = jnp.zeros_like(q2)
            qs = jnp.concatenate([jnp.where(low, q2, zero), jnp.where(low, zero, q2)], axis=0)
            st = lax.dot_general(k2, qs, _NT, preferred_element_type=F32) + bias_ref[d0, p]
            e = jnp.exp(st - jnp.max(st, axis=0, keepdims=True))
            prob = e * (1.0 / jnp.sum(e, axis=0, keepdims=True))
            o = jnp.dot(prob.T.astype(BF16), v2, preferred_element_type=F32)
            o_ref[pl.ds(qoff, GRID_W), cols] = jnp.where(low, o[:GRID_W], o[GRID_W:]).astype(BF16)
        return carry

    lax.fori_loop(0, NA_WIN_H, row_body, 0)


def _na_bias_table(rpb):
    c = jnp.arange(GRID_W)
    cs = jnp.clip(c - NA_WIN_W // 2, 0, GRID_W - NA_WIN_W)
    cc = jnp.arange(GRID_W)
    inwin = (cc[None, :] >= cs[:, None]) & (cc[None, :] < cs[:, None] + NA_WIN_W)
    dc = jnp.clip(cc[None, :] - c[:, None] + (NA_WIN_W - 1), 0, 2 * NA_WIN_W - 2)
    full = jnp.where(inwin[None, None], rpb[:, :, dc].astype(F32), NEG_BIG)
    tbl = jnp.stack([full[:, d0:d0 + NA_WIN_H] for d0 in range(NA_WIN_H)])
    tbl = tbl.reshape(NA_WIN_H, NA_HEADS // 2, 2, NA_WIN_H, GRID_W, GRID_W)
    return tbl.transpose(0, 1, 3, 5, 2, 4).reshape(NA_WIN_H, NA_HEADS // 2, NA_WIN_H * GRID_W, 2 * GRID_W)


def _na_attention(q, k, v, bias_tbl, *, seq):
    n = q.shape[0]
    rows = seq // GRID_W
    assert rows % NA_WIN_H == 0 and rows >= 2 * NA_WIN_H
    nrb = rows // NA_WIN_H
    blk = NA_WIN_H * GRID_W

    def cur(g):
        return (g, 0)

    def prev(g):
        return (g - jnp.where(g % nrb == 0, 0, 1), 0)

    def nxt(g):
        return (g + jnp.where(g % nrb == nrb - 1, 0, 1), 0)

    spec = lambda f: pl.BlockSpec((blk, NA_WIDTH), f)
    return pl.pallas_call(
        functools.partial(_na_kernel, rows=rows, nrb=nrb),
        grid=(n // blk,),
        in_specs=[spec(cur), spec(prev), spec(cur), spec(nxt), spec(prev), spec(cur), spec(nxt),
                  pl.BlockSpec(bias_tbl.shape, lambda g: (0, 0, 0, 0))],
        out_specs=spec(cur),
        out_shape=jax.ShapeDtypeStruct((n, NA_WIDTH), BF16),
        scratch_shapes=[pltpu.VMEM((3 * blk, NA_WIDTH), BF16), pltpu.VMEM((3 * blk, NA_WIDTH), BF16)],
        compiler_params=_params("parallel"),
        name="na_attention",
    )(q, k, k, k, v, v, v, bias_tbl)


def _gqa_kernel(q_ref, k_ref, vt_ref, o_ref, qs_sc, sa_sc, sb_sc, m_sc, acc_sc, *, tq, tk, seq):
    lane = lax.broadcasted_iota(I32, (tq, LANES), 1)
    low = lane < HEAD_DIM
    for p in range(HEAD_PAIRS):
        q2 = q_ref[:, p * LANES:(p + 1) * LANES]
        zero = jnp.zeros_like(q2)
        qs_sc[(2 * p) * tq:(2 * p + 1) * tq, :] = jnp.where(low, q2, zero)
        qs_sc[(2 * p + 1) * tq:(2 * p + 2) * tq, :] = jnp.where(low, zero, q2)
    m_sc[...] = jnp.full(m_sc.shape, -jnp.inf, F32)
    acc_sc[...] = jnp.zeros(acc_sc.shape, F32)

    def scores(c, st_ref):
        koff = pl.multiple_of(c * tk, tk)
        st_ref[...] = lax.dot_general(k_ref[pl.ds(koff, tk), :], qs_sc[...], _NT, preferred_element_type=F32)

    def consume(c, st_ref):
        koff = pl.multiple_of(c * tk, tk)
        st = st_ref[...]
        m_old = m_sc[...]
        m_new = jnp.maximum(m_old, jnp.max(st, axis=0, keepdims=True))
        alpha = jnp.exp2(m_old - m_new)
        e = jnp.exp2(st - m_new).astype(BF16)
        pv = jnp.dot(vt_ref[:, pl.ds(koff, tk)], e, preferred_element_type=F32)
        acc_sc[...] = alpha * acc_sc[...] + pv
        m_sc[...] = m_new

    n_chunks = seq // tk
    assert n_chunks % 2 == 0
    scores(0, sa_sc)

    def body(j, carry):
        scores(2 * j + 1, sb_sc)
        consume(2 * j, sa_sc)
        scores(2 * j + 2, sa_sc)
        consume(2 * j + 1, sb_sc)
        return carry

    lax.fori_loop(0, n_chunks // 2 - 1, body, 0)
    scores(n_chunks - 1, sb_sc)
    consume(n_chunks - 2, sa_sc)
    consume(n_chunks - 1, sb_sc)
    o_t = acc_sc[0:KV_WIDTH, :] * (1.0 / acc_sc[KV_WIDTH:KV_WIDTH + 1, :])
    top = lax.broadcasted_iota(I32, (KV_WIDTH, tq), 0) < HEAD_DIM
    for p in range(HEAD_PAIRS):
        a = o_t[:, (2 * p) * tq:(2 * p + 1) * tq]
        b = o_t[:, (2 * p + 1) * tq:(2 * p + 2) * tq]
        o_ref[:, p * LANES:(p + 1) * LANES] = jnp.where(top, a, b).T.astype(BF16)


def _gqa_attention(q, k, vt, *, seq, tq, tk):
    n = q.shape[0]
    assert seq % tq == 0 and seq % tk == 0
    qblocks = seq // tq
    width = GQA_Q_HEADS * tq
    return pl.pallas_call(
        functools.partial(_gqa_kernel, tq=tq, tk=tk, seq=seq),
        grid=(n // seq, qblocks),
        in_specs=[
            pl.BlockSpec((tq, GQA_WIDTH), lambda b, i: (b * qblocks + i, 0)),
            pl.BlockSpec((seq, KV_WIDTH), lambda b, i: (b, 0)),
            pl.BlockSpec((VT_ROWS, seq), lambda b, i: (b, 0)),
        ],
        out_specs=pl.BlockSpec((tq, GQA_WIDTH), lambda b, i: (b * qblocks + i, 0)),
        out_shape=jax.ShapeDtypeStruct((n, GQA_WIDTH), BF16),
        scratch_shapes=[pltpu.VMEM((width, KV_WIDTH), BF16), pltpu.VMEM((tk, width), F32),
                        pltpu.VMEM((tk, width), F32), pltpu.VMEM((1, width), F32),
                        pltpu.VMEM((VT_ROWS, width), F32)],
        compiler_params=_params("parallel", "parallel"),
        name="gqa_attention",
    )(q, k, vt)


def _layer_norm(h, g, b):
    mu = jnp.mean(h, axis=-1, keepdims=True)
    hc = h - mu
    var = jnp.mean(hc * hc, axis=-1, keepdims=True)
    return hc * lax.rsqrt(var + LN_EPS) * g + b


def _merge_kernel(na_ref, gq_ref, gate_ref, x_ref, wna_ref, wgq_ref, wout_ref, g_ref, b_ref,
                  wrh_ref, wrl_ref, x1_ref, aff_ref, *, alpha, d_model):
    y_na = jnp.dot(na_ref[...], wna_ref[...], preferred_element_type=F32)
    y_gq = jnp.dot(gq_ref[...], wgq_ref[...], preferred_element_type=F32)
    mixin = gate_ref[:, :d_model] * y_na + gate_ref[:, d_model:] * y_gq
    mix = jnp.dot(mixin.astype(BF16), wout_ref[...], preferred_element_type=F32)
    x1 = _layer_norm(alpha * x_ref[...] + mix, g_ref[...], b_ref[...])
    x1_ref[...] = x1
    hi = x1.astype(BF16)
    lo = (x1 - hi.astype(F32)).astype(BF16)
    wh = wrh_ref[...]
    logits = (lax.dot_general(wh, hi, _NT, preferred_element_type=F32)
              + lax.dot_general(wh, lo, _NT, preferred_element_type=F32)
              + lax.dot_general(wrl_ref[...], hi, _NT, preferred_element_type=F32))
    m = jnp.max(logits, axis=0, keepdims=True)
    e = jnp.exp(logits - m)
    aff_ref[...] = e / jnp.sum(e, axis=0, keepdims=True)


def _merge(na, gq, gates, x2d, wna, wgq, wout, ln_g, ln_b, wr_hi, wr_lo, *, alpha, tm):
    n, d = x2d.shape
    n_exp = wr_hi.shape[0]
    row = lambda i: (i, 0)
    const = lambda i: (0, 0)
    return pl.pallas_call(
        functools.partial(_merge_kernel, alpha=alpha, d_model=d),
        grid=(n // tm,),
        in_specs=[
            pl.BlockSpec((tm, NA_WIDTH), row), pl.BlockSpec((tm, GQA_WIDTH), row),
            pl.BlockSpec((tm, 2 * d), row), pl.BlockSpec((tm, d), row),
            pl.BlockSpec((NA_WIDTH, d), const), pl.BlockSpec((GQA_WIDTH, d), const),
            pl.BlockSpec((d, d), const), pl.BlockSpec((1, d), const), pl.BlockSpec((1, d), const),
            pl.BlockSpec((n_exp, d), const), pl.BlockSpec((n_exp, d), const),
        ],
        out_specs=[pl.BlockSpec((tm, d), row), pl.BlockSpec((n_exp, tm), lambda i: (0, i))],
        out_shape=[jax.ShapeDtypeStruct((n, d), F32), jax.ShapeDtypeStruct((n_exp, n), F32)],
        compiler_params=_params("parallel"),
        name="merge_ln_router",
    )(na, gq, gates, x2d, wna, wgq, wout, ln_g, ln_b, wr_hi, wr_lo)


def _route_kernel(aff_ref, pos_ref, base_ref, cnt_ref, *, cap, tile, count_chunk):
    n_exp, n = aff_ref.shape
    n_tiles = n // tile
    capf = float(cap)

    def bits_at(off, width):
        return lax.bitcast_convert_type(aff_ref[:, pl.ds(off, width)], I32)

    def count_ge(cand):
        def inner(c, acc):
            b = bits_at(pl.multiple_of(c * count_chunk, count_chunk), count_chunk)
            return acc + jnp.where(b >= cand, 1.0, 0.0)
        acc = lax.fori_loop(0, n // count_chunk, inner, jnp.zeros((n_exp, count_chunk), F32))
        return jnp.sum(acc, axis=1, keepdims=True)

    def bisect(i, prefix):
        cand = prefix | jnp.left_shift(jnp.int32(1), 30 - i)
        return jnp.where(count_ge(cand) >= capf, cand, prefix)

    thr = lax.fori_loop(0, 31, bisect, jnp.zeros((n_exp, 1), I32))
    need = capf - count_ge(thr + 1)

    ri = lax.broadcasted_iota(I32, (tile, tile), 0)
    ci = lax.broadcasted_iota(I32, (tile, tile), 1)
    upper = jnp.where(ri < ci, 1.0, 0.0).astype(BF16)
    ones = jnp.ones((tile, tile), BF16)
    tbl_lane = lax.broadcasted_iota(I32, base_ref.shape, 1)

    base_ref[...] = jnp.zeros(base_ref.shape, I32)
    cnt_ref[...] = jnp.zeros(cnt_ref.shape, I32)

    def body(i, carry):
        ceq, base = carry
        off = pl.multiple_of(i * tile, tile)
        b = lax.bitcast_convert_type(aff_ref[:, pl.ds(off, tile)], I32)
        eq = b == thr
        eqb = jnp.where(eq, 1.0, 0.0).astype(BF16)
        eqrank = jnp.dot(eqb, upper, preferred_element_type=F32) + ceq
        sel = (b > thr) | (eq & (eqrank < need))
        selb = jnp.where(sel, 1.0, 0.0).astype(BF16)
        rank_in_tile = jnp.dot(selb, upper, preferred_element_type=F32)
        cnt = jnp.dot(selb, ones, preferred_element_type=F32)
        pos_ref[:, pl.ds(off, tile)] = jnp.where(sel, base + rank_in_tile, float(UNPICKED)).astype(I32)
        base_ref[...] = jnp.where(tbl_lane == i, base[:, :1].astype(I32), base_ref[...])
        cnt_ref[...] = jnp.where(tbl_lane == i, cnt[:, :1].astype(I32), cnt_ref[...])
        padded = jnp.floor((cnt + (SUBLANES - 1.0)) * (1.0 / SUBLANES)) * SUBLANES
        return ceq + jnp.dot(eqb, ones, preferred_element_type=F32), base + padded

    zero = jnp.zeros((n_exp, tile), F32)
    _, used = lax.fori_loop(0, n_tiles, body, (zero, zero))
    base_ref[...] = jnp.where(tbl_lane == n_tiles, used[:, :1].astype(I32), base_ref[...])


def _route(aff, *, cap, tile):
    n_exp, n = aff.shape
    count_chunk = min(2048, n)
    assert n % count_chunk == 0 and n % tile == 0
    tbl_w = pl.cdiv(n // tile + 1, LANES) * LANES
    full = lambda shape: pl.BlockSpec(shape, lambda i: (0, 0))
    return pl.pallas_call(
        functools.partial(_route_kernel, cap=cap, tile=tile, count_chunk=count_chunk),
        grid=(1,),
        in_specs=[full((n_exp, n))],
        out_specs=[full((n_exp, n)), full((n_exp, tbl_w)), full((n_exp, tbl_w))],
        out_shape=[jax.ShapeDtypeStruct((n_exp, n), I32), jax.ShapeDtypeStruct((n_exp, tbl_w), I32),
                   jax.ShapeDtypeStruct((n_exp, tbl_w), I32)],
        compiler_params=_params("arbitrary"),
        name="route",
    )(aff)


def _pad_rows(c):
    return ((c + (SUBLANES - 1)) >> 3) << 3


def _start_pieces(rows, max_rows, make_copy):
    k = 3
    assert SUBLANES == 1 << k
    while (1 << k) <= max_rows:
        size = 1 << k

        @pl.when(((rows >> k) & 1) == 1)
        def _():
            make_copy(pl.multiple_of((rows >> (k + 1)) << (k + 1), SUBLANES), size).start()

        k += 1


def _wait_rows(hbm_ref, rows, sem):
    @pl.when(rows > 0)
    def _():
        view = hbm_ref.at[pl.ds(0, pl.multiple_of(rows, SUBLANES))]
        pltpu.make_async_copy(view, view, sem).wait()


def _dispatch_kernel(base_ref, cnt_ref, x_ref, aff_ref, pos_ref, xe_ref, stg, sems, *, tile, cap_pad, n_tiles):
    i = pl.program_id(0)
    n_exp = pos_ref.shape[0]
    d = x_ref.shape[1]
    xb = x_ref[...].astype(BF16)
    a = aff_ref[...]
    hi = a.astype(BF16)
    rest1 = a - hi.astype(F32)
    mid = rest1.astype(BF16)
    lo = (rest1 - mid.astype(F32)).astype(BF16)
    pr = lax.broadcasted_iota(I32, (n_exp, LANES), 0)
    pc = lax.broadcasted_iota(I32, (n_exp, LANES), 1)
    gates = sum(jnp.dot(piece, jnp.where(pc == k * n_exp + pr, 1.0, 0.0).astype(BF16), preferred_element_type=F32)
                for k, piece in enumerate((hi, mid, lo))).astype(BF16)
    few = DISPATCH_FEW_ROWS
    padded = []
    for e in range(n_exp):
        @pl.when(i > 0)
        def _():
            _wait_rows(xe_ref, _pad_rows(cnt_ref[e, jnp.maximum(i - 1, 0)]), sems.at[e])

        base = base_ref[e, i]
        rows = _pad_rows(cnt_ref[e, i])
        padded.append(rows)
        rel = pos_ref[e:e + 1, :] - base

        def fill(r0, r1):
            row_id = lax.broadcasted_iota(I32, (r1 - r0, tile), 0) + r0
            onehot = jnp.where(row_id == rel, 1.0, 0.0).astype(BF16)
            stg[e, r0:r1, 0:d] = jnp.dot(onehot, xb, preferred_element_type=F32)
            stg[e, r0:r1, d:d + LANES] = jnp.dot(onehot, gates, preferred_element_type=F32)

        @pl.when(rows > 0)
        def _():
            fill(0, few)

        @pl.when(rows > few)
        def _():
            fill(few, tile)

        dst0 = e * cap_pad + base

        def make_copy(off, size):
            return pltpu.make_async_copy(stg.at[e, pl.ds(off, size)],
                                         xe_ref.at[pl.ds(pl.multiple_of(dst0 + off, SUBLANES), size)], sems.at[e])

        _start_pieces(rows, tile, make_copy)

    @pl.when(i == n_tiles - 1)
    def _():
        for e in range(n_exp):
            _wait_rows(xe_ref, padded[e], sems.at[e])
        stg[0] = jnp.zeros((tile, stg.shape[2]), F32)
        for e in range(n_exp):
            used = base_ref[e, n_tiles]
            tail = cap_pad - used
            dst0 = e * cap_pad + used

            def zero_copy(off, size):
                return pltpu.make_async_copy(stg.at[0, pl.ds(0, size)],
                                             xe_ref.at[pl.ds(pl.multiple_of(dst0 + off, SUBLANES), size)], sems.at[0])

            def whole(j, carry):
                cp = zero_copy(j * tile, tile)
                cp.start()
                cp.wait()
                return carry

            n_whole = tail // tile
            lax.fori_loop(0, n_whole, whole, 0)
            rest = tail - n_whole * tile
            _start_pieces(rest, tile, lambda off, size: zero_copy(n_whole * tile + off, size))
            _wait_rows(xe_ref, rest, sems.at[0])


def _dispatch(x1, aff_t, pos, base_tbl, cnt_tbl, *, cap_pad, tile):
    n, d = x1.shape
    n_exp = pos.shape[0]
    assert 3 * n_exp <= LANES
    n_tiles = n // tile
    return pl.pallas_call(
        functools.partial(_dispatch_kernel, tile=tile, cap_pad=cap_pad, n_tiles=n_tiles),
        grid_spec=pltpu.PrefetchScalarGridSpec(
            num_scalar_prefetch=2,
            grid=(n_tiles,),
            in_specs=[pl.BlockSpec((tile, d), lambda i, b, c: (i, 0)),
                      pl.BlockSpec((tile, n_exp), lambda i, b, c: (i, 0)),
                      pl.BlockSpec((n_exp, tile), lambda i, b, c: (0, i))],
            out_specs=pl.BlockSpec(memory_space=pl.ANY),
            scratch_shapes=[pltpu.VMEM((n_exp, tile, d + LANES), F32), pltpu.SemaphoreType.DMA((n_exp,))],
        ),
        out_shape=jax.ShapeDtypeStruct((n_exp * cap_pad, d + LANES), F32),
        compiler_params=_params("arbitrary"),
        name="dispatch",
    )(base_tbl, cnt_tbl, x1, aff_t, pos)


def _ffn_kernel(base_ref, xe_ref, wg_ref, wu_ref, wd_ref, ye_ref, *, f_chunk, n_tiles):
    tm = xe_ref.shape[0]
    d = ye_ref.shape[1]
    n_exp = pl.num_programs(0)
    e = pl.program_id(0)
    used = base_ref[e, n_tiles]
    live = pl.program_id(1) * tm < used

    @pl.when(live)
    def _():
        xb = xe_ref[:, 0:d].astype(BF16)
        lane = lax.broadcasted_iota(I32, (tm, LANES), 1)
        mine = (lane % n_exp == e) & (lane < 3 * n_exp)
        gate = jnp.sum(jnp.where(mine, xe_ref[:, d:d + LANES], 0.0), axis=1, keepdims=True)
        d_ff = wg_ref.shape[2]
        acc = jnp.zeros(ye_ref.shape, F32)
        for c in range(d_ff // f_chunk):
            cols = slice(c * f_chunk, (c + 1) * f_chunk)
            g = jnp.dot(xb, wg_ref[0, :, cols], preferred_element_type=F32)
            u = jnp.dot(xb, wu_ref[0, :, cols], preferred_element_type=F32)
            h = (g * jax.nn.sigmoid(g)) * u
            acc = acc + jnp.dot(h.astype(BF16), wd_ref[0, cols, :], preferred_element_type=F32)
        ye_ref[...] = acc * gate

    @pl.when(jnp.logical_not(live))
    def _():
        ye_ref[...] = jnp.zeros(ye_ref.shape, F32)


def _ffn(xe, base_tbl, wg, wu, wd, *, cap_pad, tm, f_chunk, n_tiles):
    n_slots, d_ext = xe.shape
    n_exp, d, d_ff = wg.shape
    assert cap_pad % tm == 0 and d_ff % f_chunk == 0 and d_ext == d + LANES
    kblocks = cap_pad // tm
    return pl.pallas_call(
        functools.partial(_ffn_kernel, f_chunk=f_chunk, n_tiles=n_tiles),
        grid_spec=pltpu.PrefetchScalarGridSpec(
            num_scalar_prefetch=1,
            grid=(n_exp, kblocks),
            in_specs=[
                pl.BlockSpec((tm, d_ext), lambda e, k, b: (e * kblocks + k, 0)),
                pl.BlockSpec((1, d, d_ff), lambda e, k, b: (e, 0, 0)),
                pl.BlockSpec((1, d, d_ff), lambda e, k, b: (e, 0, 0)),
                pl.BlockSpec((1, d_ff, d), lambda e, k, b: (e, 0, 0)),
            ],
            out_specs=pl.BlockSpec((tm, d), lambda e, k, b: (e * kblocks + k, 0)),
        ),
        out_shape=jax.ShapeDtypeStruct((n_slots, d), F32),
        compiler_params=_params("parallel", "parallel"),
        name="expert_ffn",
    )(base_tbl, xe, wg, wu, wd)


def _combine_kernel(base_ref, cnt_ref, x_ref, pos_ref, g_ref, b_ref, ye_ref, o_ref,
                    stg, sems, *, tile, cap_pad, n_tiles, alpha):
    i = pl.program_id(0)
    n_exp = pos_ref.shape[1]
    half = stg.shape[0] // 2

    @pl.when(i == 0)
    def _():
        stg[...] = jnp.zeros(stg.shape, F32)

    def layout(j):
        segs, off = [], jnp.int32(0)
        for e in range(n_exp):
            rows = _pad_rows(cnt_ref[e, j])
            segs.append((base_ref[e, j], rows, off))
            off = off + rows
        return segs, off

    def first_row(j, total):
        return jnp.where(total > half, 0, (j % 2) * half)

    def fetch(j, segs, row0):
        for e, (base, rows, off) in enumerate(segs):
            src0 = e * cap_pad + base
            dst0 = row0 + off

            def make_copy(o, size):
                return pltpu.make_async_copy(ye_ref.at[pl.ds(pl.multiple_of(src0 + o, SUBLANES), size)],
                                             stg.at[pl.ds(pl.multiple_of(dst0 + o, SUBLANES), size)], sems.at[j % 2])

            _start_pieces(rows, tile, make_copy)

    nxt = jnp.minimum(i + 1, n_tiles - 1)
    segs, total = layout(i)
    segs_next, total_next = layout(nxt)
    _, total_prev = layout(jnp.maximum(i - 1, 0))
    row0 = first_row(i, total)
    small, small_next, small_prev = total <= half, total_next <= half, total_prev <= half

    @pl.when(jnp.logical_not((i > 0) & small_prev & small))
    def _():
        fetch(i, segs, row0)

    _wait_rows(ye_ref, total, sems.at[i % 2])

    @pl.when((i + 1 < n_tiles) & small & small_next)
    def _():
        fetch(nxt, segs_next, first_row(nxt, total_next))

    o_ref[...] = alpha * x_ref[...]
    slot_id = lax.broadcasted_iota(I32, (tile, tile), 1)

    def block(b, carry):
        chunk = stg[pl.ds(pl.multiple_of(row0 + b * tile, tile), tile), :].astype(BF16)
        hit = None
        for e, (base, rows, off) in enumerate(segs):
            match = slot_id == (pos_ref[:, e:e + 1] - (base - off + b * tile))
            hit = match if hit is None else hit | match
        onehot = jnp.where(hit, 1.0, 0.0).astype(BF16)
        o_ref[...] += jnp.dot(onehot, chunk, preferred_element_type=F32)
        return carry

    lax.fori_loop(0, (total + (tile - 1)) // tile, block, 0)
    o_ref[...] = _layer_norm(o_ref[...], g_ref[...], b_ref[...])


def _combine(x1, pos_t, base_tbl, cnt_tbl, ln_g, ln_b, ye, *, cap_pad, alpha, tile):
    n, d = x1.shape
    n_exp = pos_t.shape[1]
    row = lambda i, b, c: (i, 0)
    const = lambda i, b, c: (0, 0)
    stg_rows = n_exp * tile
    return pl.pallas_call(
        functools.partial(_combine_kernel, tile=tile, cap_pad=cap_pad, n_tiles=n // tile, alpha=alpha),
        grid_spec=pltpu.PrefetchScalarGridSpec(
            num_scalar_prefetch=2,
            grid=(n // tile,),
            in_specs=[
                pl.BlockSpec((tile, d), row), pl.BlockSpec((tile, n_exp), row),
                pl.BlockSpec((1, d), const), pl.BlockSpec((1, d), const),
                pl.BlockSpec(memory_space=pl.ANY),
            ],
            out_specs=pl.BlockSpec((tile, d), row),
            scratch_shapes=[pltpu.VMEM((stg_rows, d), F32), pltpu.SemaphoreType.DMA((2,))],
        ),
        out_shape=jax.ShapeDtypeStruct((n, d), F32),
        compiler_params=_params("arbitrary"),
        name="combine_ln",
    )(base_tbl, cnt_tbl, x1, pos_t, ln_g, ln_b, ye)


def _rope_tables(seq):
    t = jnp.arange(seq)
    row = (t // GRID_W).astype(F32)
    col = (t % GRID_W).astype(F32)
    half = HEAD_DIM // 2
    inv_freq = ROPE_THETA ** (-jnp.arange(0, half, 2, dtype=F32) / half)
    ang_r = row[:, None] * inv_freq[None, :]
    ang_c = col[:, None] * inv_freq[None, :]
    ang = jnp.concatenate([ang_r, ang_r, ang_c, ang_c], axis=-1)
    sign = jnp.where((jnp.arange(HEAD_DIM) % half) < half // 2, -1.0, 1.0).astype(F32)
    reps = LANES // HEAD_DIM
    return jnp.tile(jnp.cos(ang), (1, reps)), jnp.tile(jnp.sin(ang) * sign[None, :], (1, reps))


def _gqa_slot_columns():
    heads = [p + GQA_GROUP * half for p in range(HEAD_PAIRS) for half in range(2)]
    return np.concatenate([np.arange(HEAD_DIM) + HEAD_DIM * h for h in heads])


def _prep_layer(w_in, na_rpb, q_norm, k_norm, w_br_na, w_br_gqa, w_out, ln1_g, ln1_b,
                w_router, w_e_gate, w_e_up, w_e_down, ln2_g, ln2_b):
    d = w_in.shape[0]
    s_na = 3 * NA_WIDTH
    cols = _gqa_slot_columns()
    perm = np.concatenate([np.arange(s_na), s_na + cols, np.arange(s_na + GQA_WIDTH, w_in.shape[1])])
    reps = LANES // HEAD_DIM
    wr_t = w_router.T.astype(F32)
    wr_hi = wr_t.astype(BF16)
    gm = np.kron(np.eye(reps, dtype=np.float32), np.full((HEAD_DIM, HEAD_DIM), 1.0 / HEAD_DIM, np.float32))
    return dict(
        w_in=w_in[:, perm].astype(BF16),
        bias_tbl=_na_bias_table(na_rpb),
        qn=jnp.tile(q_norm.astype(F32), reps)[None, :], kn=jnp.tile(k_norm.astype(F32), reps)[None, :],
        gm=jnp.asarray(gm, BF16),
        wna=w_br_na.astype(BF16), wgq=w_br_gqa[cols].astype(BF16), wout=w_out.astype(BF16),
        ln1_g=ln1_g.astype(F32).reshape(1, d), ln1_b=ln1_b.astype(F32).reshape(1, d),
        wr_hi=wr_hi, wr_lo=(wr_t - wr_hi.astype(F32)).astype(BF16),
        wg=w_e_gate.astype(BF16), wu=w_e_up.astype(BF16), wd=w_e_down.astype(BF16),
        ln2_g=ln2_g.astype(F32).reshape(1, d), ln2_b=ln2_b.astype(F32).reshape(1, d),
    )


def _tiles(seq):
    return dict(tm_proj=512, tq=256, tk=1024, tm_merge=256, t_moe=256, tm_ffn=512, f_chunk=512)


def _trunk_layer(x2d, p, *, seq, alpha, rope):
    n, d = x2d.shape
    n_exp = p["wg"].shape[0]
    cap = EC_CAPACITY * n // n_exp
    tl = _tiles(seq)
    cos, sin = rope
    naq, nak, nav, gq, gk, gv, gates = _inproj(x2d, p["w_in"], cos, sin, p["qn"], p["kn"], p["gm"],
                                               seq=seq, tm=min(tl["tm_proj"], seq))
    na = _na_attention(naq, nak, nav, p["bias_tbl"], seq=seq)
    nb = n // seq
    ones_rows = jnp.zeros((nb, VT_ROWS - KV_WIDTH, seq), BF16).at[:, 0, :].set(1.0)
    gvt = jnp.concatenate([gv.reshape(nb, seq, KV_WIDTH).transpose(0, 2, 1), ones_rows], axis=1)
    gvt = gvt.reshape(nb * VT_ROWS, seq)
    ga = _gqa_attention(gq, gk, gvt, seq=seq, tq=min(tl["tq"], seq), tk=min(tl["tk"], seq // 2))
    x1, aff = _merge(na, ga, gates, x2d, p["wna"], p["wgq"], p["wout"], p["ln1_g"], p["ln1_b"],
                     p["wr_hi"], p["wr_lo"], alpha=alpha, tm=tl["tm_merge"])
    tile = tl["t_moe"]
    n_tiles = n // tile
    tm_ffn = min(tl["tm_ffn"], cap)
    cap_pad = pl.cdiv(cap + (SUBLANES - 1) * n_tiles, tm_ffn) * tm_ffn
    pos, base_tbl, cnt_tbl = _route(aff, cap=cap, tile=tile)
    xe = _dispatch(x1, aff.T, pos, base_tbl, cnt_tbl, cap_pad=cap_pad, tile=tile)
    ye = _ffn(xe, base_tbl, p["wg"], p["wu"], p["wd"], cap_pad=cap_pad, tm=tm_ffn, f_chunk=tl["f_chunk"],
              n_tiles=n_tiles)
    return _combine(x1, pos.T, base_tbl, cnt_tbl, p["ln2_g"], p["ln2_b"], ye,
                    cap_pad=cap_pad, alpha=alpha, tile=tile)


@jax.jit
def kernel(x_prompt, x_sample, w_in, na_rpb, q_norm, k_norm, w_br_na, w_br_gqa, w_out, ln1_g, ln1_b,
           w_router, w_e_gate, w_e_up, w_e_down, ln2_g, ln2_b):
    depth = w_in.shape[0]
    alpha = float((2 * depth) ** 0.25)
    d = x_prompt.shape[-1]
    groups = []
    for x in (x_prompt, x_sample):
        b, s, _ = x.shape
        groups.append(dict(x=x.reshape(b * s, d), shape=x.shape, seq=s, rope=_rope_tables(s)))
    for l in range(depth):
        p = _prep_layer(w_in[l], na_rpb[l], q_norm[l], k_norm[l], w_br_na[l], w_br_gqa[l], w_out[l],
                        ln1_g[l], ln1_b[l], w_router[l], w_e_gate[l], w_e_up[l], w_e_down[l],
                        ln2_g[l], ln2_b[l])
        for g in groups:
            g["x"] = _trunk_layer(g["x"], p, seq=g["seq"], alpha=alpha, rope=g["rope"])
    return tuple(g["x"].reshape(g["shape"]) for g in groups)
```

```python
import functools

import jax
import jax.numpy as jnp
import numpy as np
from jax import lax
from jax.experimental import pallas as pl
from jax.experimental.pallas import tpu as pltpu

F32 = jnp.float32
BF16 = jnp.bfloat16
I32 = jnp.int32

GRID_W = 64
HEAD_DIM = 64
NA_HEADS = 8
NA_WIN_H = 8
NA_WIN_W = 16
GQA_Q_HEADS = 8
GQA_KV_HEADS = 2
GQA_GROUP = GQA_Q_HEADS // GQA_KV_HEADS
ROPE_THETA = 10000.0
EC_CAPACITY = 2
LN_EPS = 1e-5
RMS_EPS = 1e-6

LANES = 128
SUBLANES = 8
NA_WIDTH = NA_HEADS * HEAD_DIM
GQA_WIDTH = GQA_Q_HEADS * HEAD_DIM
KV_WIDTH = GQA_KV_HEADS * HEAD_DIM
HEAD_PAIRS = GQA_WIDTH // LANES
BF16_TILE_ROWS = 2 * SUBLANES
VT_ROWS = KV_WIDTH + BF16_TILE_ROWS
UNPICKED = -(1 << 30)
DISPATCH_FEW_ROWS = 64
GQA_Q_SCALE = HEAD_DIM ** -0.5 * float(np.log2(np.e))
NEG_BIG = -1e30
VMEM_LIMIT_BYTES = 56 * 1024 * 1024

_NT = (((1,), (1,)), ((), ()))


def _params(*sem):
    return pltpu.CompilerParams(dimension_semantics=sem, vmem_limit_bytes=VMEM_LIMIT_BYTES)


def _inproj_kernel(x_ref, w_ref, cos_ref, sin_ref, qn_ref, kn_ref, gm_ref,
                   naq_ref, nak_ref, nav_ref, gq_ref, gk_ref, gv_ref, gate_ref, *, d_model):
    xb = x_ref[...].astype(BF16)
    s_na = 3 * NA_WIDTH
    s_gq = s_na + GQA_WIDTH
    s_gk = s_gq + KV_WIDTH
    s_gv = s_gk + KV_WIDTH

    def proj(c0, width):
        return jnp.dot(xb, w_ref[:, c0:c0 + width], preferred_element_type=F32)

    naq_ref[...] = (proj(0, NA_WIDTH) * HEAD_DIM ** -0.5).astype(BF16)
    nak_ref[...] = proj(NA_WIDTH, NA_WIDTH).astype(BF16)
    nav_ref[...] = proj(2 * NA_WIDTH, NA_WIDTH).astype(BF16)

    cos = cos_ref[...]
    sin = sin_ref[...]
    gm = gm_ref[...]
    lane = lax.broadcasted_iota(I32, cos.shape, 1)
    first_half = (lane % (HEAD_DIM // 2)) < (HEAD_DIM // 4)

    def norm_rope(a, gain):
        sq = a * a
        hi = sq.astype(BF16)
        lo = (sq - hi.astype(F32)).astype(BF16)
        ms = jnp.dot(hi, gm, preferred_element_type=F32) + jnp.dot(lo, gm, preferred_element_type=F32)
        an = a * lax.rsqrt(ms + RMS_EPS) * gain
        quarter = HEAD_DIM // 4
        rot = jnp.where(first_half, pltpu.roll(an, LANES - quarter, 1), pltpu.roll(an, quarter, 1))
        return an * cos + rot * sin

    qn = qn_ref[...]
    for p in range(HEAD_PAIRS):
        a = proj(s_na + p * LANES, LANES)
        gq_ref[:, p * LANES:(p + 1) * LANES] = (norm_rope(a, qn) * GQA_Q_SCALE).astype(BF16)
    gk_ref[...] = norm_rope(proj(s_gq, KV_WIDTH), kn_ref[...]).astype(BF16)
    gv_ref[...] = proj(s_gk, KV_WIDTH).astype(BF16)
    gate_chunk = 512
    for c in range(2 * d_model // gate_chunk):
        g = proj(s_gv + c * gate_chunk, gate_chunk)
        gate_ref[:, c * gate_chunk:(c + 1) * gate_chunk] = jax.nn.sigmoid(g)


def _inproj(x2d, w_in_b, cos, sin, qn, kn, gm, *, seq, tm):
    n, d = x2d.shape
    d_in = w_in_b.shape[1]
    assert n % tm == 0 and seq % tm == 0
    sblocks = seq // tm
    row = lambda i: (i, 0)
    const = lambda i: (0, 0)
    pos = lambda i: (i % sblocks, 0)
    out_shape = [
        jax.ShapeDtypeStruct((n, NA_WIDTH), BF16), jax.ShapeDtypeStruct((n, NA_WIDTH), BF16),
        jax.ShapeDtypeStruct((n, NA_WIDTH), BF16), jax.ShapeDtypeStruct((n, GQA_WIDTH), BF16),
        jax.ShapeDtypeStruct((n, KV_WIDTH), BF16), jax.ShapeDtypeStruct((n, KV_WIDTH), BF16),
        jax.ShapeDtypeStruct((n, 2 * d), F32),
    ]
    out_specs = [
        pl.BlockSpec((tm, NA_WIDTH), row), pl.BlockSpec((tm, NA_WIDTH), row), pl.BlockSpec((tm, NA_WIDTH), row),
        pl.BlockSpec((tm, GQA_WIDTH), row), pl.BlockSpec((tm, KV_WIDTH), row), pl.BlockSpec((tm, KV_WIDTH), row),
        pl.BlockSpec((tm, 2 * d), row),
    ]
    return pl.pallas_call(
        functools.partial(_inproj_kernel, d_model=d),
        grid=(n // tm,),
        in_specs=[
            pl.BlockSpec((tm, d), row), pl.BlockSpec((d, d_in), const),
            pl.BlockSpec((tm, LANES), pos), pl.BlockSpec((tm, LANES), pos),
            pl.BlockSpec((1, LANES), const), pl.BlockSpec((1, LANES), const),
            pl.BlockSpec((LANES, LANES), const),
        ],
        out_specs=out_specs,
        out_shape=out_shape,
        compiler_params=_params("parallel"),
        name="inproj",
    )(x2d, w_in_b, cos, sin, qn, kn, gm)


def _na_kernel(q_ref, kp_ref, kc_ref, kn_ref, vp_ref, vc_ref, vn_ref, bias_ref, o_ref,
               kbuf, vbuf, *, rows, nrb):
    blk = NA_WIN_H * GRID_W
    j = pl.program_id(0) % nrb
    kbuf[0:blk] = kp_ref[...]
    kbuf[blk:2 * blk] = kc_ref[...]
    kbuf[2 * blk:3 * blk] = kn_ref[...]
    vbuf[0:blk] = vp_ref[...]
    vbuf[blk:2 * blk] = vc_ref[...]
    vbuf[2 * blk:3 * blk] = vn_ref[...]
    lane = lax.broadcasted_iota(I32, (GRID_W, LANES), 1)
    low = lane < HEAD_DIM

    def row_body(i, carry):
        r = j * NA_WIN_H + i
        rs = jnp.clip(r - NA_WIN_H // 2, 0, rows - NA_WIN_H)
        d0 = rs - r + (NA_WIN_H - 1)
        off = pl.multiple_of((rs - (j - 1) * NA_WIN_H) * GRID_W, GRID_W)
        qoff = pl.multiple_of(i * GRID_W, GRID_W)
        for p in range(NA_WIDTH // LANES):
            cols = slice(p * LANES, (p + 1) * LANES)
            q2 = q_ref[pl.ds(qoff, GRID_W), cols]
            k2 = kbuf[pl.ds(off, blk), cols]
            v2 = vbuf[pl.ds(off, blk), cols]
            zero = jnp.zeros_like(q2)
            qs = jnp.concatenate([jnp.where(low, q2, zero), jnp.where(low, zero, q2)], axis=0)
            st = lax.dot_general(k2, qs, _NT, preferred_element_type=F32) + bias_ref[d0, p]
            e = jnp.exp(st - jnp.max(st, axis=0, keepdims=True))
            prob = e * (1.0 / jnp.sum(e, axis=0, keepdims=True))
            o = jnp.dot(prob.T.astype(BF16), v2, preferred_element_type=F32)
            o_ref[pl.ds(qoff, GRID_W), cols] = jnp.where(low, o[:GRID_W], o[GRID_W:]).astype(BF16)
        return carry

    lax.fori_loop(0, NA_WIN_H, row_body, 0)


def _na_bias_table(rpb):
    c = jnp.arange(GRID_W)
    cs = jnp.clip(c - NA_WIN_W // 2, 0, GRID_W - NA_WIN_W)
    cc = jnp.arange(GRID_W)
    inwin = (cc[None, :] >= cs[:, None]) & (cc[None, :] < cs[:, None] + NA_WIN_W)
    dc = jnp.clip(cc[None, :] - c[:, None] + (NA_WIN_W - 1), 0, 2 * NA_WIN_W - 2)
    full = jnp.where(inwin[None, None], rpb[:, :, dc].astype(F32), NEG_BIG)
    tbl = jnp.stack([full[:, d0:d0 + NA_WIN_H] for d0 in range(NA_WIN_H)])
    tbl = tbl.reshape(NA_WIN_H, NA_HEADS // 2, 2, NA_WIN_H, GRID_W, GRID_W)
    return tbl.transpose(0, 1, 3, 5, 2, 4).reshape(NA_WIN_H, NA_HEADS // 2, NA_WIN_H * GRID_W, 2 * GRID_W)


def _na_attention(q, k, v, bias_tbl, *, seq):
    n = q.shape[0]
    rows = seq // GRID_W
    assert rows % NA_WIN_H == 0 and rows >= 2 * NA_WIN_H
    nrb = rows // NA_WIN_H
    blk = NA_WIN_H * GRID_W

    def cur(g):
        return (g, 0)

    def prev(g):
        return (g - jnp.where(g % nrb == 0, 0, 1), 0)

    def nxt(g):
        return (g + jnp.where(g % nrb == nrb - 1, 0, 1), 0)

    spec = lambda f: pl.BlockSpec((blk, NA_WIDTH), f)
    return pl.pallas_call(
        functools.partial(_na_kernel, rows=rows, nrb=nrb),
        grid=(n // blk,),
        in_specs=[spec(cur), spec(prev), spec(cur), spec(nxt), spec(prev), spec(cur), spec(nxt),
                  pl.BlockSpec(bias_tbl.shape, lambda g: (0, 0, 0, 0))],
        out_specs=spec(cur),
        out_shape=jax.ShapeDtypeStruct((n, NA_WIDTH), BF16),
        scratch_shapes=[pltpu.VMEM((3 * blk, NA_WIDTH), BF16), pltpu.VMEM((3 * blk, NA_WIDTH), BF16)],
        compiler_params=_params("parallel"),
        name="na_attention",
    )(q, k, k, k, v, v, v, bias_tbl)


def _gqa_kernel(q_ref, k_ref, vt_ref, o_ref, qs_sc, sa_sc, sb_sc, m_sc, acc_sc, *, tq, tk, seq):
    lane = lax.broadcasted_iota(I32, (tq, LANES), 1)
    low = lane < HEAD_DIM
    for p in range(HEAD_PAIRS):
        q2 = q_ref[:, p * LANES:(p + 1) * LANES]
        zero = jnp.zeros_like(q2)
        qs_sc[(2 * p) * tq:(2 * p + 1) * tq, :] = jnp.where(low, q2, zero)
        qs_sc[(2 * p + 1) * tq:(2 * p + 2) * tq, :] = jnp.where(low, zero, q2)
    m_sc[...] = jnp.full(m_sc.shape, -jnp.inf, F32)
    acc_sc[...] = jnp.zeros(acc_sc.shape, F32)

    def scores(c, st_ref):
        koff = pl.multiple_of(c * tk, tk)
        st_ref[...] = lax.dot_general(k_ref[pl.ds(koff, tk), :], qs_sc[...], _NT, preferred_element_type=F32)

    def consume(c, st_ref):
        koff = pl.multiple_of(c * tk, tk)
        st = st_ref[...]
        m_old = m_sc[...]
        m_new = jnp.maximum(m_old, jnp.max(st, axis=0, keepdims=True))
        alpha = jnp.exp2(m_old - m_new)
        e = jnp.exp2(st - m_new).astype(BF16)
        pv = jnp.dot(vt_ref[:, pl.ds(koff, tk)], e, preferred_element_type=F32)
        acc_sc[...] = alpha * acc_sc[...] + pv
        m_sc[...] = m_new

    n_chunks = seq // tk
    assert n_chunks % 2 == 0
    scores(0, sa_sc)

    def body(j, carry):
        scores(2 * j + 1, sb_sc)
        consume(2 * j, sa_sc)
        scores(2 * j + 2, sa_sc)
        consume(2 * j + 1, sb_sc)
        return carry

    lax.fori_loop(0, n_chunks // 2 - 1, body, 0)
    scores(n_chunks - 1, sb_sc)
    consume(n_chunks - 2, sa_sc)
    consume(n_chunks - 1, sb_sc)
    o_t = acc_sc[0:KV_WIDTH, :] * (1.0 / acc_sc[KV_WIDTH:KV_WIDTH + 1, :])
    top = lax.broadcasted_iota(I32, (KV_WIDTH, tq), 0) < HEAD_DIM
    for p in range(HEAD_PAIRS):
        a = o_t[:, (2 * p) * tq:(2 * p + 1) * tq]
        b = o_t[:, (2 * p + 1) * tq:(2 * p + 2) * tq]
        o_ref[:, p * LANES:(p + 1) * LANES] = jnp.where(top, a, b).T.astype(BF16)


def _gqa_attention(q, k, vt, *, seq, tq, tk):
    n = q.shape[0]
    assert seq % tq == 0 and seq % tk == 0
    qblocks = seq // tq
    width = GQA_Q_HEADS * tq
    return pl.pallas_call(
        functools.partial(_gqa_kernel, tq=tq, tk=tk, seq=seq),
        grid=(n // seq, qblocks),
        in_specs=[
            pl.BlockSpec((tq, GQA_WIDTH), lambda b, i: (b * qblocks + i, 0)),
            pl.BlockSpec((seq, KV_WIDTH), lambda b, i: (b, 0)),
            pl.BlockSpec((VT_ROWS, seq), lambda b, i: (b, 0)),
        ],
        out_specs=pl.BlockSpec((tq, GQA_WIDTH), lambda b, i: (b * qblocks + i, 0)),
        out_shape=jax.ShapeDtypeStruct((n, GQA_WIDTH), BF16),
        scratch_shapes=[pltpu.VMEM((width, KV_WIDTH), BF16), pltpu.VMEM((tk, width), F32),
                        pltpu.VMEM((tk, width), F32), pltpu.VMEM((1, width), F32),
                        pltpu.VMEM((VT_ROWS, width), F32)],
        compiler_params=_params("parallel", "parallel"),
        name="gqa_attention",
    )(q, k, vt)


def _layer_norm(h, g, b):
    mu = jnp.mean(h, axis=-1, keepdims=True)
    hc = h - mu
    var = jnp.mean(hc * hc, axis=-1, keepdims=True)
    return hc * lax.rsqrt(var + LN_EPS) * g + b


def _merge_kernel(na_ref, gq_ref, gate_ref, x_ref, wna_ref, wgq_ref, wout_ref, g_ref, b_ref,
                  wrh_ref, wrl_ref, x1_ref, aff_ref, *, alpha, d_model):
    y_na = jnp.dot(na_ref[...], wna_ref[...], preferred_element_type=F32)
    y_gq = jnp.dot(gq_ref[...], wgq_ref[...], preferred_element_type=F32)
    mixin = gate_ref[:, :d_model] * y_na + gate_ref[:, d_model:] * y_gq
    mix = jnp.dot(mixin.astype(BF16), wout_ref[...], preferred_element_type=F32)
    x1 = _layer_norm(alpha * x_ref[...] + mix, g_ref[...], b_ref[...])
    x1_ref[...] = x1
    hi = x1.astype(BF16)
    lo = (x1 - hi.astype(F32)).astype(BF16)
    wh = wrh_ref[...]
    logits = (lax.dot_general(wh, hi, _NT, preferred_element_type=F32)
              + lax.dot_general(wh, lo, _NT, preferred_element_type=F32)
              + lax.dot_general(wrl_ref[...], hi, _NT, preferred_element_type=F32))
    m = jnp.max(logits, axis=0, keepdims=True)
    e = jnp.exp(logits - m)
    aff_ref[...] = e / jnp.sum(e, axis=0, keepdims=True)


def _merge(na, gq, gates, x2d, wna, wgq, wout, ln_g, ln_b, wr_hi, wr_lo, *, alpha, tm):
    n, d = x2d.shape
    n_exp = wr_hi.shape[0]
    row = lambda i: (i, 0)
    const = lambda i: (0, 0)
    return pl.pallas_call(
        functools.partial(_merge_kernel, alpha=alpha, d_model=d),
        grid=(n // tm,),
        in_specs=[
            pl.BlockSpec((tm, NA_WIDTH), row), pl.BlockSpec((tm, GQA_WIDTH), row),
            pl.BlockSpec((tm, 2 * d), row), pl.BlockSpec((tm, d), row),
            pl.BlockSpec((NA_WIDTH, d), const), pl.BlockSpec((GQA_WIDTH, d), const),
            pl.BlockSpec((d, d), const), pl.BlockSpec((1, d), const), pl.BlockSpec((1, d), const),
            pl.BlockSpec((n_exp, d), const), pl.BlockSpec((n_exp, d), const),
        ],
        out_specs=[pl.BlockSpec((tm, d), row), pl.BlockSpec((n_exp, tm), lambda i: (0, i))],
        out_shape=[jax.ShapeDtypeStruct((n, d), F32), jax.ShapeDtypeStruct((n_exp, n), F32)],
        compiler_params=_params("parallel"),
        name="merge_ln_router",
    )(na, gq, gates, x2d, wna, wgq, wout, ln_g, ln_b, wr_hi, wr_lo)


def _route_kernel(aff_ref, pos_ref, base_ref, cnt_ref, *, cap, tile, count_chunk):
    n_exp, n = aff_ref.shape
    n_tiles = n // tile
    capf = float(cap)

    def bits_at(off, width):
        return lax.bitcast_convert_type(aff_ref[:, pl.ds(off, width)], I32)

    def count_ge(cand):
        def inner(c, acc):
            b = bits_at(pl.multiple_of(c * count_chunk, count_chunk), count_chunk)
            return acc + jnp.where(b >= cand, 1.0, 0.0)
        acc = lax.fori_loop(0, n // count_chunk, inner, jnp.zeros((n_exp, count_chunk), F32))
        return jnp.sum(acc, axis=1, keepdims=True)

    def bisect(i, prefix):
        cand = prefix | jnp.left_shift(jnp.int32(1), 30 - i)
        return jnp.where(count_ge(cand) >= capf, cand, prefix)

    thr = lax.fori_loop(0, 31, bisect, jnp.zeros((n_exp, 1), I32))
    need = capf - count_ge(thr + 1)

    ri = lax.broadcasted_iota(I32, (tile, tile), 0)
    ci = lax.broadcasted_iota(I32, (tile, tile), 1)
    upper = jnp.where(ri < ci, 1.0, 0.0).astype(BF16)
    ones = jnp.ones((tile, tile), BF16)
    tbl_lane = lax.broadcasted_iota(I32, base_ref.shape, 1)

    base_ref[...] = jnp.zeros(base_ref.shape, I32)
    cnt_ref[...] = jnp.zeros(cnt_ref.shape, I32)

    def body(i, carry):
        ceq, base = carry
        off = pl.multiple_of(i * tile, tile)
        b = lax.bitcast_convert_type(aff_ref[:, pl.ds(off, tile)], I32)
        eq = b == thr
        eqb = jnp.where(eq, 1.0, 0.0).astype(BF16)
        eqrank = jnp.dot(eqb, upper, preferred_element_type=F32) + ceq
        sel = (b > thr) | (eq & (eqrank < need))
        selb = jnp.where(sel, 1.0, 0.0).astype(BF16)
        rank_in_tile = jnp.dot(selb, upper, preferred_element_type=F32)
        cnt = jnp.dot(selb, ones, preferred_element_type=F32)
        pos_ref[:, pl.ds(off, tile)] = jnp.where(sel, base + rank_in_tile, float(UNPICKED)).astype(I32)
        base_ref[...] = jnp.where(tbl_lane == i, base[:, :1].astype(I32), base_ref[...])
        cnt_ref[...] = jnp.where(tbl_lane == i, cnt[:, :1].astype(I32), cnt_ref[...])
        padded = jnp.floor((cnt + (SUBLANES - 1.0)) * (1.0 / SUBLANES)) * SUBLANES
        return ceq + jnp.dot(eqb, ones, preferred_element_type=F32), base + padded

    zero = jnp.zeros((n_exp, tile), F32)
    _, used = lax.fori_loop(0, n_tiles, body, (zero, zero))
    base_ref[...] = jnp.where(tbl_lane == n_tiles, used[:, :1].astype(I32), base_ref[...])


def _route(aff, *, cap, tile):
    n_exp, n = aff.shape
    count_chunk = min(2048, n)
    assert n % count_chunk == 0 and n % tile == 0
    tbl_w = pl.cdiv(n // tile + 1, LANES) * LANES
    full = lambda shape: pl.BlockSpec(shape, lambda i: (0, 0))
    return pl.pallas_call(
        functools.partial(_route_kernel, cap=cap, tile=tile, count_chunk=count_chunk),
        grid=(1,),
        in_specs=[full((n_exp, n))],
        out_specs=[full((n_exp, n)), full((n_exp, tbl_w)), full((n_exp, tbl_w))],
        out_shape=[jax.ShapeDtypeStruct((n_exp, n), I32), jax.ShapeDtypeStruct((n_exp, tbl_w), I32),
                   jax.ShapeDtypeStruct((n_exp, tbl_w), I32)],
        compiler_params=_params("arbitrary"),
        name="route",
    )(aff)


def _pad_rows(c):
    return ((c + (SUBLANES - 1)) >> 3) << 3


def _start_pieces(rows, max_rows, make_copy):
    k = 3
    assert SUBLANES == 1 << k
    while (1 << k) <= max_rows:
        size = 1 << k

        @pl.when(((rows >> k) & 1) == 1)
        def _():
            make_copy(pl.multiple_of((rows >> (k + 1)) << (k + 1), SUBLANES), size).start()

        k += 1


def _wait_rows(hbm_ref, rows, sem):
    @pl.when(rows > 0)
    def _():
        view = hbm_ref.at[pl.ds(0, pl.multiple_of(rows, SUBLANES))]
        pltpu.make_async_copy(view, view, sem).wait()


def _dispatch_kernel(base_ref, cnt_ref, x_ref, aff_ref, pos_ref, xe_ref, few_sc, long_sc, few_sems, long_sem,
                     *, tile, cap_pad, n_tiles):
    i = pl.program_id(0)
    n_exp = pos_ref.shape[0]
    d = x_ref.shape[1]
    xb = x_ref[...].astype(BF16)
    a = aff_ref[...]
    hi = a.astype(BF16)
    rest1 = a - hi.astype(F32)
    mid = rest1.astype(BF16)
    lo = (rest1 - mid.astype(F32)).astype(BF16)
    pr = lax.broadcasted_iota(I32, (n_exp, LANES), 0)
    pc = lax.broadcasted_iota(I32, (n_exp, LANES), 1)
    gates = sum(jnp.dot(piece, jnp.where(pc == k * n_exp + pr, 1.0, 0.0).astype(BF16), preferred_element_type=F32)
                for k, piece in enumerate((hi, mid, lo))).astype(BF16)
    few = DISPATCH_FEW_ROWS
    par = i % 2

    def few_rows_total(j):
        total = jnp.int32(0)
        for e in range(n_exp):
            r = _pad_rows(cnt_ref[e, j])
            total = total + jnp.where(r <= few, r, 0)
        return total

    @pl.when(i >= 2)
    def _():
        _wait_rows(xe_ref, few_rows_total(jnp.maximum(i - 2, 0)), few_sems.at[par])

    bases = [base_ref[e, i] for e in range(n_exp)]
    padded = [_pad_rows(cnt_ref[e, i]) for e in range(n_exp)]

    def compact(e, n_rows):
        row_id = lax.broadcasted_iota(I32, (n_rows, tile), 0)
        return jnp.where(row_id == pos_ref[e:e + 1, :] - bases[e], 1.0, 0.0).astype(BF16)

    onehot = jnp.concatenate([compact(e, few) for e in range(n_exp)], axis=0)
    few_sc[par, :, 0:d] = jnp.dot(onehot, xb, preferred_element_type=F32)
    few_sc[par, :, d:d + LANES] = jnp.dot(onehot, gates, preferred_element_type=F32)

    long_total = jnp.int32(0)
    for e in range(n_exp):
        rows = padded[e]
        dst0 = e * cap_pad + bases[e]
        is_long = rows > few

        def few_copy(off, size):
            return pltpu.make_async_copy(few_sc.at[par, pl.ds(pl.multiple_of(e * few + off, SUBLANES), size)],
                                         xe_ref.at[pl.ds(pl.multiple_of(dst0 + off, SUBLANES), size)],
                                         few_sems.at[par])

        _start_pieces(jnp.where(is_long, 0, rows), few, few_copy)

        @pl.when(is_long)
        def _():
            full = compact(e, tile)
            long_sc[e, :, 0:d] = jnp.dot(full, xb, preferred_element_type=F32)
            long_sc[e, :, d:d + LANES] = jnp.dot(full, gates, preferred_element_type=F32)

            def long_copy(off, size):
                return pltpu.make_async_copy(long_sc.at[e, pl.ds(off, size)],
                                             xe_ref.at[pl.ds(pl.multiple_of(dst0 + off, SUBLANES), size)], long_sem)

            _start_pieces(rows, tile, long_copy)

        long_total = long_total + jnp.where(is_long, rows, 0)
    _wait_rows(xe_ref, long_total, long_sem)

    @pl.when(i == n_tiles - 1)
    def _():
        _wait_rows(xe_ref, few_rows_total(i), few_sems.at[par])

        @pl.when(i >= 1)
        def _():
            _wait_rows(xe_ref, few_rows_total(jnp.maximum(i - 1, 0)), few_sems.at[1 - par])

        stg = long_sc
        sems = few_sems
        stg[0] = jnp.zeros((tile, stg.shape[2]), F32)
        for e in range(n_exp):
            used = base_ref[e, n_tiles]
            tail = cap_pad - used
            dst0 = e * cap_pad + used

            def zero_copy(off, size):
                return pltpu.make_async_copy(stg.at[0, pl.ds(0, size)],
                                             xe_ref.at[pl.ds(pl.multiple_of(dst0 + off, SUBLANES), size)], sems.at[0])

            def whole(j, carry):
                cp = zero_copy(j * tile, tile)
                cp.start()
                cp.wait()
                return carry

            n_whole = tail // tile
            lax.fori_loop(0, n_whole, whole, 0)
            rest = tail - n_whole * tile
            _start_pieces(rest, tile, lambda off, size: zero_copy(n_whole * tile + off, size))
            _wait_rows(xe_ref, rest, sems.at[0])


def _dispatch(x1, aff_t, pos, base_tbl, cnt_tbl, *, cap_pad, tile):
    n, d = x1.shape
    n_exp = pos.shape[0]
    assert 3 * n_exp <= LANES
    n_tiles = n // tile
    return pl.pallas_call(
        functools.partial(_dispatch_kernel, tile=tile, cap_pad=cap_pad, n_tiles=n_tiles),
        grid_spec=pltpu.PrefetchScalarGridSpec(
            num_scalar_prefetch=2,
            grid=(n_tiles,),
            in_specs=[pl.BlockSpec((tile, d), lambda i, b, c: (i, 0)),
                      pl.BlockSpec((tile, n_exp), lambda i, b, c: (i, 0)),
                      pl.BlockSpec((n_exp, tile), lambda i, b, c: (0, i))],
            out_specs=pl.BlockSpec(memory_space=pl.ANY),
            scratch_shapes=[pltpu.VMEM((2, n_exp * DISPATCH_FEW_ROWS, d + LANES), F32),
                            pltpu.VMEM((n_exp, tile, d + LANES), F32),
                            pltpu.SemaphoreType.DMA((2,)), pltpu.SemaphoreType.DMA(())],
        ),
        out_shape=jax.ShapeDtypeStruct((n_exp * cap_pad, d + LANES), F32),
        compiler_params=_params("arbitrary"),
        name="dispatch",
    )(base_tbl, cnt_tbl, x1, aff_t, pos)


def _ffn_kernel(base_ref, xe_ref, wg_ref, wu_ref, wd_ref, ye_ref, *, f_chunk, n_tiles):
    tm = xe_ref.shape[0]
    d = ye_ref.shape[1]
    n_exp = pl.num_programs(0)
    e = pl.program_id(0)
    used = base_ref[e, n_tiles]
    live = pl.program_id(1) * tm < used

    @pl.when(live)
    def _():
        xb = xe_ref[:, 0:d].astype(BF16)
        lane = lax.broadcasted_iota(I32, (tm, LANES), 1)
        mine = (lane % n_exp == e) & (lane < 3 * n_exp)
        gate = jnp.sum(jnp.where(mine, xe_ref[:, d:d + LANES], 0.0), axis=1, keepdims=True)
        d_ff = wg_ref.shape[2]
        acc = jnp.zeros(ye_ref.shape, F32)
        for c in range(d_ff // f_chunk):
            cols = slice(c * f_chunk, (c + 1) * f_chunk)
            g = jnp.dot(xb, wg_ref[0, :, cols], preferred_element_type=F32)
            u = jnp.dot(xb, wu_ref[0, :, cols], preferred_element_type=F32)
            h = (g * jax.nn.sigmoid(g)) * u
            acc = acc + jnp.dot(h.astype(BF16), wd_ref[0, cols, :], preferred_element_type=F32)
        ye_ref[...] = acc * gate

    @pl.when(jnp.logical_not(live))
    def _():
        ye_ref[...] = jnp.zeros(ye_ref.shape, F32)


def _ffn(xe, base_tbl, wg, wu, wd, *, cap_pad, tm, f_chunk, n_tiles):
    n_slots, d_ext = xe.shape
    n_exp, d, d_ff = wg.shape
    assert cap_pad % tm == 0 and d_ff % f_chunk == 0 and d_ext == d + LANES
    kblocks = cap_pad // tm
    return pl.pallas_call(
        functools.partial(_ffn_kernel, f_chunk=f_chunk, n_tiles=n_tiles),
        grid_spec=pltpu.PrefetchScalarGridSpec(
            num_scalar_prefetch=1,
            grid=(n_exp, kblocks),
            in_specs=[
                pl.BlockSpec((tm, d_ext), lambda e, k, b: (e * kblocks + k, 0)),
                pl.BlockSpec((1, d, d_ff), lambda e, k, b: (e, 0, 0)),
                pl.BlockSpec((1, d, d_ff), lambda e, k, b: (e, 0, 0)),
                pl.BlockSpec((1, d_ff, d), lambda e, k, b: (e, 0, 0)),
            ],
            out_specs=pl.BlockSpec((tm, d), lambda e, k, b: (e * kblocks + k, 0)),
        ),
        out_shape=jax.ShapeDtypeStruct((n_slots, d), F32),
        compiler_params=_params("parallel", "parallel"),
        name="expert_ffn",
    )(base_tbl, xe, wg, wu, wd)


def _combine_kernel(base_ref, cnt_ref, x_ref, pos_ref, g_ref, b_ref, ye_ref, o_ref,
                    stg, hit_sc, sems, *, tile, cap_pad, n_tiles, alpha):
    i = pl.program_id(0)
    n_exp = pos_ref.shape[1]
    half = stg.shape[0] // 2

    @pl.when(i == 0)
    def _():
        stg[...] = jnp.zeros(stg.shape, F32)

    def layout(j):
        segs, off = [], jnp.int32(0)
        for e in range(n_exp):
            rows = _pad_rows(cnt_ref[e, j])
            segs.append((base_ref[e, j], rows, off))
            off = off + rows
        return segs, off

    def first_row(j, total):
        return jnp.where(total > half, 0, (j % 2) * half)

    def fetch(j, segs, row0):
        for e, (base, rows, off) in enumerate(segs):
            src0 = e * cap_pad + base
            dst0 = row0 + off

            def make_copy(o, size):
                return pltpu.make_async_copy(ye_ref.at[pl.ds(pl.multiple_of(src0 + o, SUBLANES), size)],
                                             stg.at[pl.ds(pl.multiple_of(dst0 + o, SUBLANES), size)], sems.at[j % 2])

            _start_pieces(rows, tile, make_copy)

    nxt = jnp.minimum(i + 1, n_tiles - 1)
    segs, total = layout(i)
    segs_next, total_next = layout(nxt)
    _, total_prev = layout(jnp.maximum(i - 1, 0))
    row0 = first_row(i, total)
    small, small_next, small_prev = total <= half, total_next <= half, total_prev <= half

    @pl.when(jnp.logical_not((i > 0) & small_prev & small))
    def _():
        fetch(i, segs, row0)

    _wait_rows(ye_ref, total, sems.at[i % 2])

    @pl.when((i + 1 < n_tiles) & small & small_next)
    def _():
        fetch(nxt, segs_next, first_row(nxt, total_next))

    o_ref[...] = alpha * x_ref[...]
    slot_id = lax.broadcasted_iota(I32, (tile, tile), 1)

    def block(b, carry):
        chunk = stg[pl.ds(pl.multiple_of(row0 + b * tile, tile), tile), :].astype(BF16)
        hit_sc[...] = jnp.zeros(hit_sc.shape, F32)
        for e, (base, rows, off) in enumerate(segs):
            @pl.when((off < (b + 1) * tile) & (off + rows > b * tile))
            def _():
                match = slot_id == (pos_ref[:, e:e + 1] - (base - off + b * tile))
                hit_sc[...] = jnp.where(match, 1.0, hit_sc[...])

        o_ref[...] += jnp.dot(hit_sc[...].astype(BF16), chunk, preferred_element_type=F32)
        return carry

    lax.fori_loop(0, (total + (tile - 1)) // tile, block, 0)
    o_ref[...] = _layer_norm(o_ref[...], g_ref[...], b_ref[...])


def _combine(x1, pos_t, base_tbl, cnt_tbl, ln_g, ln_b, ye, *, cap_pad, alpha, tile):
    n, d = x1.shape
    n_exp = pos_t.shape[1]
    row = lambda i, b, c: (i, 0)
    const = lambda i, b, c: (0, 0)
    stg_rows = n_exp * tile
    return pl.pallas_call(
        functools.partial(_combine_kernel, tile=tile, cap_pad=cap_pad, n_tiles=n // tile, alpha=alpha),
        grid_spec=pltpu.PrefetchScalarGridSpec(
            num_scalar_prefetch=2,
            grid=(n // tile,),
            in_specs=[
                pl.BlockSpec((tile, d), row), pl.BlockSpec((tile, n_exp), row),
                pl.BlockSpec((1, d), const), pl.BlockSpec((1, d), const),
                pl.BlockSpec(memory_space=pl.ANY),
            ],
            out_specs=pl.BlockSpec((tile, d), row),
            scratch_shapes=[pltpu.VMEM((stg_rows, d), F32), pltpu.VMEM((tile, tile), F32),
                            pltpu.SemaphoreType.DMA((2,))],
        ),
        out_shape=jax.ShapeDtypeStruct((n, d), F32),
        compiler_params=_params("arbitrary"),
        name="combine_ln",
    )(base_tbl, cnt_tbl, x1, pos_t, ln_g, ln_b, ye)


def _rope_tables(seq):
    t = jnp.arange(seq)
    row = (t // GRID_W).astype(F32)
    col = (t % GRID_W).astype(F32)
    half = HEAD_DIM // 2
    inv_freq = ROPE_THETA ** (-jnp.arange(0, half, 2, dtype=F32) / half)
    ang_r = row[:, None] * inv_freq[None, :]
    ang_c = col[:, None] * inv_freq[None, :]
    ang = jnp.concatenate([ang_r, ang_r, ang_c, ang_c], axis=-1)
    sign = jnp.where((jnp.arange(HEAD_DIM) % half) < half // 2, -1.0, 1.0).astype(F32)
    reps = LANES // HEAD_DIM
    return jnp.tile(jnp.cos(ang), (1, reps)), jnp.tile(jnp.sin(ang) * sign[None, :], (1, reps))


def _gqa_slot_columns():
    heads = [p + GQA_GROUP * half for p in range(HEAD_PAIRS) for half in range(2)]
    return np.concatenate([np.arange(HEAD_DIM) + HEAD_DIM * h for h in heads])


def _prep_layer(w_in, na_rpb, q_norm, k_norm, w_br_na, w_br_gqa, w_out, ln1_g, ln1_b,
                w_router, w_e_gate, w_e_up, w_e_down, ln2_g, ln2_b):
    d = w_in.shape[0]
    s_na = 3 * NA_WIDTH
    cols = _gqa_slot_columns()
    perm = np.concatenate([np.arange(s_na), s_na + cols, np.arange(s_na + GQA_WIDTH, w_in.shape[1])])
    reps = LANES // HEAD_DIM
    wr_t = w_router.T.astype(F32)
    wr_hi = wr_t.astype(BF16)
    gm = np.kron(np.eye(reps, dtype=np.float32), np.full((HEAD_DIM, HEAD_DIM), 1.0 / HEAD_DIM, np.float32))
    return dict(
        w_in=w_in[:, perm].astype(BF16),
        bias_tbl=_na_bias_table(na_rpb),
        qn=jnp.tile(q_norm.astype(F32), reps)[None, :], kn=jnp.tile(k_norm.astype(F32), reps)[None, :],
        gm=jnp.asarray(gm, BF16),
        wna=w_br_na.astype(BF16), wgq=w_br_gqa[cols].astype(BF16), wout=w_out.astype(BF16),
        ln1_g=ln1_g.astype(F32).reshape(1, d), ln1_b=ln1_b.astype(F32).reshape(1, d),
        wr_hi=wr_hi, wr_lo=(wr_t - wr_hi.astype(F32)).astype(BF16),
        wg=w_e_gate.astype(BF16), wu=w_e_up.astype(BF16), wd=w_e_down.astype(BF16),
        ln2_g=ln2_g.astype(F32).reshape(1, d), ln2_b=ln2_b.astype(F32).reshape(1, d),
    )


def _tiles(seq):
    return dict(tm_proj=512, tq=256, tk=1024, tm_merge=256, t_moe=256, tm_ffn=512, f_chunk=512)


def _trunk_layer(x2d, p, *, seq, alpha, rope):
    n, d = x2d.shape
    n_exp = p["wg"].shape[0]
    cap = EC_CAPACITY * n // n_exp
    tl = _tiles(seq)
    cos, sin = rope
    naq, nak, nav, gq, gk, gv, gates = _inproj(x2d, p["w_in"], cos, sin, p["qn"], p["kn"], p["gm"],
                                               seq=seq, tm=min(tl["tm_proj"], seq))
    na = _na_attention(naq, nak, nav, p["bias_tbl"], seq=seq)
    nb = n // seq
    ones_rows = jnp.zeros((nb, VT_ROWS - KV_WIDTH, seq), BF16).at[:, 0, :].set(1.0)
    gvt = jnp.concatenate([gv.reshape(nb, seq, KV_WIDTH).transpose(0, 2, 1), ones_rows], axis=1)
    gvt = gvt.reshape(nb * VT_ROWS, seq)
    ga = _gqa_attention(gq, gk, gvt, seq=seq, tq=min(tl["tq"], seq), tk=min(tl["tk"], seq // 2))
    x1, aff = _merge(na, ga, gates, x2d, p["wna"], p["wgq"], p["wout"], p["ln1_g"], p["ln1_b"],
                     p["wr_hi"], p["wr_lo"], alpha=alpha, tm=tl["tm_merge"])
    tile = tl["t_moe"]
    n_tiles = n // tile
    tm_ffn = min(tl["tm_ffn"], cap)
    cap_pad = pl.cdiv(cap + (SUBLANES - 1) * n_tiles, tm_ffn) * tm_ffn
    pos, base_tbl, cnt_tbl = _route(aff, cap=cap, tile=tile)
    xe = _dispatch(x1, aff.T, pos, base_tbl, cnt_tbl, cap_pad=cap_pad, tile=tile)
    ye = _ffn(xe, base_tbl, p["wg"], p["wu"], p["wd"], cap_pad=cap_pad, tm=tm_ffn, f_chunk=tl["f_chunk"],
              n_tiles=n_tiles)
    return _combine(x1, pos.T, base_tbl, cnt_tbl, p["ln2_g"], p["ln2_b"], ye,
                    cap_pad=cap_pad, alpha=alpha, tile=tile)


@jax.jit
def kernel(x_prompt, x_sample, w_in, na_rpb, q_norm, k_norm, w_br_na, w_br_gqa, w_out, ln1_g, ln1_b,
           w_router, w_e_gate, w_e_up, w_e_down, ln2_g, ln2_b):
    depth = w_in.shape[0]
    alpha = float((2 * depth) ** 0.25)
    d = x_prompt.shape[-1]
    groups = []
    for x in (x_prompt, x_sample):
        b, s, _ = x.shape
        groups.append(dict(x=x.reshape(b * s, d), shape=x.shape, seq=s, rope=_rope_tables(s)))
    for l in range(depth):
        p = _prep_layer(w_in[l], na_rpb[l], q_norm[l], k_norm[l], w_br_na[l], w_br_gqa[l], w_out[l],
                        ln1_g[l], ln1_b[l], w_router[l], w_e_gate[l], w_e_up[l], w_e_down[l],
                        ln2_g[l], ln2_b[l])
        for g in groups:
            g["x"] = _trunk_layer(g["x"], p, seq=g["seq"], alpha=alpha, rope=g["rope"])
    return tuple(g["x"].reshape(g["shape"]) for g in groups)
```

```python
import functools

import jax
import jax.numpy as jnp
import numpy as np
from jax import lax
from jax.experimental import pallas as pl
from jax.experimental.pallas import tpu as pltpu

F32 = jnp.float32
BF16 = jnp.bfloat16
I32 = jnp.int32

GRID_W = 64
HEAD_DIM = 64
NA_HEADS = 8
NA_WIN_H = 8
NA_WIN_W = 16
GQA_Q_HEADS = 8
GQA_KV_HEADS = 2
GQA_GROUP = GQA_Q_HEADS // GQA_KV_HEADS
ROPE_THETA = 10000.0
EC_CAPACITY = 2
LN_EPS = 1e-5
RMS_EPS = 1e-6

LANES = 128
SUBLANES = 8
NA_WIDTH = NA_HEADS * HEAD_DIM
GQA_WIDTH = GQA_Q_HEADS * HEAD_DIM
KV_WIDTH = GQA_KV_HEADS * HEAD_DIM
HEAD_PAIRS = GQA_WIDTH // LANES
BF16_TILE_ROWS = 2 * SUBLANES
VT_ROWS = KV_WIDTH + BF16_TILE_ROWS
UNPICKED = -(1 << 30)
DISPATCH_FEW_ROWS = 64
GQA_Q_SCALE = HEAD_DIM ** -0.5 * float(np.log2(np.e))
NEG_BIG = -1e30
VMEM_LIMIT_BYTES = 56 * 1024 * 1024

_NT = (((1,), (1,)), ((), ()))


def _params(*sem):
    return pltpu.CompilerParams(dimension_semantics=sem, vmem_limit_bytes=VMEM_LIMIT_BYTES)


def _inproj_kernel(x_ref, w_ref, cos_ref, sin_ref, qn_ref, kn_ref, gm_ref,
                   naq_ref, nak_ref, nav_ref, gq_ref, gk_ref, gv_ref, gate_ref, *, d_model):
    xb = x_ref[...].astype(BF16)
    s_na = 3 * NA_WIDTH
    s_gq = s_na + GQA_WIDTH
    s_gk = s_gq + KV_WIDTH
    s_gv = s_gk + KV_WIDTH

    def proj(c0, width):
        return jnp.dot(xb, w_ref[:, c0:c0 + width], preferred_element_type=F32)

    naq_ref[...] = (proj(0, NA_WIDTH) * HEAD_DIM ** -0.5).astype(BF16)
    nak_ref[...] = proj(NA_WIDTH, NA_WIDTH).astype(BF16)
    nav_ref[...] = proj(2 * NA_WIDTH, NA_WIDTH).astype(BF16)

    cos = cos_ref[...]
    sin = sin_ref[...]
    gm = gm_ref[...]
    lane = lax.broadcasted_iota(I32, cos.shape, 1)
    first_half = (lane % (HEAD_DIM // 2)) < (HEAD_DIM // 4)

    def norm_rope(a, gain):
        sq = a * a
        hi = sq.astype(BF16)
        lo = (sq - hi.astype(F32)).astype(BF16)
        ms = jnp.dot(hi, gm, preferred_element_type=F32) + jnp.dot(lo, gm, preferred_element_type=F32)
        an = a * lax.rsqrt(ms + RMS_EPS) * gain
        quarter = HEAD_DIM // 4
        rot = jnp.where(first_half, pltpu.roll(an, LANES - quarter, 1), pltpu.roll(an, quarter, 1))
        return an * cos + rot * sin

    qn = qn_ref[...]
    for p in range(HEAD_PAIRS):
        a = proj(s_na + p * LANES, LANES)
        gq_ref[:, p * LANES:(p + 1) * LANES] = (norm_rope(a, qn) * GQA_Q_SCALE).astype(BF16)
    gk_ref[...] = norm_rope(proj(s_gq, KV_WIDTH), kn_ref[...]).astype(BF16)
    gv_ref[...] = proj(s_gk, KV_WIDTH).astype(BF16)
    gate_chunk = 512
    for c in range(2 * d_model // gate_chunk):
        g = proj(s_gv + c * gate_chunk, gate_chunk)
        gate_ref[:, c * gate_chunk:(c + 1) * gate_chunk] = jax.nn.sigmoid(g)


def _inproj(x2d, w_in_b, cos, sin, qn, kn, gm, *, seq, tm):
    n, d = x2d.shape
    d_in = w_in_b.shape[1]
    assert n % tm == 0 and seq % tm == 0
    sblocks = seq // tm
    row = lambda i: (i, 0)
    const = lambda i: (0, 0)
    pos = lambda i: (i % sblocks, 0)
    out_shape = [
        jax.ShapeDtypeStruct((n, NA_WIDTH), BF16), jax.ShapeDtypeStruct((n, NA_WIDTH), BF16),
        jax.ShapeDtypeStruct((n, NA_WIDTH), BF16), jax.ShapeDtypeStruct((n, GQA_WIDTH), BF16),
        jax.ShapeDtypeStruct((n, KV_WIDTH), BF16), jax.ShapeDtypeStruct((n, KV_WIDTH), BF16),
        jax.ShapeDtypeStruct((n, 2 * d), F32),
    ]
    out_specs = [
        pl.BlockSpec((tm, NA_WIDTH), row), pl.BlockSpec((tm, NA_WIDTH), row), pl.BlockSpec((tm, NA_WIDTH), row),
        pl.BlockSpec((tm, GQA_WIDTH), row), pl.BlockSpec((tm, KV_WIDTH), row), pl.BlockSpec((tm, KV_WIDTH), row),
        pl.BlockSpec((tm, 2 * d), row),
    ]
    return pl.pallas_call(
        functools.partial(_inproj_kernel, d_model=d),
        grid=(n // tm,),
        in_specs=[
            pl.BlockSpec((tm, d), row), pl.BlockSpec((d, d_in), const),
            pl.BlockSpec((tm, LANES), pos), pl.BlockSpec((tm, LANES), pos),
            pl.BlockSpec((1, LANES), const), pl.BlockSpec((1, LANES), const),
            pl.BlockSpec((LANES, LANES), const),
        ],
        out_specs=out_specs,
        out_shape=out_shape,
        compiler_params=_params("parallel"),
        name="inproj",
    )(x2d, w_in_b, cos, sin, qn, kn, gm)


def _na_kernel(q_ref, kp_ref, kc_ref, kn_ref, vp_ref, vc_ref, vn_ref, bias_ref, o_ref,
               kbuf, vbuf, *, rows, nrb):
    blk = NA_WIN_H * GRID_W
    j = pl.program_id(0) % nrb
    kbuf[0:blk] = kp_ref[...]
    kbuf[blk:2 * blk] = kc_ref[...]
    kbuf[2 * blk:3 * blk] = kn_ref[...]
    vbuf[0:blk] = vp_ref[...]
    vbuf[blk:2 * blk] = vc_ref[...]
    vbuf[2 * blk:3 * blk] = vn_ref[...]
    lane = lax.broadcasted_iota(I32, (GRID_W, LANES), 1)
    low = lane < HEAD_DIM

    def row_body(i, carry):
        r = j * NA_WIN_H + i
        rs = jnp.clip(r - NA_WIN_H // 2, 0, rows - NA_WIN_H)
        d0 = rs - r + (NA_WIN_H - 1)
        off = pl.multiple_of((rs - (j - 1) * NA_WIN_H) * GRID_W, GRID_W)
        qoff = pl.multiple_of(i * GRID_W, GRID_W)
        pairs = range(NA_WIDTH // LANES)
        scores = []
        for p in pairs:
            cols = slice(p * LANES, (p + 1) * LANES)
            q2 = q_ref[pl.ds(qoff, GRID_W), cols]
            zero = jnp.zeros_like(q2)
            qs = jnp.concatenate([jnp.where(low, q2, zero), jnp.where(low, zero, q2)], axis=0)
            k2 = kbuf[pl.ds(off, blk), cols]
            scores.append(lax.dot_general(k2, qs, _NT, preferred_element_type=F32) + bias_ref[d0, p])
        probs = []
        for st in scores:
            e = jnp.exp(st - jnp.max(st, axis=0, keepdims=True))
            probs.append((e * (1.0 / jnp.sum(e, axis=0, keepdims=True))).T.astype(BF16))
        for p in pairs:
            cols = slice(p * LANES, (p + 1) * LANES)
            o = jnp.dot(probs[p], vbuf[pl.ds(off, blk), cols], preferred_element_type=F32)
            o_ref[pl.ds(qoff, GRID_W), cols] = jnp.where(low, o[:GRID_W], o[GRID_W:]).astype(BF16)
        return carry

    lax.fori_loop(0, NA_WIN_H, row_body, 0)


def _na_bias_table(rpb):
    c = jnp.arange(GRID_W)
    cs = jnp.clip(c - NA_WIN_W // 2, 0, GRID_W - NA_WIN_W)
    cc = jnp.arange(GRID_W)
    inwin = (cc[None, :] >= cs[:, None]) & (cc[None, :] < cs[:, None] + NA_WIN_W)
    dc = jnp.clip(cc[None, :] - c[:, None] + (NA_WIN_W - 1), 0, 2 * NA_WIN_W - 2)
    full = jnp.where(inwin[None, None], rpb[:, :, dc].astype(F32), NEG_BIG)
    tbl = jnp.stack([full[:, d0:d0 + NA_WIN_H] for d0 in range(NA_WIN_H)])
    tbl = tbl.reshape(NA_WIN_H, NA_HEADS // 2, 2, NA_WIN_H, GRID_W, GRID_W)
    return tbl.transpose(0, 1, 3, 5, 2, 4).reshape(NA_WIN_H, NA_HEADS // 2, NA_WIN_H * GRID_W, 2 * GRID_W)


def _na_attention(q, k, v, bias_tbl, *, seq):
    n = q.shape[0]
    rows = seq // GRID_W
    assert rows % NA_WIN_H == 0 and rows >= 2 * NA_WIN_H
    nrb = rows // NA_WIN_H
    blk = NA_WIN_H * GRID_W

    def cur(g):
        return (g, 0)

    def prev(g):
        return (g - jnp.where(g % nrb == 0, 0, 1), 0)

    def nxt(g):
        return (g + jnp.where(g % nrb == nrb - 1, 0, 1), 0)

    spec = lambda f: pl.BlockSpec((blk, NA_WIDTH), f)
    return pl.pallas_call(
        functools.partial(_na_kernel, rows=rows, nrb=nrb),
        grid=(n // blk,),
        in_specs=[spec(cur), spec(prev), spec(cur), spec(nxt), spec(prev), spec(cur), spec(nxt),
                  pl.BlockSpec(bias_tbl.shape, lambda g: (0, 0, 0, 0))],
        out_specs=spec(cur),
        out_shape=jax.ShapeDtypeStruct((n, NA_WIDTH), BF16),
        scratch_shapes=[pltpu.VMEM((3 * blk, NA_WIDTH), BF16), pltpu.VMEM((3 * blk, NA_WIDTH), BF16)],
        compiler_params=_params("parallel"),
        name="na_attention",
    )(q, k, k, k, v, v, v, bias_tbl)


def _gqa_kernel(q_ref, k_ref, vt_ref, o_ref, qs_sc, sa_sc, sb_sc, m_sc, acc_sc, *, tq, tk, seq):
    q_groups = qs_sc.shape[0]
    lane = lax.broadcasted_iota(I32, (tq, LANES), 1)
    low = lane < HEAD_DIM
    for g in range(q_groups):
        for p in range(HEAD_PAIRS):
            q2 = q_ref[g * tq:(g + 1) * tq, p * LANES:(p + 1) * LANES]
            zero = jnp.zeros_like(q2)
            qs_sc[g, (2 * p) * tq:(2 * p + 1) * tq, :] = jnp.where(low, q2, zero)
            qs_sc[g, (2 * p + 1) * tq:(2 * p + 2) * tq, :] = jnp.where(low, zero, q2)
    m_sc[...] = jnp.full(m_sc.shape, -jnp.inf, F32)
    acc_sc[...] = jnp.zeros(acc_sc.shape, F32)
    n_chunks = seq // tk
    n_items = q_groups * n_chunks
    assert n_chunks % 2 == 0

    def scores(w, st_ref):
        koff = pl.multiple_of((w % n_chunks) * tk, tk)
        st_ref[...] = lax.dot_general(k_ref[pl.ds(koff, tk), :], qs_sc[w // n_chunks], _NT,
                                      preferred_element_type=F32)

    def consume(w, st_ref):
        g = w // n_chunks
        koff = pl.multiple_of((w % n_chunks) * tk, tk)
        st = st_ref[...]
        m_old = m_sc[g]
        m_new = jnp.maximum(m_old, jnp.max(st, axis=0, keepdims=True))
        alpha = jnp.exp2(m_old - m_new)
        e = jnp.exp2(st - m_new).astype(BF16)
        pv = jnp.dot(vt_ref[:, pl.ds(koff, tk)], e, preferred_element_type=F32)
        acc_sc[g] = alpha * acc_sc[g] + pv
        m_sc[g] = m_new

    scores(0, sa_sc)

    def body(j, carry):
        scores(2 * j + 1, sb_sc)
        consume(2 * j, sa_sc)
        scores(2 * j + 2, sa_sc)
        consume(2 * j + 1, sb_sc)
        return carry

    lax.fori_loop(0, n_items // 2 - 1, body, 0)
    scores(n_items - 1, sb_sc)
    consume(n_items - 2, sa_sc)
    consume(n_items - 1, sb_sc)
    top = lax.broadcasted_iota(I32, (KV_WIDTH, tq), 0) < HEAD_DIM
    for g in range(q_groups):
        o_t = acc_sc[g, 0:KV_WIDTH, :] * (1.0 / acc_sc[g, KV_WIDTH:KV_WIDTH + 1, :])
        for p in range(HEAD_PAIRS):
            a = o_t[:, (2 * p) * tq:(2 * p + 1) * tq]
            b = o_t[:, (2 * p + 1) * tq:(2 * p + 2) * tq]
            o_ref[g * tq:(g + 1) * tq, p * LANES:(p + 1) * LANES] = jnp.where(top, a, b).T.astype(BF16)


def _gqa_attention(q, k, vt, *, seq, tq, tk, q_groups):
    n = q.shape[0]
    rows = q_groups * tq
    assert seq % rows == 0 and seq % tk == 0
    qblocks = seq // rows
    width = GQA_Q_HEADS * tq
    return pl.pallas_call(
        functools.partial(_gqa_kernel, tq=tq, tk=tk, seq=seq),
        grid=(n // seq, qblocks),
        in_specs=[
            pl.BlockSpec((rows, GQA_WIDTH), lambda b, i: (b * qblocks + i, 0)),
            pl.BlockSpec((seq, KV_WIDTH), lambda b, i: (b, 0)),
            pl.BlockSpec((VT_ROWS, seq), lambda b, i: (b, 0)),
        ],
        out_specs=pl.BlockSpec((rows, GQA_WIDTH), lambda b, i: (b * qblocks + i, 0)),
        out_shape=jax.ShapeDtypeStruct((n, GQA_WIDTH), BF16),
        scratch_shapes=[pltpu.VMEM((q_groups, width, KV_WIDTH), BF16), pltpu.VMEM((tk, width), F32),
                        pltpu.VMEM((tk, width), F32), pltpu.VMEM((q_groups, 1, width), F32),
                        pltpu.VMEM((q_groups, VT_ROWS, width), F32)],
        compiler_params=_params("parallel", "parallel"),
        name="gqa_attention",
    )(q, k, vt)


def _layer_norm(h, g, b):
    mu = jnp.mean(h, axis=-1, keepdims=True)
    hc = h - mu
    var = jnp.mean(hc * hc, axis=-1, keepdims=True)
    return hc * lax.rsqrt(var + LN_EPS) * g + b


def _merge_kernel(na_ref, gq_ref, gate_ref, x_ref, wna_ref, wgq_ref, wout_ref, g_ref, b_ref,
                  wrh_ref, wrl_ref, x1_ref, aff_ref, *, alpha, d_model):
    y_na = jnp.dot(na_ref[...], wna_ref[...], preferred_element_type=F32)
    y_gq = jnp.dot(gq_ref[...], wgq_ref[...], preferred_element_type=F32)
    mixin = gate_ref[:, :d_model] * y_na + gate_ref[:, d_model:] * y_gq
    mix = jnp.dot(mixin.astype(BF16), wout_ref[...], preferred_element_type=F32)
    x1 = _layer_norm(alpha * x_ref[...] + mix, g_ref[...], b_ref[...])
    x1_ref[...] = x1
    hi = x1.astype(BF16)
    lo = (x1 - hi.astype(F32)).astype(BF16)
    wh = wrh_ref[...]
    logits = (lax.dot_general(wh, hi, _NT, preferred_element_type=F32)
              + lax.dot_general(wh, lo, _NT, preferred_element_type=F32)
              + lax.dot_general(wrl_ref[...], hi, _NT, preferred_element_type=F32))
    m = jnp.max(logits, axis=0, keepdims=True)
    e = jnp.exp(logits - m)
    aff_ref[...] = e / jnp.sum(e, axis=0, keepdims=True)


def _merge(na, gq, gates, x2d, wna, wgq, wout, ln_g, ln_b, wr_hi, wr_lo, *, alpha, tm):
    n, d = x2d.shape
    n_exp = wr_hi.shape[0]
    row = lambda i: (i, 0)
    const = lambda i: (0, 0)
    return pl.pallas_call(
        functools.partial(_merge_kernel, alpha=alpha, d_model=d),
        grid=(n // tm,),
        in_specs=[
            pl.BlockSpec((tm, NA_WIDTH), row), pl.BlockSpec((tm, GQA_WIDTH), row),
            pl.BlockSpec((tm, 2 * d), row), pl.BlockSpec((tm, d), row),
            pl.BlockSpec((NA_WIDTH, d), const), pl.BlockSpec((GQA_WIDTH, d), const),
            pl.BlockSpec((d, d), const), pl.BlockSpec((1, d), const), pl.BlockSpec((1, d), const),
            pl.BlockSpec((n_exp, d), const), pl.BlockSpec((n_exp, d), const),
        ],
        out_specs=[pl.BlockSpec((tm, d), row), pl.BlockSpec((n_exp, tm), lambda i: (0, i))],
        out_shape=[jax.ShapeDtypeStruct((n, d), F32), jax.ShapeDtypeStruct((n_exp, n), F32)],
        compiler_params=_params("parallel"),
        name="merge_ln_router",
    )(na, gq, gates, x2d, wna, wgq, wout, ln_g, ln_b, wr_hi, wr_lo)


def _route_kernel(aff_ref, pos_ref, base_ref, cnt_ref, *, cap, tile, count_chunk):
    n_exp, n = aff_ref.shape
    n_tiles = n // tile
    capf = float(cap)

    def bits_at(off, width):
        return lax.bitcast_convert_type(aff_ref[:, pl.ds(off, width)], I32)

    def count_ge(cand):
        def inner(c, acc):
            b = bits_at(pl.multiple_of(c * count_chunk, count_chunk), count_chunk)
            return acc + jnp.where(b >= cand, 1.0, 0.0)
        acc = lax.fori_loop(0, n // count_chunk, inner, jnp.zeros((n_exp, count_chunk), F32))
        return jnp.sum(acc, axis=1, keepdims=True)

    def bisect(i, prefix):
        cand = prefix | jnp.left_shift(jnp.int32(1), 30 - i)
        return jnp.where(count_ge(cand) >= capf, cand, prefix)

    thr = lax.fori_loop(0, 31, bisect, jnp.zeros((n_exp, 1), I32))
    need = capf - count_ge(thr + 1)

    ri = lax.broadcasted_iota(I32, (tile, tile), 0)
    ci = lax.broadcasted_iota(I32, (tile, tile), 1)
    upper = jnp.where(ri < ci, 1.0, 0.0).astype(BF16)
    ones = jnp.ones((tile, tile), BF16)
    tbl_lane = lax.broadcasted_iota(I32, base_ref.shape, 1)

    base_ref[...] = jnp.zeros(base_ref.shape, I32)
    cnt_ref[...] = jnp.zeros(cnt_ref.shape, I32)

    def body(i, carry):
        ceq, base = carry
        off = pl.multiple_of(i * tile, tile)
        b = lax.bitcast_convert_type(aff_ref[:, pl.ds(off, tile)], I32)
        eq = b == thr
        eqb = jnp.where(eq, 1.0, 0.0).astype(BF16)
        eqrank = jnp.dot(eqb, upper, preferred_element_type=F32) + ceq
        sel = (b > thr) | (eq & (eqrank < need))
        selb = jnp.where(sel, 1.0, 0.0).astype(BF16)
        rank_in_tile = jnp.dot(selb, upper, preferred_element_type=F32)
        cnt = jnp.dot(selb, ones, preferred_element_type=F32)
        pos_ref[:, pl.ds(off, tile)] = jnp.where(sel, base + rank_in_tile, float(UNPICKED)).astype(I32)
        base_ref[...] = jnp.where(tbl_lane == i, base[:, :1].astype(I32), base_ref[...])
        cnt_ref[...] = jnp.where(tbl_lane == i, cnt[:, :1].astype(I32), cnt_ref[...])
        padded = jnp.floor((cnt + (SUBLANES - 1.0)) * (1.0 / SUBLANES)) * SUBLANES
        return ceq + jnp.dot(eqb, ones, preferred_element_type=F32), base + padded

    zero = jnp.zeros((n_exp, tile), F32)
    _, used = lax.fori_loop(0, n_tiles, body, (zero, zero))
    base_ref[...] = jnp.where(tbl_lane == n_tiles, used[:, :1].astype(I32), base_ref[...])


def _route(aff, *, cap, tile):
    n_exp, n = aff.shape
    count_chunk = min(2048, n)
    assert n % count_chunk == 0 and n % tile == 0
    tbl_w = pl.cdiv(n // tile + 1, LANES) * LANES
    full = lambda shape: pl.BlockSpec(shape, lambda i: (0, 0))
    return pl.pallas_call(
        functools.partial(_route_kernel, cap=cap, tile=tile, count_chunk=count_chunk),
        grid=(1,),
        in_specs=[full((n_exp, n))],
        out_specs=[full((n_exp, n)), full((n_exp, tbl_w)), full((n_exp, tbl_w))],
        out_shape=[jax.ShapeDtypeStruct((n_exp, n), I32), jax.ShapeDtypeStruct((n_exp, tbl_w), I32),
                   jax.ShapeDtypeStruct((n_exp, tbl_w), I32)],
        compiler_params=_params("arbitrary"),
        name="route",
    )(aff)


def _pad_rows(c):
    return ((c + (SUBLANES - 1)) >> 3) << 3


def _start_pieces(rows, max_rows, make_copy):
    k = 3
    assert SUBLANES == 1 << k
    while (1 << k) <= max_rows:
        size = 1 << k

        @pl.when(((rows >> k) & 1) == 1)
        def _():
            make_copy(pl.multiple_of((rows >> (k + 1)) << (k + 1), SUBLANES), size).start()

        k += 1


def _wait_rows(hbm_ref, rows, sem):
    @pl.when(rows > 0)
    def _():
        view = hbm_ref.at[pl.ds(0, pl.multiple_of(rows, SUBLANES))]
        pltpu.make_async_copy(view, view, sem).wait()


def _dispatch_kernel(base_ref, cnt_ref, x_ref, aff_ref, pos_ref, xe_ref, few_sc, long_sc, few_sems, long_sem,
                     *, tile, cap_pad, n_tiles):
    i = pl.program_id(0)
    n_exp = pos_ref.shape[0]
    d = x_ref.shape[1]
    xb = x_ref[...].astype(BF16)
    a = aff_ref[...]
    hi = a.astype(BF16)
    rest1 = a - hi.astype(F32)
    mid = rest1.astype(BF16)
    lo = (rest1 - mid.astype(F32)).astype(BF16)
    pr = lax.broadcasted_iota(I32, (n_exp, LANES), 0)
    pc = lax.broadcasted_iota(I32, (n_exp, LANES), 1)
    gates = sum(jnp.dot(piece, jnp.where(pc == k * n_exp + pr, 1.0, 0.0).astype(BF16), preferred_element_type=F32)
                for k, piece in enumerate((hi, mid, lo))).astype(BF16)
    few = DISPATCH_FEW_ROWS
    par = i % 2

    def few_rows_total(j):
        total = jnp.int32(0)
        for e in range(n_exp):
            r = _pad_rows(cnt_ref[e, j])
            total = total + jnp.where(r <= few, r, 0)
        return total

    @pl.when(i >= 2)
    def _():
        _wait_rows(xe_ref, few_rows_total(jnp.maximum(i - 2, 0)), few_sems.at[par])

    bases = [base_ref[e, i] for e in range(n_exp)]
    padded = [_pad_rows(cnt_ref[e, i]) for e in range(n_exp)]

    def compact(e, n_rows):
        row_id = lax.broadcasted_iota(I32, (n_rows, tile), 0)
        return jnp.where(row_id == pos_ref[e:e + 1, :] - bases[e], 1.0, 0.0).astype(BF16)

    onehot = jnp.concatenate([compact(e, few) for e in range(n_exp)], axis=0)
    few_sc[par, :, 0:d] = jnp.dot(onehot, xb, preferred_element_type=F32)
    few_sc[par, :, d:d + LANES] = jnp.dot(onehot, gates, preferred_element_type=F32)

    long_total = jnp.int32(0)
    for e in range(n_exp):
        rows = padded[e]
        dst0 = e * cap_pad + bases[e]
        is_long = rows > few

        def few_copy(off, size):
            return pltpu.make_async_copy(few_sc.at[par, pl.ds(pl.multiple_of(e * few + off, SUBLANES), size)],
                                         xe_ref.at[pl.ds(pl.multiple_of(dst0 + off, SUBLANES), size)],
                                         few_sems.at[par])

        _start_pieces(jnp.where(is_long, 0, rows), few, few_copy)

        @pl.when(is_long)
        def _():
            full = compact(e, tile)
            long_sc[e, :, 0:d] = jnp.dot(full, xb, preferred_element_type=F32)
            long_sc[e, :, d:d + LANES] = jnp.dot(full, gates, preferred_element_type=F32)

            def long_copy(off, size):
                return pltpu.make_async_copy(long_sc.at[e, pl.ds(off, size)],
                                             xe_ref.at[pl.ds(pl.multiple_of(dst0 + off, SUBLANES), size)], long_sem)

            _start_pieces(rows, tile, long_copy)

        long_total = long_total + jnp.where(is_long, rows, 0)
    _wait_rows(xe_ref, long_total, long_sem)

    @pl.when(i == n_tiles - 1)
    def _():
        _wait_rows(xe_ref, few_rows_total(i), few_sems.at[par])

        @pl.when(i >= 1)
        def _():
            _wait_rows(xe_ref, few_rows_total(jnp.maximum(i - 1, 0)), few_sems.at[1 - par])

        stg = long_sc
        sems = few_sems
        stg[0] = jnp.zeros((tile, stg.shape[2]), F32)
        for e in range(n_exp):
            used = base_ref[e, n_tiles]
            tail = cap_pad - used
            dst0 = e * cap_pad + used

            def zero_copy(off, size):
                return pltpu.make_async_copy(stg.at[0, pl.ds(0, size)],
                                             xe_ref.at[pl.ds(pl.multiple_of(dst0 + off, SUBLANES), size)], sems.at[0])

            def whole(j, carry):
                cp = zero_copy(j * tile, tile)
                cp.start()
                cp.wait()
                return carry

            n_whole = tail // tile
            lax.fori_loop(0, n_whole, whole, 0)
            rest = tail - n_whole * tile
            _start_pieces(rest, tile, lambda off, size: zero_copy(n_whole * tile + off, size))
            _wait_rows(xe_ref, rest, sems.at[0])


def _dispatch(x1, aff_t, pos, base_tbl, cnt_tbl, *, cap_pad, tile):
    n, d = x1.shape
    n_exp = pos.shape[0]
    assert 3 * n_exp <= LANES
    n_tiles = n // tile
    return pl.pallas_call(
        functools.partial(_dispatch_kernel, tile=tile, cap_pad=cap_pad, n_tiles=n_tiles),
        grid_spec=pltpu.PrefetchScalarGridSpec(
            num_scalar_prefetch=2,
            grid=(n_tiles,),
            in_specs=[pl.BlockSpec((tile, d), lambda i, b, c: (i, 0)),
                      pl.BlockSpec((tile, n_exp), lambda i, b, c: (i, 0)),
                      pl.BlockSpec((n_exp, tile), lambda i, b, c: (0, i))],
            out_specs=pl.BlockSpec(memory_space=pl.ANY),
            scratch_shapes=[pltpu.VMEM((2, n_exp * DISPATCH_FEW_ROWS, d + LANES), F32),
                            pltpu.VMEM((n_exp, tile, d + LANES), F32),
                            pltpu.SemaphoreType.DMA((2,)), pltpu.SemaphoreType.DMA(())],
        ),
        out_shape=jax.ShapeDtypeStruct((n_exp * cap_pad, d + LANES), F32),
        compiler_params=_params("arbitrary"),
        name="dispatch",
    )(base_tbl, cnt_tbl, x1, aff_t, pos)


def _ffn_kernel(base_ref, xe_ref, wg_ref, wu_ref, wd_ref, ye_ref, *, f_chunk, n_tiles):
    tm = xe_ref.shape[0]
    d = ye_ref.shape[1]
    n_exp = pl.num_programs(0)
    e = pl.program_id(0)
    used = base_ref[e, n_tiles]
    live = pl.program_id(1) * tm < used

    @pl.when(live)
    def _():
        xb = xe_ref[:, 0:d].astype(BF16)
        lane = lax.broadcasted_iota(I32, (tm, LANES), 1)
        mine = (lane % n_exp == e) & (lane < 3 * n_exp)
        gate = jnp.sum(jnp.where(mine, xe_ref[:, d:d + LANES], 0.0), axis=1, keepdims=True)
        d_ff = wg_ref.shape[2]
        acc = jnp.zeros(ye_ref.shape, F32)
        for c in range(d_ff // f_chunk):
            cols = slice(c * f_chunk, (c + 1) * f_chunk)
            g = jnp.dot(xb, wg_ref[0, :, cols], preferred_element_type=F32)
            u = jnp.dot(xb, wu_ref[0, :, cols], preferred_element_type=F32)
            h = (g * jax.nn.sigmoid(g)) * u
            acc = acc + jnp.dot(h.astype(BF16), wd_ref[0, cols, :], preferred_element_type=F32)
        ye_ref[...] = acc * gate

    @pl.when(jnp.logical_not(live))
    def _():
        ye_ref[...] = jnp.zeros(ye_ref.shape, F32)


def _ffn(xe, base_tbl, wg, wu, wd, *, cap_pad, tm, f_chunk, n_tiles):
    n_slots, d_ext = xe.shape
    n_exp, d, d_ff = wg.shape
    assert cap_pad % tm == 0 and d_ff % f_chunk == 0 and d_ext == d + LANES
    kblocks = cap_pad // tm
    return pl.pallas_call(
        functools.partial(_ffn_kernel, f_chunk=f_chunk, n_tiles=n_tiles),
        grid_spec=pltpu.PrefetchScalarGridSpec(
            num_scalar_prefetch=1,
            grid=(n_exp, kblocks),
            in_specs=[
                pl.BlockSpec((tm, d_ext), lambda e, k, b: (e * kblocks + k, 0)),
                pl.BlockSpec((1, d, d_ff), lambda e, k, b: (e, 0, 0)),
                pl.BlockSpec((1, d, d_ff), lambda e, k, b: (e, 0, 0)),
                pl.BlockSpec((1, d_ff, d), lambda e, k, b: (e, 0, 0)),
            ],
            out_specs=pl.BlockSpec((tm, d), lambda e, k, b: (e * kblocks + k, 0)),
        ),
        out_shape=jax.ShapeDtypeStruct((n_slots, d), F32),
        compiler_params=_params("parallel", "parallel"),
        name="expert_ffn",
    )(base_tbl, xe, wg, wu, wd)


def _combine_kernel(base_ref, cnt_ref, x_ref, pos_ref, g_ref, b_ref, ye_ref, o_ref,
                    stg, hit_sc, sems, *, tile, cap_pad, n_tiles, alpha):
    i = pl.program_id(0)
    n_exp = pos_ref.shape[1]
    half = stg.shape[0] // 2

    @pl.when(i == 0)
    def _():
        stg[...] = jnp.zeros(stg.shape, F32)

    def layout(j):
        segs, off = [], jnp.int32(0)
        for e in range(n_exp):
            rows = _pad_rows(cnt_ref[e, j])
            segs.append((base_ref[e, j], rows, off))
            off = off + rows
        return segs, off

    def first_row(j, total):
        return jnp.where(total > half, 0, (j % 2) * half)

    def fetch(j, segs, row0):
        for e, (base, rows, off) in enumerate(segs):
            src0 = e * cap_pad + base
            dst0 = row0 + off

            def make_copy(o, size):
                return pltpu.make_async_copy(ye_ref.at[pl.ds(pl.multiple_of(src0 + o, SUBLANES), size)],
                                             stg.at[pl.ds(pl.multiple_of(dst0 + o, SUBLANES), size)], sems.at[j % 2])

            _start_pieces(rows, tile, make_copy)

    nxt = jnp.minimum(i + 1, n_tiles - 1)
    segs, total = layout(i)
    segs_next, total_next = layout(nxt)
    _, total_prev = layout(jnp.maximum(i - 1, 0))
    row0 = first_row(i, total)
    small, small_next, small_prev = total <= half, total_next <= half, total_prev <= half

    @pl.when(jnp.logical_not((i > 0) & small_prev & small))
    def _():
        fetch(i, segs, row0)

    _wait_rows(ye_ref, total, sems.at[i % 2])

    @pl.when((i + 1 < n_tiles) & small & small_next)
    def _():
        fetch(nxt, segs_next, first_row(nxt, total_next))

    o_ref[...] = alpha * x_ref[...]
    slot_id = lax.broadcasted_iota(I32, (tile, tile), 1)

    def block(b, carry):
        chunk = stg[pl.ds(pl.multiple_of(row0 + b * tile, tile), tile), :].astype(BF16)
        hit_sc[...] = jnp.zeros(hit_sc.shape, F32)
        for e, (base, rows, off) in enumerate(segs):
            @pl.when((off < (b + 1) * tile) & (off + rows > b * tile))
            def _():
                match = slot_id == (pos_ref[:, e:e + 1] - (base - off + b * tile))
                hit_sc[...] = jnp.where(match, 1.0, hit_sc[...])

        o_ref[...] += jnp.dot(hit_sc[...].astype(BF16), chunk, preferred_element_type=F32)
        return carry

    lax.fori_loop(0, (total + (tile - 1)) // tile, block, 0)
    o_ref[...] = _layer_norm(o_ref[...], g_ref[...], b_ref[...])


def _combine(x1, pos_t, base_tbl, cnt_tbl, ln_g, ln_b, ye, *, cap_pad, alpha, tile):
    n, d = x1.shape
    n_exp = pos_t.shape[1]
    row = lambda i, b, c: (i, 0)
    const = lambda i, b, c: (0, 0)
    stg_rows = n_exp * tile
    return pl.pallas_call(
        functools.partial(_combine_kernel, tile=tile, cap_pad=cap_pad, n_tiles=n // tile, alpha=alpha),
        grid_spec=pltpu.PrefetchScalarGridSpec(
            num_scalar_prefetch=2,
            grid=(n // tile,),
            in_specs=[
                pl.BlockSpec((tile, d), row), pl.BlockSpec((tile, n_exp), row),
                pl.BlockSpec((1, d), const), pl.BlockSpec((1, d), const),
                pl.BlockSpec(memory_space=pl.ANY),
            ],
            out_specs=pl.BlockSpec((tile, d), row),
            scratch_shapes=[pltpu.VMEM((stg_rows, d), F32), pltpu.VMEM((tile, tile), F32),
                            pltpu.SemaphoreType.DMA((2,))],
        ),
        out_shape=jax.ShapeDtypeStruct((n, d), F32),
        compiler_params=_params("arbitrary"),
        name="combine_ln",
    )(base_tbl, cnt_tbl, x1, pos_t, ln_g, ln_b, ye)


def _rope_tables(seq):
    t = jnp.arange(seq)
    row = (t // GRID_W).astype(F32)
    col = (t % GRID_W).astype(F32)
    half = HEAD_DIM // 2
    inv_freq = ROPE_THETA ** (-jnp.arange(0, half, 2, dtype=F32) / half)
    ang_r = row[:, None] * inv_freq[None, :]
    ang_c = col[:, None] * inv_freq[None, :]
    ang = jnp.concatenate([ang_r, ang_r, ang_c, ang_c], axis=-1)
    sign = jnp.where((jnp.arange(HEAD_DIM) % half) < half // 2, -1.0, 1.0).astype(F32)
    reps = LANES // HEAD_DIM
    return jnp.tile(jnp.cos(ang), (1, reps)), jnp.tile(jnp.sin(ang) * sign[None, :], (1, reps))


def _gqa_slot_columns():
    heads = [p + GQA_GROUP * half for p in range(HEAD_PAIRS) for half in range(2)]
    return np.concatenate([np.arange(HEAD_DIM) + HEAD_DIM * h for h in heads])


def _prep_layer(w_in, na_rpb, q_norm, k_norm, w_br_na, w_br_gqa, w_out, ln1_g, ln1_b,
                w_router, w_e_gate, w_e_up, w_e_down, ln2_g, ln2_b):
    d = w_in.shape[0]
    s_na = 3 * NA_WIDTH
    cols = _gqa_slot_columns()
    perm = np.concatenate([np.arange(s_na), s_na + cols, np.arange(s_na + GQA_WIDTH, w_in.shape[1])])
    reps = LANES // HEAD_DIM
    wr_t = w_router.T.astype(F32)
    wr_hi = wr_t.astype(BF16)
    gm = np.kron(np.eye(reps, dtype=np.float32), np.full((HEAD_DIM, HEAD_DIM), 1.0 / HEAD_DIM, np.float32))
    return dict(
        w_in=w_in[:, perm].astype(BF16),
        bias_tbl=_na_bias_table(na_rpb),
        qn=jnp.tile(q_norm.astype(F32), reps)[None, :], kn=jnp.tile(k_norm.astype(F32), reps)[None, :],
        gm=jnp.asarray(gm, BF16),
        wna=w_br_na.astype(BF16), wgq=w_br_gqa[cols].astype(BF16), wout=w_out.astype(BF16),
        ln1_g=ln1_g.astype(F32).reshape(1, d), ln1_b=ln1_b.astype(F32).reshape(1, d),
        wr_hi=wr_hi, wr_lo=(wr_t - wr_hi.astype(F32)).astype(BF16),
        wg=w_e_gate.astype(BF16), wu=w_e_up.astype(BF16), wd=w_e_down.astype(BF16),
        ln2_g=ln2_g.astype(F32).reshape(1, d), ln2_b=ln2_b.astype(F32).reshape(1, d),
    )


def _tiles(seq):
    return dict(tm_proj=512, tq=256, tk=1024, q_groups=2, tm_merge=256, t_moe=256, tm_ffn=512, f_chunk=512)


def _trunk_layer(x2d, p, *, seq, alpha, rope):
    n, d = x2d.shape
    n_exp = p["wg"].shape[0]
    cap = EC_CAPACITY * n // n_exp
    tl = _tiles(seq)
    cos, sin = rope
    naq, nak, nav, gq, gk, gv, gates = _inproj(x2d, p["w_in"], cos, sin, p["qn"], p["kn"], p["gm"],
                                               seq=seq, tm=min(tl["tm_proj"], seq))
    na = _na_attention(naq, nak, nav, p["bias_tbl"], seq=seq)
    nb = n // seq
    ones_rows = jnp.zeros((nb, VT_ROWS - KV_WIDTH, seq), BF16).at[:, 0, :].set(1.0)
    gvt = jnp.concatenate([gv.reshape(nb, seq, KV_WIDTH).transpose(0, 2, 1), ones_rows], axis=1)
    gvt = gvt.reshape(nb * VT_ROWS, seq)
    ga = _gqa_attention(gq, gk, gvt, seq=seq, tq=min(tl["tq"], seq), tk=min(tl["tk"], seq // 2),
                        q_groups=tl["q_groups"])
    x1, aff = _merge(na, ga, gates, x2d, p["wna"], p["wgq"], p["wout"], p["ln1_g"], p["ln1_b"],
                     p["wr_hi"], p["wr_lo"], alpha=alpha, tm=tl["tm_merge"])
    tile = tl["t_moe"]
    n_tiles = n // tile
    tm_ffn = min(tl["tm_ffn"], cap)
    cap_pad = pl.cdiv(cap + (SUBLANES - 1) * n_tiles, tm_ffn) * tm_ffn
    pos, base_tbl, cnt_tbl = _route(aff, cap=cap, tile=tile)
    xe = _dispatch(x1, aff.T, pos, base_tbl, cnt_tbl, cap_pad=cap_pad, tile=tile)
    ye = _ffn(xe, base_tbl, p["wg"], p["wu"], p["wd"], cap_pad=cap_pad, tm=tm_ffn, f_chunk=tl["f_chunk"],
              n_tiles=n_tiles)
    return _combine(x1, pos.T, base_tbl, cnt_tbl, p["ln2_g"], p["ln2_b"], ye,
                    cap_pad=cap_pad, alpha=alpha, tile=tile)


@jax.jit
def kernel(x_prompt, x_sample, w_in, na_rpb, q_norm, k_norm, w_br_na, w_br_gqa, w_out, ln1_g, ln1_b,
           w_router, w_e_gate, w_e_up, w_e_down, ln2_g, ln2_b):
    depth = w_in.shape[0]
    alpha = float((2 * depth) ** 0.25)
    d = x_prompt.shape[-1]
    groups = []
    for x in (x_prompt, x_sample):
        b, s, _ = x.shape
        groups.append(dict(x=x.reshape(b * s, d), shape=x.shape, seq=s, rope=_rope_tables(s)))
    for l in range(depth):
        p = _prep_layer(w_in[l], na_rpb[l], q_norm[l], k_norm[l], w_br_na[l], w_br_gqa[l], w_out[l],
                        ln1_g[l], ln1_b[l], w_router[l], w_e_gate[l], w_e_up[l], w_e_down[l],
                        ln2_g[l], ln2_b[l])
        for g in groups:
            g["x"] = _trunk_layer(g["x"], p, seq=g["seq"], alpha=alpha, rope=g["rope"])
    return tuple(g["x"].reshape(g["shape"]) for g in groups)
```

```python
import functools

import jax
import jax.numpy as jnp
import numpy as np
from jax import lax
from jax.experimental import pallas as pl
from jax.experimental.pallas import tpu as pltpu

F32 = jnp.float32
BF16 = jnp.bfloat16
I32 = jnp.int32

GRID_W = 64
HEAD_DIM = 64
NA_HEADS = 8
NA_WIN_H = 8
NA_WIN_W = 16
GQA_Q_HEADS = 8
GQA_KV_HEADS = 2
GQA_GROUP = GQA_Q_HEADS // GQA_KV_HEADS
ROPE_THETA = 10000.0
EC_CAPACITY = 2
LN_EPS = 1e-5
RMS_EPS = 1e-6

LANES = 128
SUBLANES = 8
NA_WIDTH = NA_HEADS * HEAD_DIM
GQA_WIDTH = GQA_Q_HEADS * HEAD_DIM
KV_WIDTH = GQA_KV_HEADS * HEAD_DIM
HEAD_PAIRS = GQA_WIDTH // LANES
BF16_TILE_ROWS = 2 * SUBLANES
VT_ROWS = KV_WIDTH + BF16_TILE_ROWS
UNPICKED = -(1 << 30)
MERGE_PART_ROWS = 256
DISPATCH_FEW_ROWS = 64
GQA_Q_SCALE = HEAD_DIM ** -0.5 * float(np.log2(np.e))
NEG_BIG = -1e30
VMEM_LIMIT_BYTES = 56 * 1024 * 1024

_NT = (((1,), (1,)), ((), ()))


def _params(*sem):
    return pltpu.CompilerParams(dimension_semantics=sem, vmem_limit_bytes=VMEM_LIMIT_BYTES)


def _inproj_kernel(x_ref, w_ref, cos_ref, sin_ref, qn_ref, kn_ref, gm_ref,
                   naq_ref, nak_ref, nav_ref, gq_ref, gk_ref, gv_ref, gate_ref, *, d_model):
    xb = x_ref[...].astype(BF16)
    s_na = 3 * NA_WIDTH
    s_gq = s_na + GQA_WIDTH
    s_gk = s_gq + KV_WIDTH
    s_gv = s_gk + KV_WIDTH

    def proj(c0, width):
        return jnp.dot(xb, w_ref[:, c0:c0 + width], preferred_element_type=F32)

    naq_ref[...] = (proj(0, NA_WIDTH) * HEAD_DIM ** -0.5).astype(BF16)
    nak_ref[...] = proj(NA_WIDTH, NA_WIDTH).astype(BF16)
    nav_ref[...] = proj(2 * NA_WIDTH, NA_WIDTH).astype(BF16)

    cos = cos_ref[...]
    sin = sin_ref[...]
    gm = gm_ref[...]
    lane = lax.broadcasted_iota(I32, cos.shape, 1)
    first_half = (lane % (HEAD_DIM // 2)) < (HEAD_DIM // 4)

    def norm_rope(a, gain):
        sq = a * a
        hi = sq.astype(BF16)
        lo = (sq - hi.astype(F32)).astype(BF16)
        ms = jnp.dot(hi, gm, preferred_element_type=F32) + jnp.dot(lo, gm, preferred_element_type=F32)
        an = a * lax.rsqrt(ms + RMS_EPS) * gain
        quarter = HEAD_DIM // 4
        rot = jnp.where(first_half, pltpu.roll(an, LANES - quarter, 1), pltpu.roll(an, quarter, 1))
        return an * cos + rot * sin

    gq_raw = [proj(s_na + p * LANES, LANES) for p in range(HEAD_PAIRS)]
    gk_raw = proj(s_gq, KV_WIDTH)
    gv_ref[...] = proj(s_gk, KV_WIDTH).astype(BF16)
    gate_chunk = 512
    for c in range(2 * d_model // gate_chunk):
        g = proj(s_gv + c * gate_chunk, gate_chunk)
        gate_ref[:, c * gate_chunk:(c + 1) * gate_chunk] = jax.nn.sigmoid(g)
    qn = qn_ref[...]
    for p in range(HEAD_PAIRS):
        gq_ref[:, p * LANES:(p + 1) * LANES] = (norm_rope(gq_raw[p], qn) * GQA_Q_SCALE).astype(BF16)
    gk_ref[...] = norm_rope(gk_raw, kn_ref[...]).astype(BF16)


def _inproj(x2d, w_in_b, cos, sin, qn, kn, gm, *, seq, tm):
    n, d = x2d.shape
    d_in = w_in_b.shape[1]
    assert n % tm == 0 and seq % tm == 0
    sblocks = seq // tm
    row = lambda i: (i, 0)
    const = lambda i: (0, 0)
    pos = lambda i: (i % sblocks, 0)
    out_shape = [
        jax.ShapeDtypeStruct((n, NA_WIDTH), BF16), jax.ShapeDtypeStruct((n, NA_WIDTH), BF16),
        jax.ShapeDtypeStruct((n, NA_WIDTH), BF16), jax.ShapeDtypeStruct((n, GQA_WIDTH), BF16),
        jax.ShapeDtypeStruct((n, KV_WIDTH), BF16), jax.ShapeDtypeStruct((n, KV_WIDTH), BF16),
        jax.ShapeDtypeStruct((n, 2 * d), F32),
    ]
    out_specs = [
        pl.BlockSpec((tm, NA_WIDTH), row), pl.BlockSpec((tm, NA_WIDTH), row), pl.BlockSpec((tm, NA_WIDTH), row),
        pl.BlockSpec((tm, GQA_WIDTH), row), pl.BlockSpec((tm, KV_WIDTH), row), pl.BlockSpec((tm, KV_WIDTH), row),
        pl.BlockSpec((tm, 2 * d), row),
    ]
    return pl.pallas_call(
        functools.partial(_inproj_kernel, d_model=d),
        grid=(n // tm,),
        in_specs=[
            pl.BlockSpec((tm, d), row), pl.BlockSpec((d, d_in), const),
            pl.BlockSpec((tm, LANES), pos), pl.BlockSpec((tm, LANES), pos),
            pl.BlockSpec((1, LANES), const), pl.BlockSpec((1, LANES), const),
            pl.BlockSpec((LANES, LANES), const),
        ],
        out_specs=out_specs,
        out_shape=out_shape,
        compiler_params=_params("parallel"),
        name="inproj",
    )(x2d, w_in_b, cos, sin, qn, kn, gm)


def _na_kernel(q_ref, kp_ref, kc_ref, kn_ref, vp_ref, vc_ref, vn_ref, bias_ref, o_ref,
               kbuf, vbuf, *, rows, nrb):
    blk = NA_WIN_H * GRID_W
    j = pl.program_id(0) % nrb
    kbuf[0:blk] = kp_ref[...]
    kbuf[blk:2 * blk] = kc_ref[...]
    kbuf[2 * blk:3 * blk] = kn_ref[...]
    vbuf[0:blk] = vp_ref[...]
    vbuf[blk:2 * blk] = vc_ref[...]
    vbuf[2 * blk:3 * blk] = vn_ref[...]
    lane = lax.broadcasted_iota(I32, (GRID_W, LANES), 1)
    low = lane < HEAD_DIM

    def row_body(i, carry):
        r = j * NA_WIN_H + i
        rs = jnp.clip(r - NA_WIN_H // 2, 0, rows - NA_WIN_H)
        d0 = rs - r + (NA_WIN_H - 1)
        off = pl.multiple_of((rs - (j - 1) * NA_WIN_H) * GRID_W, GRID_W)
        qoff = pl.multiple_of(i * GRID_W, GRID_W)
        pairs = range(NA_WIDTH // LANES)
        scores = []
        for p in pairs:
            cols = slice(p * LANES, (p + 1) * LANES)
            q2 = q_ref[pl.ds(qoff, GRID_W), cols]
            zero = jnp.zeros_like(q2)
            qs = jnp.concatenate([jnp.where(low, q2, zero), jnp.where(low, zero, q2)], axis=0)
            k2 = kbuf[pl.ds(off, blk), cols]
            scores.append(lax.dot_general(k2, qs, _NT, preferred_element_type=F32) + bias_ref[d0, p])
        probs = []
        for st in scores:
            e = jnp.exp(st - jnp.max(st, axis=0, keepdims=True))
            probs.append((e * (1.0 / jnp.sum(e, axis=0, keepdims=True))).T.astype(BF16))
        for p in pairs:
            cols = slice(p * LANES, (p + 1) * LANES)
            o = jnp.dot(probs[p], vbuf[pl.ds(off, blk), cols], preferred_element_type=F32)
            o_ref[pl.ds(qoff, GRID_W), cols] = jnp.where(low, o[:GRID_W], o[GRID_W:]).astype(BF16)
        return carry

    lax.fori_loop(0, NA_WIN_H, row_body, 0)


def _na_bias_table(rpb):
    c = jnp.arange(GRID_W)
    cs = jnp.clip(c - NA_WIN_W // 2, 0, GRID_W - NA_WIN_W)
    cc = jnp.arange(GRID_W)
    inwin = (cc[None, :] >= cs[:, None]) & (cc[None, :] < cs[:, None] + NA_WIN_W)
    dc = jnp.clip(cc[None, :] - c[:, None] + (NA_WIN_W - 1), 0, 2 * NA_WIN_W - 2)
    full = jnp.where(inwin[None, None], rpb[:, :, dc].astype(F32), NEG_BIG)
    tbl = jnp.stack([full[:, d0:d0 + NA_WIN_H] for d0 in range(NA_WIN_H)])
    tbl = tbl.reshape(NA_WIN_H, NA_HEADS // 2, 2, NA_WIN_H, GRID_W, GRID_W)
    return tbl.transpose(0, 1, 3, 5, 2, 4).reshape(NA_WIN_H, NA_HEADS // 2, NA_WIN_H * GRID_W, 2 * GRID_W)


def _na_attention(q, k, v, bias_tbl, *, seq):
    n = q.shape[0]
    rows = seq // GRID_W
    assert rows % NA_WIN_H == 0 and rows >= 2 * NA_WIN_H
    nrb = rows // NA_WIN_H
    blk = NA_WIN_H * GRID_W

    def cur(g):
        return (g, 0)

    def prev(g):
        return (g - jnp.where(g % nrb == 0, 0, 1), 0)

    def nxt(g):
        return (g + jnp.where(g % nrb == nrb - 1, 0, 1), 0)

    spec = lambda f: pl.BlockSpec((blk, NA_WIDTH), f)
    return pl.pallas_call(
        functools.partial(_na_kernel, rows=rows, nrb=nrb),
        grid=(n // blk,),
        in_specs=[spec(cur), spec(prev), spec(cur), spec(nxt), spec(prev), spec(cur), spec(nxt),
                  pl.BlockSpec(bias_tbl.shape, lambda g: (0, 0, 0, 0))],
        out_specs=spec(cur),
        out_shape=jax.ShapeDtypeStruct((n, NA_WIDTH), BF16),
        scratch_shapes=[pltpu.VMEM((3 * blk, NA_WIDTH), BF16), pltpu.VMEM((3 * blk, NA_WIDTH), BF16)],
        compiler_params=_params("parallel"),
        name="na_attention",
    )(q, k, k, k, v, v, v, bias_tbl)


def _gqa_kernel(q_ref, k_ref, vt_ref, o_ref, qs_sc, sa_sc, sb_sc, m_sc, acc_sc, *, tq, tk, seq):
    q_groups = qs_sc.shape[0]
    lane = lax.broadcasted_iota(I32, (tq, LANES), 1)
    low = lane < HEAD_DIM
    for g in range(q_groups):
        for p in range(HEAD_PAIRS):
            q2 = q_ref[g * tq:(g + 1) * tq, p * LANES:(p + 1) * LANES]
            zero = jnp.zeros_like(q2)
            qs_sc[g, (2 * p) * tq:(2 * p + 1) * tq, :] = jnp.where(low, q2, zero)
            qs_sc[g, (2 * p + 1) * tq:(2 * p + 2) * tq, :] = jnp.where(low, zero, q2)
    m_sc[...] = jnp.full(m_sc.shape, -jnp.inf, F32)
    acc_sc[...] = jnp.zeros(acc_sc.shape, F32)
    n_chunks = seq // tk
    n_items = q_groups * n_chunks
    assert n_chunks % 2 == 0

    def scores(w, st_ref):
        koff = pl.multiple_of((w % n_chunks) * tk, tk)
        st_ref[...] = lax.dot_general(k_ref[pl.ds(koff, tk), :], qs_sc[w // n_chunks], _NT,
                                      preferred_element_type=F32)

    def consume(w, st_ref):
        g = w // n_chunks
        koff = pl.multiple_of((w % n_chunks) * tk, tk)
        st = st_ref[...]
        m_old = m_sc[g]
        m_new = jnp.maximum(m_old, jnp.max(st, axis=0, keepdims=True))
        alpha = jnp.exp2(m_old - m_new)
        e = jnp.exp2(st - m_new).astype(BF16)
        pv = jnp.dot(vt_ref[:, pl.ds(koff, tk)], e, preferred_element_type=F32)
        acc_sc[g] = alpha * acc_sc[g] + pv
        m_sc[g] = m_new

    scores(0, sa_sc)

    def body(j, carry):
        scores(2 * j + 1, sb_sc)
        consume(2 * j, sa_sc)
        scores(2 * j + 2, sa_sc)
        consume(2 * j + 1, sb_sc)
        return carry

    lax.fori_loop(0, n_items // 2 - 1, body, 0)
    scores(n_items - 1, sb_sc)
    consume(n_items - 2, sa_sc)
    consume(n_items - 1, sb_sc)
    top = lax.broadcasted_iota(I32, (KV_WIDTH, tq), 0) < HEAD_DIM
    for g in range(q_groups):
        o_t = acc_sc[g, 0:KV_WIDTH, :] * (1.0 / acc_sc[g, KV_WIDTH:KV_WIDTH + 1, :])
        for p in range(HEAD_PAIRS):
            a = o_t[:, (2 * p) * tq:(2 * p + 1) * tq]
            b = o_t[:, (2 * p + 1) * tq:(2 * p + 2) * tq]
            o_ref[g * tq:(g + 1) * tq, p * LANES:(p + 1) * LANES] = jnp.where(top, a, b).T.astype(BF16)


def _gqa_attention(q, k, vt, *, seq, tq, tk, q_groups):
    n = q.shape[0]
    rows = q_groups * tq
    assert seq % rows == 0 and seq % tk == 0
    qblocks = seq // rows
    width = GQA_Q_HEADS * tq
    return pl.pallas_call(
        functools.partial(_gqa_kernel, tq=tq, tk=tk, seq=seq),
        grid=(n // seq, qblocks),
        in_specs=[
            pl.BlockSpec((rows, GQA_WIDTH), lambda b, i: (b * qblocks + i, 0)),
            pl.BlockSpec((seq, KV_WIDTH), lambda b, i: (b, 0)),
            pl.BlockSpec((VT_ROWS, seq), lambda b, i: (b, 0)),
        ],
        out_specs=pl.BlockSpec((rows, GQA_WIDTH), lambda b, i: (b * qblocks + i, 0)),
        out_shape=jax.ShapeDtypeStruct((n, GQA_WIDTH), BF16),
        scratch_shapes=[pltpu.VMEM((q_groups, width, KV_WIDTH), BF16), pltpu.VMEM((tk, width), F32),
                        pltpu.VMEM((tk, width), F32), pltpu.VMEM((q_groups, 1, width), F32),
                        pltpu.VMEM((q_groups, VT_ROWS, width), F32)],
        compiler_params=_params("parallel", "parallel"),
        name="gqa_attention",
    )(q, k, vt)


def _layer_norm(h, g, b):
    mu = jnp.mean(h, axis=-1, keepdims=True)
    hc = h - mu
    var = jnp.mean(hc * hc, axis=-1, keepdims=True)
    return hc * lax.rsqrt(var + LN_EPS) * g + b


def _merge_kernel(na_ref, gq_ref, gate_ref, x_ref, wna_ref, wgq_ref, wout_ref, g_ref, b_ref,
                  wrh_ref, wrl_ref, x1_ref, aff_ref, *, alpha, d_model):
    tm = x_ref.shape[0]
    parts = [slice(r, r + MERGE_PART_ROWS) for r in range(0, tm, MERGE_PART_ROWS)]
    y_na = [jnp.dot(na_ref[rows, :], wna_ref[...], preferred_element_type=F32) for rows in parts]
    y_gq = [jnp.dot(gq_ref[rows, :], wgq_ref[...], preferred_element_type=F32) for rows in parts]
    mixin = [(gate_ref[rows, :d_model] * a + gate_ref[rows, d_model:] * b).astype(BF16)
             for rows, a, b in zip(parts, y_na, y_gq)]
    mix = [jnp.dot(m, wout_ref[...], preferred_element_type=F32) for m in mixin]
    x1 = [_layer_norm(alpha * x_ref[rows, :] + m, g_ref[...], b_ref[...]) for rows, m in zip(parts, mix)]
    wh = wrh_ref[...]
    for rows, x1p in zip(parts, x1):
        x1_ref[rows, :] = x1p
        hi = x1p.astype(BF16)
        lo = (x1p - hi.astype(F32)).astype(BF16)
        logits = (lax.dot_general(wh, hi, _NT, preferred_element_type=F32)
                  + lax.dot_general(wh, lo, _NT, preferred_element_type=F32)
                  + lax.dot_general(wrl_ref[...], hi, _NT, preferred_element_type=F32))
        e = jnp.exp(logits - jnp.max(logits, axis=0, keepdims=True))
        aff_ref[:, rows] = e / jnp.sum(e, axis=0, keepdims=True)


def _merge(na, gq, gates, x2d, wna, wgq, wout, ln_g, ln_b, wr_hi, wr_lo, *, alpha, tm):
    n, d = x2d.shape
    n_exp = wr_hi.shape[0]
    row = lambda i: (i, 0)
    const = lambda i: (0, 0)
    return pl.pallas_call(
        functools.partial(_merge_kernel, alpha=alpha, d_model=d),
        grid=(n // tm,),
        in_specs=[
            pl.BlockSpec((tm, NA_WIDTH), row), pl.BlockSpec((tm, GQA_WIDTH), row),
            pl.BlockSpec((tm, 2 * d), row), pl.BlockSpec((tm, d), row),
            pl.BlockSpec((NA_WIDTH, d), const), pl.BlockSpec((GQA_WIDTH, d), const),
            pl.BlockSpec((d, d), const), pl.BlockSpec((1, d), const), pl.BlockSpec((1, d), const),
            pl.BlockSpec((n_exp, d), const), pl.BlockSpec((n_exp, d), const),
        ],
        out_specs=[pl.BlockSpec((tm, d), row), pl.BlockSpec((n_exp, tm), lambda i: (0, i))],
        out_shape=[jax.ShapeDtypeStruct((n, d), F32), jax.ShapeDtypeStruct((n_exp, n), F32)],
        compiler_params=_params("parallel"),
        name="merge_ln_router",
    )(na, gq, gates, x2d, wna, wgq, wout, ln_g, ln_b, wr_hi, wr_lo)


def _route_kernel(aff_ref, pos_ref, base_ref, cnt_ref, *, cap, tile, count_chunk):
    n_exp, n = aff_ref.shape
    n_tiles = n // tile
    capf = float(cap)

    def bits_at(off, width):
        return lax.bitcast_convert_type(aff_ref[:, pl.ds(off, width)], I32)

    def count_ge(cand):
        def inner(c, acc):
            b = bits_at(pl.multiple_of(c * count_chunk, count_chunk), count_chunk)
            return acc + jnp.where(b >= cand, 1.0, 0.0)
        acc = lax.fori_loop(0, n // count_chunk, inner, jnp.zeros((n_exp, count_chunk), F32))
        return jnp.sum(acc, axis=1, keepdims=True)

    def bisect(i, prefix):
        cand = prefix | jnp.left_shift(jnp.int32(1), 30 - i)
        return jnp.where(count_ge(cand) >= capf, cand, prefix)

    thr = lax.fori_loop(0, 31, bisect, jnp.zeros((n_exp, 1), I32))
    need = capf - count_ge(thr + 1)

    ri = lax.broadcasted_iota(I32, (tile, tile), 0)
    ci = lax.broadcasted_iota(I32, (tile, tile), 1)
    upper = jnp.where(ri < ci, 1.0, 0.0).astype(BF16)
    ones = jnp.ones((tile, tile), BF16)
    tbl_lane = lax.broadcasted_iota(I32, base_ref.shape, 1)

    base_ref[...] = jnp.zeros(base_ref.shape, I32)
    cnt_ref[...] = jnp.zeros(cnt_ref.shape, I32)

    def body(i, carry):
        ceq, base = carry
        off = pl.multiple_of(i * tile, tile)
        b = lax.bitcast_convert_type(aff_ref[:, pl.ds(off, tile)], I32)
        eq = b == thr
        eqb = jnp.where(eq, 1.0, 0.0).astype(BF16)
        eqrank = jnp.dot(eqb, upper, preferred_element_type=F32) + ceq
        sel = (b > thr) | (eq & (eqrank < need))
        selb = jnp.where(sel, 1.0, 0.0).astype(BF16)
        rank_in_tile = jnp.dot(selb, upper, preferred_element_type=F32)
        cnt = jnp.dot(selb, ones, preferred_element_type=F32)
        pos_ref[:, pl.ds(off, tile)] = jnp.where(sel, base + rank_in_tile, float(UNPICKED)).astype(I32)
        base_ref[...] = jnp.where(tbl_lane == i, base[:, :1].astype(I32), base_ref[...])
        cnt_ref[...] = jnp.where(tbl_lane == i, cnt[:, :1].astype(I32), cnt_ref[...])
        padded = jnp.floor((cnt + (SUBLANES - 1.0)) * (1.0 / SUBLANES)) * SUBLANES
        return ceq + jnp.dot(eqb, ones, preferred_element_type=F32), base + padded

    zero = jnp.zeros((n_exp, tile), F32)
    _, used = lax.fori_loop(0, n_tiles, body, (zero, zero))
    base_ref[...] = jnp.where(tbl_lane == n_tiles, used[:, :1].astype(I32), base_ref[...])


def _route(aff, *, cap, tile):
    n_exp, n = aff.shape
    count_chunk = min(2048, n)
    assert n % count_chunk == 0 and n % tile == 0
    tbl_w = pl.cdiv(n // tile + 1, LANES) * LANES
    full = lambda shape: pl.BlockSpec(shape, lambda i: (0, 0))
    return pl.pallas_call(
        functools.partial(_route_kernel, cap=cap, tile=tile, count_chunk=count_chunk),
        grid=(1,),
        in_specs=[full((n_exp, n))],
        out_specs=[full((n_exp, n)), full((n_exp, tbl_w)), full((n_exp, tbl_w))],
        out_shape=[jax.ShapeDtypeStruct((n_exp, n), I32), jax.ShapeDtypeStruct((n_exp, tbl_w), I32),
                   jax.ShapeDtypeStruct((n_exp, tbl_w), I32)],
        compiler_params=_params("arbitrary"),
        name="route",
    )(aff)


def _pad_rows(c):
    return ((c + (SUBLANES - 1)) >> 3) << 3


def _start_pieces(rows, max_rows, make_copy):
    k = 3
    assert SUBLANES == 1 << k
    while (1 << k) <= max_rows:
        size = 1 << k

        @pl.when(((rows >> k) & 1) == 1)
        def _():
            make_copy(pl.multiple_of((rows >> (k + 1)) << (k + 1), SUBLANES), size).start()

        k += 1


def _wait_rows(hbm_ref, rows, sem):
    @pl.when(rows > 0)
    def _():
        view = hbm_ref.at[pl.ds(0, pl.multiple_of(rows, SUBLANES))]
        pltpu.make_async_copy(view, view, sem).wait()


def _dispatch_kernel(base_ref, cnt_ref, x_ref, aff_ref, pos_ref, xe_ref, few_sc, long_sc, few_sems, long_sem,
                     *, tile, cap_pad, n_tiles):
    i = pl.program_id(0)
    n_exp = pos_ref.shape[0]
    d = x_ref.shape[1]
    xb = x_ref[...].astype(BF16)
    a = aff_ref[...]
    hi = a.astype(BF16)
    rest1 = a - hi.astype(F32)
    mid = rest1.astype(BF16)
    lo = (rest1 - mid.astype(F32)).astype(BF16)
    pr = lax.broadcasted_iota(I32, (n_exp, LANES), 0)
    pc = lax.broadcasted_iota(I32, (n_exp, LANES), 1)
    gates = sum(jnp.dot(piece, jnp.where(pc == k * n_exp + pr, 1.0, 0.0).astype(BF16), preferred_element_type=F32)
                for k, piece in enumerate((hi, mid, lo))).astype(BF16)
    few = DISPATCH_FEW_ROWS
    par = i % 2

    def few_rows_total(j):
        total = jnp.int32(0)
        for e in range(n_exp):
            r = _pad_rows(cnt_ref[e, j])
            total = total + jnp.where(r <= few, r, 0)
        return total

    @pl.when(i >= 2)
    def _():
        _wait_rows(xe_ref, few_rows_total(jnp.maximum(i - 2, 0)), few_sems.at[par])

    bases = [base_ref[e, i] for e in range(n_exp)]
    padded = [_pad_rows(cnt_ref[e, i]) for e in range(n_exp)]

    def compact(e, n_rows):
        row_id = lax.broadcasted_iota(I32, (n_rows, tile), 0)
        return jnp.where(row_id == pos_ref[e:e + 1, :] - bases[e], 1.0, 0.0).astype(BF16)

    onehot = jnp.concatenate([compact(e, few) for e in range(n_exp)], axis=0)
    few_sc[par, :, 0:d] = jnp.dot(onehot, xb, preferred_element_type=F32)
    few_sc[par, :, d:d + LANES] = jnp.dot(onehot, gates, preferred_element_type=F32)

    long_total = jnp.int32(0)
    for e in range(n_exp):
        rows = padded[e]
        dst0 = e * cap_pad + bases[e]
        is_long = rows > few

        def few_copy(off, size):
            return pltpu.make_async_copy(few_sc.at[par, pl.ds(pl.multiple_of(e * few + off, SUBLANES), size)],
                                         xe_ref.at[pl.ds(pl.multiple_of(dst0 + off, SUBLANES), size)],
                                         few_sems.at[par])

        _start_pieces(jnp.where(is_long, 0, rows), few, few_copy)

        @pl.when(is_long)
        def _():
            full = compact(e, tile)
            long_sc[e, :, 0:d] = jnp.dot(full, xb, preferred_element_type=F32)
            long_sc[e, :, d:d + LANES] = jnp.dot(full, gates, preferred_element_type=F32)

            def long_copy(off, size):
                return pltpu.make_async_copy(long_sc.at[e, pl.ds(off, size)],
                                             xe_ref.at[pl.ds(pl.multiple_of(dst0 + off, SUBLANES), size)], long_sem)

            _start_pieces(rows, tile, long_copy)

        long_total = long_total + jnp.where(is_long, rows, 0)
    _wait_rows(xe_ref, long_total, long_sem)

    @pl.when(i == n_tiles - 1)
    def _():
        _wait_rows(xe_ref, few_rows_total(i), few_sems.at[par])

        @pl.when(i >= 1)
        def _():
            _wait_rows(xe_ref, few_rows_total(jnp.maximum(i - 1, 0)), few_sems.at[1 - par])

        stg = long_sc
        sems = few_sems
        stg[0] = jnp.zeros((tile, stg.shape[2]), F32)
        for e in range(n_exp):
            used = base_ref[e, n_tiles]
            tail = cap_pad - used
            dst0 = e * cap_pad + used

            def zero_copy(off, size):
                return pltpu.make_async_copy(stg.at[0, pl.ds(0, size)],
                                             xe_ref.at[pl.ds(pl.multiple_of(dst0 + off, SUBLANES), size)], sems.at[0])

            def whole(j, carry):
                cp = zero_copy(j * tile, tile)
                cp.start()
                cp.wait()
                return carry

            n_whole = tail // tile
            lax.fori_loop(0, n_whole, whole, 0)
            rest = tail - n_whole * tile
            _start_pieces(rest, tile, lambda off, size: zero_copy(n_whole * tile + off, size))
            _wait_rows(xe_ref, rest, sems.at[0])


def _dispatch(x1, aff_t, pos, base_tbl, cnt_tbl, *, cap_pad, tile):
    n, d = x1.shape
    n_exp = pos.shape[0]
    assert 3 * n_exp <= LANES
    n_tiles = n // tile
    return pl.pallas_call(
        functools.partial(_dispatch_kernel, tile=tile, cap_pad=cap_pad, n_tiles=n_tiles),
        grid_spec=pltpu.PrefetchScalarGridSpec(
            num_scalar_prefetch=2,
            grid=(n_tiles,),
            in_specs=[pl.BlockSpec((tile, d), lambda i, b, c: (i, 0)),
                      pl.BlockSpec((tile, n_exp), lambda i, b, c: (i, 0)),
                      pl.BlockSpec((n_exp, tile), lambda i, b, c: (0, i))],
            out_specs=pl.BlockSpec(memory_space=pl.ANY),
            scratch_shapes=[pltpu.VMEM((2, n_exp * DISPATCH_FEW_ROWS, d + LANES), F32),
                            pltpu.VMEM((n_exp, tile, d + LANES), F32),
                            pltpu.SemaphoreType.DMA((2,)), pltpu.SemaphoreType.DMA(())],
        ),
        out_shape=jax.ShapeDtypeStruct((n_exp * cap_pad, d + LANES), F32),
        compiler_params=_params("arbitrary"),
        name="dispatch",
    )(base_tbl, cnt_tbl, x1, aff_t, pos)


def _ffn_kernel(base_ref, xe_ref, wg_ref, wu_ref, wd_ref, ye_ref, *, f_chunk, n_tiles):
    tm = xe_ref.shape[0]
    d = ye_ref.shape[1]
    n_exp = pl.num_programs(0)
    e = pl.program_id(0)
    used = base_ref[e, n_tiles]
    live = pl.program_id(1) * tm < used

    @pl.when(live)
    def _():
        xb = xe_ref[:, 0:d].astype(BF16)
        lane = lax.broadcasted_iota(I32, (tm, LANES), 1)
        mine = (lane % n_exp == e) & (lane < 3 * n_exp)
        gate = jnp.sum(jnp.where(mine, xe_ref[:, d:d + LANES], 0.0), axis=1, keepdims=True)
        d_ff = wg_ref.shape[2]
        acc = jnp.zeros(ye_ref.shape, F32)
        for c in range(d_ff // f_chunk):
            cols = slice(c * f_chunk, (c + 1) * f_chunk)
            g = jnp.dot(xb, wg_ref[0, :, cols], preferred_element_type=F32)
            u = jnp.dot(xb, wu_ref[0, :, cols], preferred_element_type=F32)
            h = (g * jax.nn.sigmoid(g)) * u
            acc = acc + jnp.dot(h.astype(BF16), wd_ref[0, cols, :], preferred_element_type=F32)
        ye_ref[...] = acc * gate

    @pl.when(jnp.logical_not(live))
    def _():
        ye_ref[...] = jnp.zeros(ye_ref.shape, F32)


def _ffn(xe, base_tbl, wg, wu, wd, *, cap_pad, tm, f_chunk, n_tiles):
    n_slots, d_ext = xe.shape
    n_exp, d, d_ff = wg.shape
    assert cap_pad % tm == 0 and d_ff % f_chunk == 0 and d_ext == d + LANES
    kblocks = cap_pad // tm
    return pl.pallas_call(
        functools.partial(_ffn_kernel, f_chunk=f_chunk, n_tiles=n_tiles),
        grid_spec=pltpu.PrefetchScalarGridSpec(
            num_scalar_prefetch=1,
            grid=(n_exp, kblocks),
            in_specs=[
                pl.BlockSpec((tm, d_ext), lambda e, k, b: (e * kblocks + k, 0)),
                pl.BlockSpec((1, d, d_ff), lambda e, k, b: (e, 0, 0)),
                pl.BlockSpec((1, d, d_ff), lambda e, k, b: (e, 0, 0)),
                pl.BlockSpec((1, d_ff, d), lambda e, k, b: (e, 0, 0)),
            ],
            out_specs=pl.BlockSpec((tm, d), lambda e, k, b: (e * kblocks + k, 0)),
        ),
        out_shape=jax.ShapeDtypeStruct((n_slots, d), F32),
        compiler_params=_params("parallel", "parallel"),
        name="expert_ffn",
    )(base_tbl, xe, wg, wu, wd)


def _combine_kernel(base_ref, cnt_ref, x_ref, pos_ref, g_ref, b_ref, ye_ref, o_ref,
                    stg, hit_sc, sems, *, tile, cap_pad, n_tiles, alpha):
    i = pl.program_id(0)
    n_exp = pos_ref.shape[1]
    half = stg.shape[0] // 2

    @pl.when(i == 0)
    def _():
        stg[...] = jnp.zeros(stg.shape, F32)

    def layout(j):
        segs, off = [], jnp.int32(0)
        for e in range(n_exp):
            rows = _pad_rows(cnt_ref[e, j])
            segs.append((base_ref[e, j], rows, off))
            off = off + rows
        return segs, off

    def first_row(j, total):
        return jnp.where(total > half, 0, (j % 2) * half)

    def fetch(j, segs, row0):
        for e, (base, rows, off) in enumerate(segs):
            src0 = e * cap_pad + base
            dst0 = row0 + off

            def make_copy(o, size):
                return pltpu.make_async_copy(ye_ref.at[pl.ds(pl.multiple_of(src0 + o, SUBLANES), size)],
                                             stg.at[pl.ds(pl.multiple_of(dst0 + o, SUBLANES), size)], sems.at[j % 2])

            _start_pieces(rows, tile, make_copy)

    nxt = jnp.minimum(i + 1, n_tiles - 1)
    segs, total = layout(i)
    segs_next, total_next = layout(nxt)
    _, total_prev = layout(jnp.maximum(i - 1, 0))
    row0 = first_row(i, total)
    small, small_next, small_prev = total <= half, total_next <= half, total_prev <= half

    @pl.when(jnp.logical_not((i > 0) & small_prev & small))
    def _():
        fetch(i, segs, row0)

    _wait_rows(ye_ref, total, sems.at[i % 2])

    @pl.when((i + 1 < n_tiles) & small & small_next)
    def _():
        fetch(nxt, segs_next, first_row(nxt, total_next))

    o_ref[...] = alpha * x_ref[...]
    slot_id = lax.broadcasted_iota(I32, (tile, tile), 1)

    def block(b, carry):
        chunk = stg[pl.ds(pl.multiple_of(row0 + b * tile, tile), tile), :].astype(BF16)
        hit_sc[...] = jnp.zeros(hit_sc.shape, F32)
        for e, (base, rows, off) in enumerate(segs):
            @pl.when((off < (b + 1) * tile) & (off + rows > b * tile))
            def _():
                match = slot_id == (pos_ref[:, e:e + 1] - (base - off + b * tile))
                hit_sc[...] = jnp.where(match, 1.0, hit_sc[...])

        o_ref[...] += jnp.dot(hit_sc[...].astype(BF16), chunk, preferred_element_type=F32)
        return carry

    lax.fori_loop(0, (total + (tile - 1)) // tile, block, 0)
    o_ref[...] = _layer_norm(o_ref[...], g_ref[...], b_ref[...])


def _combine(x1, pos_t, base_tbl, cnt_tbl, ln_g, ln_b, ye, *, cap_pad, alpha, tile):
    n, d = x1.shape
    n_exp = pos_t.shape[1]
    row = lambda i, b, c: (i, 0)
    const = lambda i, b, c: (0, 0)
    stg_rows = n_exp * tile
    return pl.pallas_call(
        functools.partial(_combine_kernel, tile=tile, cap_pad=cap_pad, n_tiles=n // tile, alpha=alpha),
        grid_spec=pltpu.PrefetchScalarGridSpec(
            num_scalar_prefetch=2,
            grid=(n // tile,),
            in_specs=[
                pl.BlockSpec((tile, d), row), pl.BlockSpec((tile, n_exp), row),
                pl.BlockSpec((1, d), const), pl.BlockSpec((1, d), const),
                pl.BlockSpec(memory_space=pl.ANY),
            ],
            out_specs=pl.BlockSpec((tile, d), row),
            scratch_shapes=[pltpu.VMEM((stg_rows, d), F32), pltpu.VMEM((tile, tile), F32),
                            pltpu.SemaphoreType.DMA((2,))],
        ),
        out_shape=jax.ShapeDtypeStruct((n, d), F32),
        compiler_params=_params("arbitrary"),
        name="combine_ln",
    )(base_tbl, cnt_tbl, x1, pos_t, ln_g, ln_b, ye)


def _rope_tables(seq):
    t = jnp.arange(seq)
    row = (t // GRID_W).astype(F32)
    col = (t % GRID_W).astype(F32)
    half = HEAD_DIM // 2
    inv_freq = ROPE_THETA ** (-jnp.arange(0, half, 2, dtype=F32) / half)
    ang_r = row[:, None] * inv_freq[None, :]
    ang_c = col[:, None] * inv_freq[None, :]
    ang = jnp.concatenate([ang_r, ang_r, ang_c, ang_c], axis=-1)
    sign = jnp.where((jnp.arange(HEAD_DIM) % half) < half // 2, -1.0, 1.0).astype(F32)
    reps = LANES // HEAD_DIM
    return jnp.tile(jnp.cos(ang), (1, reps)), jnp.tile(jnp.sin(ang) * sign[None, :], (1, reps))


def _gqa_slot_columns():
    heads = [p + GQA_GROUP * half for p in range(HEAD_PAIRS) for half in range(2)]
    return np.concatenate([np.arange(HEAD_DIM) + HEAD_DIM * h for h in heads])


def _prep_layer(w_in, na_rpb, q_norm, k_norm, w_br_na, w_br_gqa, w_out, ln1_g, ln1_b,
                w_router, w_e_gate, w_e_up, w_e_down, ln2_g, ln2_b):
    d = w_in.shape[0]
    s_na = 3 * NA_WIDTH
    cols = _gqa_slot_columns()
    perm = np.concatenate([np.arange(s_na), s_na + cols, np.arange(s_na + GQA_WIDTH, w_in.shape[1])])
    reps = LANES // HEAD_DIM
    wr_t = w_router.T.astype(F32)
    wr_hi = wr_t.astype(BF16)
    gm = np.kron(np.eye(reps, dtype=np.float32), np.full((HEAD_DIM, HEAD_DIM), 1.0 / HEAD_DIM, np.float32))
    return dict(
        w_in=w_in[:, perm].astype(BF16),
        bias_tbl=_na_bias_table(na_rpb),
        qn=jnp.tile(q_norm.astype(F32), reps)[None, :], kn=jnp.tile(k_norm.astype(F32), reps)[None, :],
        gm=jnp.asarray(gm, BF16),
        wna=w_br_na.astype(BF16), wgq=w_br_gqa[cols].astype(BF16), wout=w_out.astype(BF16),
        ln1_g=ln1_g.astype(F32).reshape(1, d), ln1_b=ln1_b.astype(F32).reshape(1, d),
        wr_hi=wr_hi, wr_lo=(wr_t - wr_hi.astype(F32)).astype(BF16),
        wg=w_e_gate.astype(BF16), wu=w_e_up.astype(BF16), wd=w_e_down.astype(BF16),
        ln2_g=ln2_g.astype(F32).reshape(1, d), ln2_b=ln2_b.astype(F32).reshape(1, d),
    )


def _tiles(seq):
    return dict(tm_proj=512, tq=256, tk=1024, q_groups=2, tm_merge=512, t_moe=256, tm_ffn=512, f_chunk=512)


def _trunk_layer(x2d, p, *, seq, alpha, rope):
    n, d = x2d.shape
    n_exp = p["wg"].shape[0]
    cap = EC_CAPACITY * n // n_exp
    tl = _tiles(seq)
    cos, sin = rope
    naq, nak, nav, gq, gk, gv, gates = _inproj(x2d, p["w_in"], cos, sin, p["qn"], p["kn"], p["gm"],
                                               seq=seq, tm=min(tl["tm_proj"], seq))
    na = _na_attention(naq, nak, nav, p["bias_tbl"], seq=seq)
    nb = n // seq
    ones_rows = jnp.zeros((nb, VT_ROWS - KV_WIDTH, seq), BF16).at[:, 0, :].set(1.0)
    gvt = jnp.concatenate([gv.reshape(nb, seq, KV_WIDTH).transpose(0, 2, 1), ones_rows], axis=1)
    gvt = gvt.reshape(nb * VT_ROWS, seq)
    ga = _gqa_attention(gq, gk, gvt, seq=seq, tq=min(tl["tq"], seq), tk=min(tl["tk"], seq // 2),
                        q_groups=tl["q_groups"])
    x1, aff = _merge(na, ga, gates, x2d, p["wna"], p["wgq"], p["wout"], p["ln1_g"], p["ln1_b"],
                     p["wr_hi"], p["wr_lo"], alpha=alpha, tm=tl["tm_merge"])
    tile = tl["t_moe"]
    n_tiles = n // tile
    tm_ffn = min(tl["tm_ffn"], cap)
    cap_pad = pl.cdiv(cap + (SUBLANES - 1) * n_tiles, tm_ffn) * tm_ffn
    pos, base_tbl, cnt_tbl = _route(aff, cap=cap, tile=tile)
    xe = _dispatch(x1, aff.T, pos, base_tbl, cnt_tbl, cap_pad=cap_pad, tile=tile)
    ye = _ffn(xe, base_tbl, p["wg"], p["wu"], p["wd"], cap_pad=cap_pad, tm=tm_ffn, f_chunk=tl["f_chunk"],
              n_tiles=n_tiles)
    return _combine(x1, pos.T, base_tbl, cnt_tbl, p["ln2_g"], p["ln2_b"], ye,
                    cap_pad=cap_pad, alpha=alpha, tile=tile)


@jax.jit
def kernel(x_prompt, x_sample, w_in, na_rpb, q_norm, k_norm, w_br_na, w_br_gqa, w_out, ln1_g, ln1_b,
           w_router, w_e_gate, w_e_up, w_e_down, ln2_g, ln2_b):
    depth = w_in.shape[0]
    alpha = float((2 * depth) ** 0.25)
    d = x_prompt.shape[-1]
    groups = []
    for x in (x_prompt, x_sample):
        b, s, _ = x.shape
        groups.append(dict(x=x.reshape(b * s, d), shape=x.shape, seq=s, rope=_rope_tables(s)))
    for l in range(depth):
        p = _prep_layer(w_in[l], na_rpb[l], q_norm[l], k_norm[l], w_br_na[l], w_br_gqa[l], w_out[l],
                        ln1_g[l], ln1_b[l], w_router[l], w_e_gate[l], w_e_up[l], w_e_down[l],
                        ln2_g[l], ln2_b[l])
        for g in groups:
            g["x"] = _trunk_layer(g["x"], p, seq=g["seq"], alpha=alpha, rope=g["rope"])
    return tuple(g["x"].reshape(g["shape"]) for g in groups)
```

```python
import functools

import jax
import jax.numpy as jnp
import numpy as np
from jax import lax
from jax.experimental import pallas as pl
from jax.experimental.pallas import tpu as pltpu

F32 = jnp.float32
BF16 = jnp.bfloat16
I32 = jnp.int32

GRID_W = 64
HEAD_DIM = 64
NA_HEADS = 8
NA_WIN_H = 8
NA_WIN_W = 16
GQA_Q_HEADS = 8
GQA_KV_HEADS = 2
GQA_GROUP = GQA_Q_HEADS // GQA_KV_HEADS
ROPE_THETA = 10000.0
EC_CAPACITY = 2
LN_EPS = 1e-5
RMS_EPS = 1e-6

LANES = 128
SUBLANES = 8
NA_WIDTH = NA_HEADS * HEAD_DIM
GQA_WIDTH = GQA_Q_HEADS * HEAD_DIM
KV_WIDTH = GQA_KV_HEADS * HEAD_DIM
HEAD_PAIRS = GQA_WIDTH // LANES
BF16_TILE_ROWS = 2 * SUBLANES
VT_ROWS = KV_WIDTH + BF16_TILE_ROWS
UNPICKED = -(1 << 30)
NA_ROWS_PER_ITER = 4
MERGE_PART_ROWS = 256
DISPATCH_FEW_ROWS = 64
GQA_Q_SCALE = HEAD_DIM ** -0.5 * float(np.log2(np.e))
NEG_BIG = -1e30
VMEM_LIMIT_BYTES = 56 * 1024 * 1024

_NT = (((1,), (1,)), ((), ()))


def _params(*sem):
    return pltpu.CompilerParams(dimension_semantics=sem, vmem_limit_bytes=VMEM_LIMIT_BYTES)


def _inproj_kernel(x_ref, w_ref, cos_ref, sin_ref, qn_ref, kn_ref, gm_ref,
                   naq_ref, nak_ref, nav_ref, gq_ref, gk_ref, gv_ref, gate_ref, *, d_model):
    xb = x_ref[...].astype(BF16)
    s_na = 3 * NA_WIDTH
    s_gq = s_na + GQA_WIDTH
    s_gk = s_gq + KV_WIDTH
    s_gv = s_gk + KV_WIDTH

    def proj(c0, width):
        return jnp.dot(xb, w_ref[:, c0:c0 + width], preferred_element_type=F32)

    naq_ref[...] = (proj(0, NA_WIDTH) * HEAD_DIM ** -0.5).astype(BF16)
    nak_ref[...] = proj(NA_WIDTH, NA_WIDTH).astype(BF16)
    nav_ref[...] = proj(2 * NA_WIDTH, NA_WIDTH).astype(BF16)

    cos = cos_ref[...]
    sin = sin_ref[...]
    gm = gm_ref[...]
    lane = lax.broadcasted_iota(I32, cos.shape, 1)
    first_half = (lane % (HEAD_DIM // 2)) < (HEAD_DIM // 4)

    def norm_rope(a, gain):
        sq = a * a
        hi = sq.astype(BF16)
        lo = (sq - hi.astype(F32)).astype(BF16)
        ms = jnp.dot(hi, gm, preferred_element_type=F32) + jnp.dot(lo, gm, preferred_element_type=F32)
        an = a * lax.rsqrt(ms + RMS_EPS) * gain
        quarter = HEAD_DIM // 4
        rot = jnp.where(first_half, pltpu.roll(an, LANES - quarter, 1), pltpu.roll(an, quarter, 1))
        return an * cos + rot * sin

    gq_raw = [proj(s_na + p * LANES, LANES) for p in range(HEAD_PAIRS)]
    gk_raw = proj(s_gq, KV_WIDTH)
    gv_ref[...] = proj(s_gk, KV_WIDTH).astype(BF16)
    gate_chunk = 512
    for c in range(2 * d_model // gate_chunk):
        g = proj(s_gv + c * gate_chunk, gate_chunk)
        gate_ref[:, c * gate_chunk:(c + 1) * gate_chunk] = jax.nn.sigmoid(g)
    qn = qn_ref[...]
    for p in range(HEAD_PAIRS):
        gq_ref[:, p * LANES:(p + 1) * LANES] = (norm_rope(gq_raw[p], qn) * GQA_Q_SCALE).astype(BF16)
    gk_ref[...] = norm_rope(gk_raw, kn_ref[...]).astype(BF16)


def _inproj(x2d, w_in_b, cos, sin, qn, kn, gm, *, seq, tm):
    n, d = x2d.shape
    d_in = w_in_b.shape[1]
    assert n % tm == 0 and seq % tm == 0
    sblocks = seq // tm
    row = lambda i: (i, 0)
    const = lambda i: (0, 0)
    pos = lambda i: (i % sblocks, 0)
    out_shape = [
        jax.ShapeDtypeStruct((n, NA_WIDTH), BF16), jax.ShapeDtypeStruct((n, NA_WIDTH), BF16),
        jax.ShapeDtypeStruct((n, NA_WIDTH), BF16), jax.ShapeDtypeStruct((n, GQA_WIDTH), BF16),
        jax.ShapeDtypeStruct((n, KV_WIDTH), BF16), jax.ShapeDtypeStruct((n, KV_WIDTH), BF16),
        jax.ShapeDtypeStruct((n, 2 * d), F32),
    ]
    out_specs = [
        pl.BlockSpec((tm, NA_WIDTH), row), pl.BlockSpec((tm, NA_WIDTH), row), pl.BlockSpec((tm, NA_WIDTH), row),
        pl.BlockSpec((tm, GQA_WIDTH), row), pl.BlockSpec((tm, KV_WIDTH), row), pl.BlockSpec((tm, KV_WIDTH), row),
        pl.BlockSpec((tm, 2 * d), row),
    ]
    return pl.pallas_call(
        functools.partial(_inproj_kernel, d_model=d),
        grid=(n // tm,),
        in_specs=[
            pl.BlockSpec((tm, d), row), pl.BlockSpec((d, d_in), const),
            pl.BlockSpec((tm, LANES), pos), pl.BlockSpec((tm, LANES), pos),
            pl.BlockSpec((1, LANES), const), pl.BlockSpec((1, LANES), const),
            pl.BlockSpec((LANES, LANES), const),
        ],
        out_specs=out_specs,
        out_shape=out_shape,
        compiler_params=_params("parallel"),
        name="inproj",
    )(x2d, w_in_b, cos, sin, qn, kn, gm)


def _na_kernel(q_ref, kp_ref, kc_ref, kn_ref, vp_ref, vc_ref, vn_ref, bias_ref, o_ref,
               kbuf, vbuf, *, rows, nrb):
    blk = NA_WIN_H * GRID_W
    j = pl.program_id(0) % nrb
    kbuf[0:blk] = kp_ref[...]
    kbuf[blk:2 * blk] = kc_ref[...]
    kbuf[2 * blk:3 * blk] = kn_ref[...]
    vbuf[0:blk] = vp_ref[...]
    vbuf[blk:2 * blk] = vc_ref[...]
    vbuf[2 * blk:3 * blk] = vn_ref[...]
    lane = lax.broadcasted_iota(I32, (GRID_W, LANES), 1)
    low = lane < HEAD_DIM

    def rows_body(it, carry):
        items = []
        for u in range(NA_ROWS_PER_ITER):
            i = it * NA_ROWS_PER_ITER + u
            r = j * NA_WIN_H + i
            rs = jnp.clip(r - NA_WIN_H // 2, 0, rows - NA_WIN_H)
            d0 = rs - r + (NA_WIN_H - 1)
            off = pl.multiple_of((rs - (j - 1) * NA_WIN_H) * GRID_W, GRID_W)
            qoff = pl.multiple_of(i * GRID_W, GRID_W)
            for p in range(NA_WIDTH // LANES):
                items.append((d0, off, qoff, p, slice(p * LANES, (p + 1) * LANES)))
        scores = []
        for d0, off, qoff, p, cols in items:
            q2 = q_ref[pl.ds(qoff, GRID_W), cols]
            zero = jnp.zeros_like(q2)
            qs = jnp.concatenate([jnp.where(low, q2, zero), jnp.where(low, zero, q2)], axis=0)
            k2 = kbuf[pl.ds(off, blk), cols]
            scores.append(lax.dot_general(k2, qs, _NT, preferred_element_type=F32) + bias_ref[d0, p])
        probs = []
        for st in scores:
            e = jnp.exp(st - jnp.max(st, axis=0, keepdims=True))
            probs.append((e * (1.0 / jnp.sum(e, axis=0, keepdims=True))).T.astype(BF16))
        for (d0, off, qoff, p, cols), prob in zip(items, probs):
            o = jnp.dot(prob, vbuf[pl.ds(off, blk), cols], preferred_element_type=F32)
            o_ref[pl.ds(qoff, GRID_W), cols] = jnp.where(low, o[:GRID_W], o[GRID_W:]).astype(BF16)
        return carry

    lax.fori_loop(0, NA_WIN_H // NA_ROWS_PER_ITER, rows_body, 0)


def _na_bias_table(rpb):
    c = jnp.arange(GRID_W)
    cs = jnp.clip(c - NA_WIN_W // 2, 0, GRID_W - NA_WIN_W)
    cc = jnp.arange(GRID_W)
    inwin = (cc[None, :] >= cs[:, None]) & (cc[None, :] < cs[:, None] + NA_WIN_W)
    dc = jnp.clip(cc[None, :] - c[:, None] + (NA_WIN_W - 1), 0, 2 * NA_WIN_W - 2)
    full = jnp.where(inwin[None, None], rpb[:, :, dc].astype(F32), NEG_BIG)
    tbl = jnp.stack([full[:, d0:d0 + NA_WIN_H] for d0 in range(NA_WIN_H)])
    tbl = tbl.reshape(NA_WIN_H, NA_HEADS // 2, 2, NA_WIN_H, GRID_W, GRID_W)
    return tbl.transpose(0, 1, 3, 5, 2, 4).reshape(NA_WIN_H, NA_HEADS // 2, NA_WIN_H * GRID_W, 2 * GRID_W)


def _na_attention(q, k, v, bias_tbl, *, seq):
    n = q.shape[0]
    rows = seq // GRID_W
    assert rows % NA_WIN_H == 0 and rows >= 2 * NA_WIN_H
    nrb = rows // NA_WIN_H
    blk = NA_WIN_H * GRID_W

    def cur(g):
        return (g, 0)

    def prev(g):
        return (g - jnp.where(g % nrb == 0, 0, 1), 0)

    def nxt(g):
        return (g + jnp.where(g % nrb == nrb - 1, 0, 1), 0)

    spec = lambda f: pl.BlockSpec((blk, NA_WIDTH), f)
    return pl.pallas_call(
        functools.partial(_na_kernel, rows=rows, nrb=nrb),
        grid=(n // blk,),
        in_specs=[spec(cur), spec(prev), spec(cur), spec(nxt), spec(prev), spec(cur), spec(nxt),
                  pl.BlockSpec(bias_tbl.shape, lambda g: (0, 0, 0, 0))],
        out_specs=spec(cur),
        out_shape=jax.ShapeDtypeStruct((n, NA_WIDTH), BF16),
        scratch_shapes=[pltpu.VMEM((3 * blk, NA_WIDTH), BF16), pltpu.VMEM((3 * blk, NA_WIDTH), BF16)],
        compiler_params=_params("parallel"),
        name="na_attention",
    )(q, k, k, k, v, v, v, bias_tbl)


def _gqa_kernel(q_ref, k_ref, vt_ref, o_ref, qs_sc, sa_sc, sb_sc, m_sc, acc_sc, *, tq, tk, seq):
    q_groups = qs_sc.shape[0]
    lane = lax.broadcasted_iota(I32, (tq, LANES), 1)
    low = lane < HEAD_DIM
    for g in range(q_groups):
        for p in range(HEAD_PAIRS):
            q2 = q_ref[g * tq:(g + 1) * tq, p * LANES:(p + 1) * LANES]
            zero = jnp.zeros_like(q2)
            qs_sc[g, (2 * p) * tq:(2 * p + 1) * tq, :] = jnp.where(low, q2, zero)
            qs_sc[g, (2 * p + 1) * tq:(2 * p + 2) * tq, :] = jnp.where(low, zero, q2)
    m_sc[...] = jnp.full(m_sc.shape, -jnp.inf, F32)
    acc_sc[...] = jnp.zeros(acc_sc.shape, F32)
    n_chunks = seq // tk
    n_items = q_groups * n_chunks
    assert n_chunks % 2 == 0

    def scores(w, st_ref):
        koff = pl.multiple_of((w % n_chunks) * tk, tk)
        st_ref[...] = lax.dot_general(k_ref[pl.ds(koff, tk), :], qs_sc[w // n_chunks], _NT,
                                      preferred_element_type=F32)

    def consume(w, st_ref):
        g = w // n_chunks
        koff = pl.multiple_of((w % n_chunks) * tk, tk)
        st = st_ref[...]
        m_old = m_sc[g]
        m_new = jnp.maximum(m_old, jnp.max(st, axis=0, keepdims=True))
        alpha = jnp.exp2(m_old - m_new)
        e = jnp.exp2(st - m_new).astype(BF16)
        pv = jnp.dot(vt_ref[:, pl.ds(koff, tk)], e, preferred_element_type=F32)
        acc_sc[g] = alpha * acc_sc[g] + pv
        m_sc[g] = m_new

    scores(0, sa_sc)

    def body(j, carry):
        scores(2 * j + 1, sb_sc)
        consume(2 * j, sa_sc)
        scores(2 * j + 2, sa_sc)
        consume(2 * j + 1, sb_sc)
        return carry

    lax.fori_loop(0, n_items // 2 - 1, body, 0)
    scores(n_items - 1, sb_sc)
    consume(n_items - 2, sa_sc)
    consume(n_items - 1, sb_sc)
    top = lax.broadcasted_iota(I32, (KV_WIDTH, tq), 0) < HEAD_DIM
    for g in range(q_groups):
        o_t = acc_sc[g, 0:KV_WIDTH, :] * (1.0 / acc_sc[g, KV_WIDTH:KV_WIDTH + 1, :])
        for p in range(HEAD_PAIRS):
            a = o_t[:, (2 * p) * tq:(2 * p + 1) * tq]
            b = o_t[:, (2 * p + 1) * tq:(2 * p + 2) * tq]
            o_ref[g * tq:(g + 1) * tq, p * LANES:(p + 1) * LANES] = jnp.where(top, a, b).T.astype(BF16)


def _gqa_attention(q, k, vt, *, seq, tq, tk, q_groups):
    n = q.shape[0]
    rows = q_groups * tq
    assert seq % rows == 0 and seq % tk == 0
    qblocks = seq // rows
    width = GQA_Q_HEADS * tq
    return pl.pallas_call(
        functools.partial(_gqa_kernel, tq=tq, tk=tk, seq=seq),
        grid=(n // seq, qblocks),
        in_specs=[
            pl.BlockSpec((rows, GQA_WIDTH), lambda b, i: (b * qblocks + i, 0)),
            pl.BlockSpec((seq, KV_WIDTH), lambda b, i: (b, 0)),
            pl.BlockSpec((VT_ROWS, seq), lambda b, i: (b, 0)),
        ],
        out_specs=pl.BlockSpec((rows, GQA_WIDTH), lambda b, i: (b * qblocks + i, 0)),
        out_shape=jax.ShapeDtypeStruct((n, GQA_WIDTH), BF16),
        scratch_shapes=[pltpu.VMEM((q_groups, width, KV_WIDTH), BF16), pltpu.VMEM((tk, width), F32),
                        pltpu.VMEM((tk, width), F32), pltpu.VMEM((q_groups, 1, width), F32),
                        pltpu.VMEM((q_groups, VT_ROWS, width), F32)],
        compiler_params=_params("parallel", "parallel"),
        name="gqa_attention",
    )(q, k, vt)


def _layer_norm(h, g, b):
    mu = jnp.mean(h, axis=-1, keepdims=True)
    hc = h - mu
    var = jnp.mean(hc * hc, axis=-1, keepdims=True)
    return hc * lax.rsqrt(var + LN_EPS) * g + b


def _merge_kernel(na_ref, gq_ref, gate_ref, x_ref, wna_ref, wgq_ref, wout_ref, g_ref, b_ref,
                  wrh_ref, wrl_ref, x1_ref, aff_ref, *, alpha, d_model):
    tm = x_ref.shape[0]
    parts = [slice(r, r + MERGE_PART_ROWS) for r in range(0, tm, MERGE_PART_ROWS)]
    y_na = [jnp.dot(na_ref[rows, :], wna_ref[...], preferred_element_type=F32) for rows in parts]
    y_gq = [jnp.dot(gq_ref[rows, :], wgq_ref[...], preferred_element_type=F32) for rows in parts]
    mixin = [(gate_ref[rows, :d_model] * a + gate_ref[rows, d_model:] * b).astype(BF16)
             for rows, a, b in zip(parts, y_na, y_gq)]
    mix = [jnp.dot(m, wout_ref[...], preferred_element_type=F32) for m in mixin]
    x1 = [_layer_norm(alpha * x_ref[rows, :] + m, g_ref[...], b_ref[...]) for rows, m in zip(parts, mix)]
    wh = wrh_ref[...]
    for rows, x1p in zip(parts, x1):
        x1_ref[rows, :] = x1p
        hi = x1p.astype(BF16)
        lo = (x1p - hi.astype(F32)).astype(BF16)
        logits = (lax.dot_general(wh, hi, _NT, preferred_element_type=F32)
                  + lax.dot_general(wh, lo, _NT, preferred_element_type=F32)
                  + lax.dot_general(wrl_ref[...], hi, _NT, preferred_element_type=F32))
        e = jnp.exp(logits - jnp.max(logits, axis=0, keepdims=True))
        aff_ref[:, rows] = e / jnp.sum(e, axis=0, keepdims=True)


def _merge(na, gq, gates, x2d, wna, wgq, wout, ln_g, ln_b, wr_hi, wr_lo, *, alpha, tm):
    n, d = x2d.shape
    n_exp = wr_hi.shape[0]
    row = lambda i: (i, 0)
    const = lambda i: (0, 0)
    return pl.pallas_call(
        functools.partial(_merge_kernel, alpha=alpha, d_model=d),
        grid=(n // tm,),
        in_specs=[
            pl.BlockSpec((tm, NA_WIDTH), row), pl.BlockSpec((tm, GQA_WIDTH), row),
            pl.BlockSpec((tm, 2 * d), row), pl.BlockSpec((tm, d), row),
            pl.BlockSpec((NA_WIDTH, d), const), pl.BlockSpec((GQA_WIDTH, d), const),
            pl.BlockSpec((d, d), const), pl.BlockSpec((1, d), const), pl.BlockSpec((1, d), const),
            pl.BlockSpec((n_exp, d), const), pl.BlockSpec((n_exp, d), const),
        ],
        out_specs=[pl.BlockSpec((tm, d), row), pl.BlockSpec((n_exp, tm), lambda i: (0, i))],
        out_shape=[jax.ShapeDtypeStruct((n, d), F32), jax.ShapeDtypeStruct((n_exp, n), F32)],
        compiler_params=_params("parallel"),
        name="merge_ln_router",
    )(na, gq, gates, x2d, wna, wgq, wout, ln_g, ln_b, wr_hi, wr_lo)


def _route_kernel(aff_ref, pos_ref, base_ref, cnt_ref, *, cap, tile, count_chunk):
    n_exp, n = aff_ref.shape
    n_tiles = n // tile
    capf = float(cap)

    def bits_at(off, width):
        return lax.bitcast_convert_type(aff_ref[:, pl.ds(off, width)], I32)

    def count_ge(cand):
        def inner(c, acc):
            b = bits_at(pl.multiple_of(c * count_chunk, count_chunk), count_chunk)
            return acc + jnp.where(b >= cand, 1.0, 0.0)
        acc = lax.fori_loop(0, n // count_chunk, inner, jnp.zeros((n_exp, count_chunk), F32))
        return jnp.sum(acc, axis=1, keepdims=True)

    def bisect(i, prefix):
        cand = prefix | jnp.left_shift(jnp.int32(1), 30 - i)
        return jnp.where(count_ge(cand) >= capf, cand, prefix)

    thr = lax.fori_loop(0, 31, bisect, jnp.zeros((n_exp, 1), I32))
    need = capf - count_ge(thr + 1)

    ri = lax.broadcasted_iota(I32, (tile, tile), 0)
    ci = lax.broadcasted_iota(I32, (tile, tile), 1)
    upper = jnp.where(ri < ci, 1.0, 0.0).astype(BF16)
    ones = jnp.ones((tile, tile), BF16)
    tbl_lane = lax.broadcasted_iota(I32, base_ref.shape, 1)

    base_ref[...] = jnp.zeros(base_ref.shape, I32)
    cnt_ref[...] = jnp.zeros(cnt_ref.shape, I32)

    def body(i, carry):
        ceq, base = carry
        off = pl.multiple_of(i * tile, tile)
        b = lax.bitcast_convert_type(aff_ref[:, pl.ds(off, tile)], I32)
        eq = b == thr
        eqb = jnp.where(eq, 1.0, 0.0).astype(BF16)
        eqrank = jnp.dot(eqb, upper, preferred_element_type=F32) + ceq
        sel = (b > thr) | (eq & (eqrank < need))
        selb = jnp.where(sel, 1.0, 0.0).astype(BF16)
        rank_in_tile = jnp.dot(selb, upper, preferred_element_type=F32)
        cnt = jnp.dot(selb, ones, preferred_element_type=F32)
        pos_ref[:, pl.ds(off, tile)] = jnp.where(sel, base + rank_in_tile, float(UNPICKED)).astype(I32)
        base_ref[...] = jnp.where(tbl_lane == i, base[:, :1].astype(I32), base_ref[...])
        cnt_ref[...] = jnp.where(tbl_lane == i, cnt[:, :1].astype(I32), cnt_ref[...])
        padded = jnp.floor((cnt + (SUBLANES - 1.0)) * (1.0 / SUBLANES)) * SUBLANES
        return ceq + jnp.dot(eqb, ones, preferred_element_type=F32), base + padded

    zero = jnp.zeros((n_exp, tile), F32)
    _, used = lax.fori_loop(0, n_tiles, body, (zero, zero))
    base_ref[...] = jnp.where(tbl_lane == n_tiles, used[:, :1].astype(I32), base_ref[...])


def _route(aff, *, cap, tile):
    n_exp, n = aff.shape
    count_chunk = min(2048, n)
    assert n % count_chunk == 0 and n % tile == 0
    tbl_w = pl.cdiv(n // tile + 1, LANES) * LANES
    full = lambda shape: pl.BlockSpec(shape, lambda i: (0, 0))
    return pl.pallas_call(
        functools.partial(_route_kernel, cap=cap, tile=tile, count_chunk=count_chunk),
        grid=(1,),
        in_specs=[full((n_exp, n))],
        out_specs=[full((n_exp, n)), full((n_exp, tbl_w)), full((n_exp, tbl_w))],
        out_shape=[jax.ShapeDtypeStruct((n_exp, n), I32), jax.ShapeDtypeStruct((n_exp, tbl_w), I32),
                   jax.ShapeDtypeStruct((n_exp, tbl_w), I32)],
        compiler_params=_params("arbitrary"),
        name="route",
    )(aff)


def _pad_rows(c):
    return ((c + (SUBLANES - 1)) >> 3) << 3


def _start_pieces(rows, max_rows, make_copy):
    k = 3
    assert SUBLANES == 1 << k
    while (1 << k) <= max_rows:
        size = 1 << k

        @pl.when(((rows >> k) & 1) == 1)
        def _():
            make_copy(pl.multiple_of((rows >> (k + 1)) << (k + 1), SUBLANES), size).start()

        k += 1


def _wait_rows(hbm_ref, rows, sem):
    @pl.when(rows > 0)
    def _():
        view = hbm_ref.at[pl.ds(0, pl.multiple_of(rows, SUBLANES))]
        pltpu.make_async_copy(view, view, sem).wait()


def _dispatch_kernel(base_ref, cnt_ref, x_ref, aff_ref, pos_ref, xe_ref, few_sc, long_sc, few_sems, long_sem,
                     *, tile, cap_pad, n_tiles):
    i = pl.program_id(0)
    n_exp = pos_ref.shape[0]
    d = x_ref.shape[1]
    xb = x_ref[...].astype(BF16)
    a = aff_ref[...]
    hi = a.astype(BF16)
    rest1 = a - hi.astype(F32)
    mid = rest1.astype(BF16)
    lo = (rest1 - mid.astype(F32)).astype(BF16)
    pr = lax.broadcasted_iota(I32, (n_exp, LANES), 0)
    pc = lax.broadcasted_iota(I32, (n_exp, LANES), 1)
    gates = sum(jnp.dot(piece, jnp.where(pc == k * n_exp + pr, 1.0, 0.0).astype(BF16), preferred_element_type=F32)
                for k, piece in enumerate((hi, mid, lo))).astype(BF16)
    few = DISPATCH_FEW_ROWS
    par = i % 2

    def few_rows_total(j):
        total = jnp.int32(0)
        for e in range(n_exp):
            r = _pad_rows(cnt_ref[e, j])
            total = total + jnp.where(r <= few, r, 0)
        return total

    @pl.when(i >= 2)
    def _():
        _wait_rows(xe_ref, few_rows_total(jnp.maximum(i - 2, 0)), few_sems.at[par])

    bases = [base_ref[e, i] for e in range(n_exp)]
    padded = [_pad_rows(cnt_ref[e, i]) for e in range(n_exp)]

    def compact(e, n_rows):
        row_id = lax.broadcasted_iota(I32, (n_rows, tile), 0)
        return jnp.where(row_id == pos_ref[e:e + 1, :] - bases[e], 1.0, 0.0).astype(BF16)

    onehot = jnp.concatenate([compact(e, few) for e in range(n_exp)], axis=0)
    few_sc[par, :, 0:d] = jnp.dot(onehot, xb, preferred_element_type=F32)
    few_sc[par, :, d:d + LANES] = jnp.dot(onehot, gates, preferred_element_type=F32)

    long_total = jnp.int32(0)
    for e in range(n_exp):
        rows = padded[e]
        dst0 = e * cap_pad + bases[e]
        is_long = rows > few

        def few_copy(off, size):
            return pltpu.make_async_copy(few_sc.at[par, pl.ds(pl.multiple_of(e * few + off, SUBLANES), size)],
                                         xe_ref.at[pl.ds(pl.multiple_of(dst0 + off, SUBLANES), size)],
                                         few_sems.at[par])

        _start_pieces(jnp.where(is_long, 0, rows), few, few_copy)

        @pl.when(is_long)
        def _():
            full = compact(e, tile)
            long_sc[e, :, 0:d] = jnp.dot(full, xb, preferred_element_type=F32)
            long_sc[e, :, d:d + LANES] = jnp.dot(full, gates, preferred_element_type=F32)

            def long_copy(off, size):
                return pltpu.make_async_copy(long_sc.at[e, pl.ds(off, size)],
                                             xe_ref.at[pl.ds(pl.multiple_of(dst0 + off, SUBLANES), size)], long_sem)

            _start_pieces(rows, tile, long_copy)

        long_total = long_total + jnp.where(is_long, rows, 0)
    _wait_rows(xe_ref, long_total, long_sem)

    @pl.when(i == n_tiles - 1)
    def _():
        _wait_rows(xe_ref, few_rows_total(i), few_sems.at[par])

        @pl.when(i >= 1)
        def _():
            _wait_rows(xe_ref, few_rows_total(jnp.maximum(i - 1, 0)), few_sems.at[1 - par])

        stg = long_sc
        sems = few_sems
        stg[0] = jnp.zeros((tile, stg.shape[2]), F32)
        for e in range(n_exp):
            used = base_ref[e, n_tiles]
            tail = cap_pad - used
            dst0 = e * cap_pad + used

            def zero_copy(off, size):
                return pltpu.make_async_copy(stg.at[0, pl.ds(0, size)],
                                             xe_ref.at[pl.ds(pl.multiple_of(dst0 + off, SUBLANES), size)], sems.at[0])

            def whole(j, carry):
                cp = zero_copy(j * tile, tile)
                cp.start()
                cp.wait()
                return carry

            n_whole = tail // tile
            lax.fori_loop(0, n_whole, whole, 0)
            rest = tail - n_whole * tile
            _start_pieces(rest, tile, lambda off, size: zero_copy(n_whole * tile + off, size))
            _wait_rows(xe_ref, rest, sems.at[0])


def _dispatch(x1, aff_t, pos, base_tbl, cnt_tbl, *, cap_pad, tile):
    n, d = x1.shape
    n_exp = pos.shape[0]
    assert 3 * n_exp <= LANES
    n_tiles = n // tile
    return pl.pallas_call(
        functools.partial(_dispatch_kernel, tile=tile, cap_pad=cap_pad, n_tiles=n_tiles),
        grid_spec=pltpu.PrefetchScalarGridSpec(
            num_scalar_prefetch=2,
            grid=(n_tiles,),
            in_specs=[pl.BlockSpec((tile, d), lambda i, b, c: (i, 0)),
                      pl.BlockSpec((tile, n_exp), lambda i, b, c: (i, 0)),
                      pl.BlockSpec((n_exp, tile), lambda i, b, c: (0, i))],
            out_specs=pl.BlockSpec(memory_space=pl.ANY),
            scratch_shapes=[pltpu.VMEM((2, n_exp * DISPATCH_FEW_ROWS, d + LANES), F32),
                            pltpu.VMEM((n_exp, tile, d + LANES), F32),
                            pltpu.SemaphoreType.DMA((2,)), pltpu.SemaphoreType.DMA(())],
        ),
        out_shape=jax.ShapeDtypeStruct((n_exp * cap_pad, d + LANES), F32),
        compiler_params=_params("arbitrary"),
        name="dispatch",
    )(base_tbl, cnt_tbl, x1, aff_t, pos)


def _ffn_kernel(base_ref, xe_ref, wg_ref, wu_ref, wd_ref, ye_ref, *, f_chunk, n_tiles):
    tm = xe_ref.shape[0]
    d = ye_ref.shape[1]
    n_exp = pl.num_programs(0)
    e = pl.program_id(0)
    used = base_ref[e, n_tiles]
    live = pl.program_id(1) * tm < used

    @pl.when(live)
    def _():
        xb = xe_ref[:, 0:d].astype(BF16)
        lane = lax.broadcasted_iota(I32, (tm, LANES), 1)
        mine = (lane % n_exp == e) & (lane < 3 * n_exp)
        gate = jnp.sum(jnp.where(mine, xe_ref[:, d:d + LANES], 0.0), axis=1, keepdims=True)
        n_chunks = wg_ref.shape[2] // f_chunk

        def gate_up(c):
            cols = slice(c * f_chunk, (c + 1) * f_chunk)
            return (jnp.dot(xb, wg_ref[0, :, cols], preferred_element_type=F32),
                    jnp.dot(xb, wu_ref[0, :, cols], preferred_element_type=F32))

        acc = jnp.zeros(ye_ref.shape, F32)
        g, u = gate_up(0)
        for c in range(n_chunks):
            h = ((g * jax.nn.sigmoid(g)) * u).astype(BF16)
            if c + 1 < n_chunks:
                g, u = gate_up(c + 1)
            acc = acc + jnp.dot(h, wd_ref[0, c * f_chunk:(c + 1) * f_chunk, :], preferred_element_type=F32)
        ye_ref[...] = acc * gate

    @pl.when(jnp.logical_not(live))
    def _():
        ye_ref[...] = jnp.zeros(ye_ref.shape, F32)


def _ffn(xe, base_tbl, wg, wu, wd, *, cap_pad, tm, f_chunk, n_tiles):
    n_slots, d_ext = xe.shape
    n_exp, d, d_ff = wg.shape
    assert cap_pad % tm == 0 and d_ff % f_chunk == 0 and d_ext == d + LANES
    kblocks = cap_pad // tm
    return pl.pallas_call(
        functools.partial(_ffn_kernel, f_chunk=f_chunk, n_tiles=n_tiles),
        grid_spec=pltpu.PrefetchScalarGridSpec(
            num_scalar_prefetch=1,
            grid=(n_exp, kblocks),
            in_specs=[
                pl.BlockSpec((tm, d_ext), lambda e, k, b: (e * kblocks + k, 0)),
                pl.BlockSpec((1, d, d_ff), lambda e, k, b: (e, 0, 0)),
                pl.BlockSpec((1, d, d_ff), lambda e, k, b: (e, 0, 0)),
                pl.BlockSpec((1, d_ff, d), lambda e, k, b: (e, 0, 0)),
            ],
            out_specs=pl.BlockSpec((tm, d), lambda e, k, b: (e * kblocks + k, 0)),
        ),
        out_shape=jax.ShapeDtypeStruct((n_slots, d), F32),
        compiler_params=_params("parallel", "parallel"),
        name="expert_ffn",
    )(base_tbl, xe, wg, wu, wd)


def _combine_kernel(base_ref, cnt_ref, x_ref, pos_ref, g_ref, b_ref, ye_ref, o_ref,
                    stg, hit_sc, sems, *, tile, cap_pad, n_tiles, alpha):
    i = pl.program_id(0)
    n_exp = pos_ref.shape[1]
    half = stg.shape[0] // 2

    @pl.when(i == 0)
    def _():
        stg[...] = jnp.zeros(stg.shape, F32)

    def layout(j):
        segs, off = [], jnp.int32(0)
        for e in range(n_exp):
            rows = _pad_rows(cnt_ref[e, j])
            segs.append((base_ref[e, j], rows, off))
            off = off + rows
        return segs, off

    def first_row(j, total):
        return jnp.where(total > half, 0, (j % 2) * half)

    def fetch(j, segs, row0):
        for e, (base, rows, off) in enumerate(segs):
            src0 = e * cap_pad + base
            dst0 = row0 + off

            def make_copy(o, size):
                return pltpu.make_async_copy(ye_ref.at[pl.ds(pl.multiple_of(src0 + o, SUBLANES), size)],
                                             stg.at[pl.ds(pl.multiple_of(dst0 + o, SUBLANES), size)], sems.at[j % 2])

            _start_pieces(rows, tile, make_copy)

    nxt = jnp.minimum(i + 1, n_tiles - 1)
    segs, total = layout(i)
    segs_next, total_next = layout(nxt)
    _, total_prev = layout(jnp.maximum(i - 1, 0))
    row0 = first_row(i, total)
    small, small_next, small_prev = total <= half, total_next <= half, total_prev <= half

    @pl.when(jnp.logical_not((i > 0) & small_prev & small))
    def _():
        fetch(i, segs, row0)

    _wait_rows(ye_ref, total, sems.at[i % 2])

    @pl.when((i + 1 < n_tiles) & small & small_next)
    def _():
        fetch(nxt, segs_next, first_row(nxt, total_next))

    o_ref[...] = alpha * x_ref[...]
    slot_id = lax.broadcasted_iota(I32, (tile, tile), 1)

    def block(b, carry):
        chunk = stg[pl.ds(pl.multiple_of(row0 + b * tile, tile), tile), :].astype(BF16)
        hit_sc[...] = jnp.zeros(hit_sc.shape, F32)
        for e, (base, rows, off) in enumerate(segs):
            @pl.when((off < (b + 1) * tile) & (off + rows > b * tile))
            def _():
                match = slot_id == (pos_ref[:, e:e + 1] - (base - off + b * tile))
                hit_sc[...] = jnp.where(match, 1.0, hit_sc[...])

        o_ref[...] += jnp.dot(hit_sc[...].astype(BF16), chunk, preferred_element_type=F32)
        return carry

    lax.fori_loop(0, (total + (tile - 1)) // tile, block, 0)
    o_ref[...] = _layer_norm(o_ref[...], g_ref[...], b_ref[...])


def _combine(x1, pos_t, base_tbl, cnt_tbl, ln_g, ln_b, ye, *, cap_pad, alpha, tile):
    n, d = x1.shape
    n_exp = pos_t.shape[1]
    row = lambda i, b, c: (i, 0)
    const = lambda i, b, c: (0, 0)
    stg_rows = n_exp * tile
    return pl.pallas_call(
        functools.partial(_combine_kernel, tile=tile, cap_pad=cap_pad, n_tiles=n // tile, alpha=alpha),
        grid_spec=pltpu.PrefetchScalarGridSpec(
            num_scalar_prefetch=2,
            grid=(n // tile,),
            in_specs=[
                pl.BlockSpec((tile, d), row), pl.BlockSpec((tile, n_exp), row),
                pl.BlockSpec((1, d), const), pl.BlockSpec((1, d), const),
                pl.BlockSpec(memory_space=pl.ANY),
            ],
            out_specs=pl.BlockSpec((tile, d), row),
            scratch_shapes=[pltpu.VMEM((stg_rows, d), F32), pltpu.VMEM((tile, tile), F32),
                            pltpu.SemaphoreType.DMA((2,))],
        ),
        out_shape=jax.ShapeDtypeStruct((n, d), F32),
        compiler_params=_params("arbitrary"),
        name="combine_ln",
    )(base_tbl, cnt_tbl, x1, pos_t, ln_g, ln_b, ye)


def _rope_tables(seq):
    t = jnp.arange(seq)
    row = (t // GRID_W).astype(F32)
    col = (t % GRID_W).astype(F32)
    half = HEAD_DIM // 2
    inv_freq = ROPE_THETA ** (-jnp.arange(0, half, 2, dtype=F32) / half)
    ang_r = row[:, None] * inv_freq[None, :]
    ang_c = col[:, None] * inv_freq[None, :]
    ang = jnp.concatenate([ang_r, ang_r, ang_c, ang_c], axis=-1)
    sign = jnp.where((jnp.arange(HEAD_DIM) % half) < half // 2, -1.0, 1.0).astype(F32)
    reps = LANES // HEAD_DIM
    return jnp.tile(jnp.cos(ang), (1, reps)), jnp.tile(jnp.sin(ang) * sign[None, :], (1, reps))


def _gqa_slot_columns():
    heads = [p + GQA_GROUP * half for p in range(HEAD_PAIRS) for half in range(2)]
    return np.concatenate([np.arange(HEAD_DIM) + HEAD_DIM * h for h in heads])


def _prep_layer(w_in, na_rpb, q_norm, k_norm, w_br_na, w_br_gqa, w_out, ln1_g, ln1_b,
                w_router, w_e_gate, w_e_up, w_e_down, ln2_g, ln2_b):
    d = w_in.shape[0]
    s_na = 3 * NA_WIDTH
    cols = _gqa_slot_columns()
    perm = np.concatenate([np.arange(s_na), s_na + cols, np.arange(s_na + GQA_WIDTH, w_in.shape[1])])
    reps = LANES // HEAD_DIM
    wr_t = w_router.T.astype(F32)
    wr_hi = wr_t.astype(BF16)
    gm = np.kron(np.eye(reps, dtype=np.float32), np.full((HEAD_DIM, HEAD_DIM), 1.0 / HEAD_DIM, np.float32))
    return dict(
        w_in=w_in[:, perm].astype(BF16),
        bias_tbl=_na_bias_table(na_rpb),
        qn=jnp.tile(q_norm.astype(F32), reps)[None, :], kn=jnp.tile(k_norm.astype(F32), reps)[None, :],
        gm=jnp.asarray(gm, BF16),
        wna=w_br_na.astype(BF16), wgq=w_br_gqa[cols].astype(BF16), wout=w_out.astype(BF16),
        ln1_g=ln1_g.astype(F32).reshape(1, d), ln1_b=ln1_b.astype(F32).reshape(1, d),
        wr_hi=wr_hi, wr_lo=(wr_t - wr_hi.astype(F32)).astype(BF16),
        wg=w_e_gate.astype(BF16), wu=w_e_up.astype(BF16), wd=w_e_down.astype(BF16),
        ln2_g=ln2_g.astype(F32).reshape(1, d), ln2_b=ln2_b.astype(F32).reshape(1, d),
    )


def _tiles(seq):
    return dict(tm_proj=512, tq=256, tk=1024, q_groups=4, tm_merge=512, t_moe=256, tm_ffn=512, f_chunk=512)


def _trunk_layer(x2d, p, *, seq, alpha, rope):
    n, d = x2d.shape
    n_exp = p["wg"].shape[0]
    cap = EC_CAPACITY * n // n_exp
    tl = _tiles(seq)
    cos, sin = rope
    naq, nak, nav, gq, gk, gv, gates = _inproj(x2d, p["w_in"], cos, sin, p["qn"], p["kn"], p["gm"],
                                               seq=seq, tm=min(tl["tm_proj"], seq))
    na = _na_attention(naq, nak, nav, p["bias_tbl"], seq=seq)
    nb = n // seq
    ones_rows = jnp.zeros((nb, VT_ROWS - KV_WIDTH, seq), BF16).at[:, 0, :].set(1.0)
    gvt = jnp.concatenate([gv.reshape(nb, seq, KV_WIDTH).transpose(0, 2, 1), ones_rows], axis=1)
    gvt = gvt.reshape(nb * VT_ROWS, seq)
    ga = _gqa_attention(gq, gk, gvt, seq=seq, tq=min(tl["tq"], seq), tk=min(tl["tk"], seq // 2),
                        q_groups=tl["q_groups"])
    x1, aff = _merge(na, ga, gates, x2d, p["wna"], p["wgq"], p["wout"], p["ln1_g"], p["ln1_b"],
                     p["wr_hi"], p["wr_lo"], alpha=alpha, tm=tl["tm_merge"])
    tile = tl["t_moe"]
    n_tiles = n // tile
    tm_ffn = min(tl["tm_ffn"], cap)
    cap_pad = pl.cdiv(cap + (SUBLANES - 1) * n_tiles, tm_ffn) * tm_ffn
    pos, base_tbl, cnt_tbl = _route(aff, cap=cap, tile=tile)
    xe = _dispatch(x1, aff.T, pos, base_tbl, cnt_tbl, cap_pad=cap_pad, tile=tile)
    ye = _ffn(xe, base_tbl, p["wg"], p["wu"], p["wd"], cap_pad=cap_pad, tm=tm_ffn, f_chunk=tl["f_chunk"],
              n_tiles=n_tiles)
    return _combine(x1, pos.T, base_tbl, cnt_tbl, p["ln2_g"], p["ln2_b"], ye,
                    cap_pad=cap_pad, alpha=alpha, tile=tile)


@jax.jit
def kernel(x_prompt, x_sample, w_in, na_rpb, q_norm, k_norm, w_br_na, w_br_gqa, w_out, ln1_g, ln1_b,
           w_router, w_e_gate, w_e_up, w_e_down, ln2_g, ln2_b):
    depth = w_in.shape[0]
    alpha = float((2 * depth) ** 0.25)
    d = x_prompt.shape[-1]
    groups = []
    for x in (x_prompt, x_sample):
        b, s, _ = x.shape
        groups.append(dict(x=x.reshape(b * s, d), shape=x.shape, seq=s, rope=_rope_tables(s)))
    for l in range(depth):
        p = _prep_layer(w_in[l], na_rpb[l], q_norm[l], k_norm[l], w_br_na[l], w_br_gqa[l], w_out[l],
                        ln1_g[l], ln1_b[l], w_router[l], w_e_gate[l], w_e_up[l], w_e_down[l],
                        ln2_g[l], ln2_b[l])
        for g in groups:
            g["x"] = _trunk_layer(g["x"], p, seq=g["seq"], alpha=alpha, rope=g["rope"])
    return tuple(g["x"].reshape(g["shape"]) for g in groups)
```

```python
import functools

import jax
import jax.numpy as jnp
import numpy as np
from jax import lax
from jax.experimental import pallas as pl
from jax.experimental.pallas import tpu as pltpu

F32 = jnp.float32
BF16 = jnp.bfloat16
I32 = jnp.int32

GRID_W = 64
HEAD_DIM = 64
NA_HEADS = 8
NA_WIN_H = 8
NA_WIN_W = 16
GQA_Q_HEADS = 8
GQA_KV_HEADS = 2
GQA_GROUP = GQA_Q_HEADS // GQA_KV_HEADS
ROPE_THETA = 10000.0
EC_CAPACITY = 2
LN_EPS = 1e-5
RMS_EPS = 1e-6

LANES = 128
SUBLANES = 8
NA_WIDTH = NA_HEADS * HEAD_DIM
GQA_WIDTH = GQA_Q_HEADS * HEAD_DIM
KV_WIDTH = GQA_KV_HEADS * HEAD_DIM
HEAD_PAIRS = GQA_WIDTH // LANES
BF16_TILE_ROWS = 2 * SUBLANES
VT_ROWS = KV_WIDTH + BF16_TILE_ROWS
UNPICKED = -(1 << 30)
NA_ROWS_PER_ITER = 4
MERGE_PART_ROWS = 256
DISPATCH_FEW_ROWS = 64
GQA_Q_SCALE = HEAD_DIM ** -0.5 * float(np.log2(np.e))
NEG_BIG = -1e30
VMEM_LIMIT_BYTES = 56 * 1024 * 1024

_NT = (((1,), (1,)), ((), ()))


def _params(*sem):
    return pltpu.CompilerParams(dimension_semantics=sem, vmem_limit_bytes=VMEM_LIMIT_BYTES)


def _inproj_kernel(x_ref, w_ref, cos_ref, sin_ref, qn_ref, kn_ref, gm_ref,
                   naq_ref, nak_ref, nav_ref, gq_ref, gk_ref, gv_ref, gate_ref, *, d_model):
    xb = x_ref[...].astype(BF16)
    s_na = 3 * NA_WIDTH
    s_gq = s_na + GQA_WIDTH
    s_gk = s_gq + KV_WIDTH
    s_gv = s_gk + KV_WIDTH

    def proj(c0, width):
        return jnp.dot(xb, w_ref[:, c0:c0 + width], preferred_element_type=F32)

    naq_ref[...] = (proj(0, NA_WIDTH) * HEAD_DIM ** -0.5).astype(BF16)
    nak_ref[...] = proj(NA_WIDTH, NA_WIDTH).astype(BF16)
    nav_ref[...] = proj(2 * NA_WIDTH, NA_WIDTH).astype(BF16)

    cos = cos_ref[...]
    sin = sin_ref[...]
    gm = gm_ref[...]
    lane = lax.broadcasted_iota(I32, cos.shape, 1)
    first_half = (lane % (HEAD_DIM // 2)) < (HEAD_DIM // 4)

    def norm_rope(a, gain):
        sq = a * a
        hi = sq.astype(BF16)
        lo = (sq - hi.astype(F32)).astype(BF16)
        ms = jnp.dot(hi, gm, preferred_element_type=F32) + jnp.dot(lo, gm, preferred_element_type=F32)
        an = a * lax.rsqrt(ms + RMS_EPS) * gain
        quarter = HEAD_DIM // 4
        rot = jnp.where(first_half, pltpu.roll(an, LANES - quarter, 1), pltpu.roll(an, quarter, 1))
        return an * cos + rot * sin

    gq_raw = [proj(s_na + p * LANES, LANES) for p in range(HEAD_PAIRS)]
    gk_raw = proj(s_gq, KV_WIDTH)
    gv_ref[...] = proj(s_gk, KV_WIDTH).astype(BF16)
    gate_chunk = 512
    for c in range(2 * d_model // gate_chunk):
        g = proj(s_gv + c * gate_chunk, gate_chunk)
        gate_ref[:, c * gate_chunk:(c + 1) * gate_chunk] = jax.nn.sigmoid(g)
    qn = qn_ref[...]
    for p in range(HEAD_PAIRS):
        gq_ref[:, p * LANES:(p + 1) * LANES] = (norm_rope(gq_raw[p], qn) * GQA_Q_SCALE).astype(BF16)
    gk_ref[...] = norm_rope(gk_raw, kn_ref[...]).astype(BF16)


def _inproj(x2d, w_in_b, cos, sin, qn, kn, gm, *, seq, tm):
    n, d = x2d.shape
    d_in = w_in_b.shape[1]
    assert n % tm == 0 and seq % tm == 0
    sblocks = seq // tm
    row = lambda i: (i, 0)
    const = lambda i: (0, 0)
    pos = lambda i: (i % sblocks, 0)
    out_shape = [
        jax.ShapeDtypeStruct((n, NA_WIDTH), BF16), jax.ShapeDtypeStruct((n, NA_WIDTH), BF16),
        jax.ShapeDtypeStruct((n, NA_WIDTH), BF16), jax.ShapeDtypeStruct((n, GQA_WIDTH), BF16),
        jax.ShapeDtypeStruct((n, KV_WIDTH), BF16), jax.ShapeDtypeStruct((n, KV_WIDTH), BF16),
        jax.ShapeDtypeStruct((n, 2 * d), F32),
    ]
    out_specs = [
        pl.BlockSpec((tm, NA_WIDTH), row), pl.BlockSpec((tm, NA_WIDTH), row), pl.BlockSpec((tm, NA_WIDTH), row),
        pl.BlockSpec((tm, GQA_WIDTH), row), pl.BlockSpec((tm, KV_WIDTH), row), pl.BlockSpec((tm, KV_WIDTH), row),
        pl.BlockSpec((tm, 2 * d), row),
    ]
    return pl.pallas_call(
        functools.partial(_inproj_kernel, d_model=d),
        grid=(n // tm,),
        in_specs=[
            pl.BlockSpec((tm, d), row), pl.BlockSpec((d, d_in), const),
            pl.BlockSpec((tm, LANES), pos), pl.BlockSpec((tm, LANES), pos),
            pl.BlockSpec((1, LANES), const), pl.BlockSpec((1, LANES), const),
            pl.BlockSpec((LANES, LANES), const),
        ],
        out_specs=out_specs,
        out_shape=out_shape,
        compiler_params=_params("parallel"),
        name="inproj",
    )(x2d, w_in_b, cos, sin, qn, kn, gm)


def _na_kernel(q_ref, kp_ref, kc_ref, kn_ref, vp_ref, vc_ref, vn_ref, bias_ref, o_ref,
               kbuf, vbuf, *, rows, nrb):
    blk = NA_WIN_H * GRID_W
    j = pl.program_id(0) % nrb
    kbuf[0:blk] = kp_ref[...]
    kbuf[blk:2 * blk] = kc_ref[...]
    kbuf[2 * blk:3 * blk] = kn_ref[...]
    vbuf[0:blk] = vp_ref[...]
    vbuf[blk:2 * blk] = vc_ref[...]
    vbuf[2 * blk:3 * blk] = vn_ref[...]
    lane = lax.broadcasted_iota(I32, (GRID_W, LANES), 1)
    low = lane < HEAD_DIM

    def rows_body(it, carry):
        items = []
        for u in range(NA_ROWS_PER_ITER):
            i = it * NA_ROWS_PER_ITER + u
            r = j * NA_WIN_H + i
            rs = jnp.clip(r - NA_WIN_H // 2, 0, rows - NA_WIN_H)
            d0 = rs - r + (NA_WIN_H - 1)
            off = pl.multiple_of((rs - (j - 1) * NA_WIN_H) * GRID_W, GRID_W)
            qoff = pl.multiple_of(i * GRID_W, GRID_W)
            for p in range(NA_WIDTH // LANES):
                items.append((d0, off, qoff, p, slice(p * LANES, (p + 1) * LANES)))
        scores = []
        for d0, off, qoff, p, cols in items:
            q2 = q_ref[pl.ds(qoff, GRID_W), cols]
            zero = jnp.zeros_like(q2)
            qs = jnp.concatenate([jnp.where(low, q2, zero), jnp.where(low, zero, q2)], axis=0)
            k2 = kbuf[pl.ds(off, blk), cols]
            scores.append(lax.dot_general(k2, qs, _NT, preferred_element_type=F32) + bias_ref[d0, p])
        probs = []
        for st in scores:
            e = jnp.exp(st - jnp.max(st, axis=0, keepdims=True))
            probs.append((e * (1.0 / jnp.sum(e, axis=0, keepdims=True))).T.astype(BF16))
        for (d0, off, qoff, p, cols), prob in zip(items, probs):
            o = jnp.dot(prob, vbuf[pl.ds(off, blk), cols], preferred_element_type=F32)
            o_ref[pl.ds(qoff, GRID_W), cols] = jnp.where(low, o[:GRID_W], o[GRID_W:]).astype(BF16)
        return carry

    lax.fori_loop(0, NA_WIN_H // NA_ROWS_PER_ITER, rows_body, 0)


def _na_bias_table(rpb):
    c = jnp.arange(GRID_W)
    cs = jnp.clip(c - NA_WIN_W // 2, 0, GRID_W - NA_WIN_W)
    cc = jnp.arange(GRID_W)
    inwin = (cc[None, :] >= cs[:, None]) & (cc[None, :] < cs[:, None] + NA_WIN_W)
    dc = jnp.clip(cc[None, :] - c[:, None] + (NA_WIN_W - 1), 0, 2 * NA_WIN_W - 2)
    full = jnp.where(inwin[None, None], rpb[:, :, dc].astype(F32), NEG_BIG)
    tbl = jnp.stack([full[:, d0:d0 + NA_WIN_H] for d0 in range(NA_WIN_H)])
    tbl = tbl.reshape(NA_WIN_H, NA_HEADS // 2, 2, NA_WIN_H, GRID_W, GRID_W)
    return tbl.transpose(0, 1, 3, 5, 2, 4).reshape(NA_WIN_H, NA_HEADS // 2, NA_WIN_H * GRID_W, 2 * GRID_W)


def _na_attention(q, k, v, bias_tbl, *, seq):
    n = q.shape[0]
    rows = seq // GRID_W
    assert rows % NA_WIN_H == 0 and rows >= 2 * NA_WIN_H
    nrb = rows // NA_WIN_H
    blk = NA_WIN_H * GRID_W

    def cur(g):
        return (g, 0)

    def prev(g):
        return (g - jnp.where(g % nrb == 0, 0, 1), 0)

    def nxt(g):
        return (g + jnp.where(g % nrb == nrb - 1, 0, 1), 0)

    spec = lambda f: pl.BlockSpec((blk, NA_WIDTH), f)
    return pl.pallas_call(
        functools.partial(_na_kernel, rows=rows, nrb=nrb),
        grid=(n // blk,),
        in_specs=[spec(cur), spec(prev), spec(cur), spec(nxt), spec(prev), spec(cur), spec(nxt),
                  pl.BlockSpec(bias_tbl.shape, lambda g: (0, 0, 0, 0))],
        out_specs=spec(cur),
        out_shape=jax.ShapeDtypeStruct((n, NA_WIDTH), BF16),
        scratch_shapes=[pltpu.VMEM((3 * blk, NA_WIDTH), BF16), pltpu.VMEM((3 * blk, NA_WIDTH), BF16)],
        compiler_params=_params("parallel"),
        name="na_attention",
    )(q, k, k, k, v, v, v, bias_tbl)


def _gqa_kernel(q_ref, k_ref, vt_ref, o_ref, qs_sc, sa_sc, sb_sc, m_sc, acc_sc, *, tq, tk, seq):
    q_groups = qs_sc.shape[0]
    lane = lax.broadcasted_iota(I32, (tq, LANES), 1)
    low = lane < HEAD_DIM
    for g in range(q_groups):
        for p in range(HEAD_PAIRS):
            q2 = q_ref[g * tq:(g + 1) * tq, p * LANES:(p + 1) * LANES]
            zero = jnp.zeros_like(q2)
            qs_sc[g, (2 * p) * tq:(2 * p + 1) * tq, :] = jnp.where(low, q2, zero)
            qs_sc[g, (2 * p + 1) * tq:(2 * p + 2) * tq, :] = jnp.where(low, zero, q2)
    m_sc[...] = jnp.full(m_sc.shape, -jnp.inf, F32)
    acc_sc[...] = jnp.zeros(acc_sc.shape, F32)
    n_chunks = seq // tk
    n_items = q_groups * n_chunks
    assert n_chunks % 2 == 0

    def scores(w, st_ref):
        koff = pl.multiple_of((w % n_chunks) * tk, tk)
        st_ref[...] = lax.dot_general(k_ref[pl.ds(koff, tk), :], qs_sc[w // n_chunks], _NT,
                                      preferred_element_type=F32)

    def consume(w, st_ref):
        g = w // n_chunks
        koff = pl.multiple_of((w % n_chunks) * tk, tk)
        st = st_ref[...]
        m_old = m_sc[g]
        m_new = jnp.maximum(m_old, jnp.max(st, axis=0, keepdims=True))
        alpha = jnp.exp2(m_old - m_new)
        e = jnp.exp2(st - m_new).astype(BF16)
        pv = jnp.dot(vt_ref[:, pl.ds(koff, tk)], e, preferred_element_type=F32)
        acc_sc[g] = alpha * acc_sc[g] + pv
        m_sc[g] = m_new

    scores(0, sa_sc)

    def body(j, carry):
        scores(2 * j + 1, sb_sc)
        consume(2 * j, sa_sc)
        scores(2 * j + 2, sa_sc)
        consume(2 * j + 1, sb_sc)
        return carry

    lax.fori_loop(0, n_items // 2 - 1, body, 0)
    scores(n_items - 1, sb_sc)
    consume(n_items - 2, sa_sc)
    consume(n_items - 1, sb_sc)
    top = lax.broadcasted_iota(I32, (KV_WIDTH, tq), 0) < HEAD_DIM
    for g in range(q_groups):
        o_t = acc_sc[g, 0:KV_WIDTH, :] * (1.0 / acc_sc[g, KV_WIDTH:KV_WIDTH + 1, :])
        for p in range(HEAD_PAIRS):
            a = o_t[:, (2 * p) * tq:(2 * p + 1) * tq]
            b = o_t[:, (2 * p + 1) * tq:(2 * p + 2) * tq]
            o_ref[g * tq:(g + 1) * tq, p * LANES:(p + 1) * LANES] = jnp.where(top, a, b).T.astype(BF16)


def _gqa_attention(q, k, vt, *, seq, tq, tk, q_groups):
    n = q.shape[0]
    rows = q_groups * tq
    assert seq % rows == 0 and seq % tk == 0
    qblocks = seq // rows
    width = GQA_Q_HEADS * tq
    return pl.pallas_call(
        functools.partial(_gqa_kernel, tq=tq, tk=tk, seq=seq),
        grid=(n // seq, qblocks),
        in_specs=[
            pl.BlockSpec((rows, GQA_WIDTH), lambda b, i: (b * qblocks + i, 0)),
            pl.BlockSpec((seq, KV_WIDTH), lambda b, i: (b, 0)),
            pl.BlockSpec((VT_ROWS, seq), lambda b, i: (b, 0)),
        ],
        out_specs=pl.BlockSpec((rows, GQA_WIDTH), lambda b, i: (b * qblocks + i, 0)),
        out_shape=jax.ShapeDtypeStruct((n, GQA_WIDTH), BF16),
        scratch_shapes=[pltpu.VMEM((q_groups, width, KV_WIDTH), BF16), pltpu.VMEM((tk, width), F32),
                        pltpu.VMEM((tk, width), F32), pltpu.VMEM((q_groups, 1, width), F32),
                        pltpu.VMEM((q_groups, VT_ROWS, width), F32)],
        compiler_params=_params("parallel", "parallel"),
        name="gqa_attention",
    )(q, k, vt)


def _layer_norm(h, g, b):
    mu = jnp.mean(h, axis=-1, keepdims=True)
    hc = h - mu
    var = jnp.mean(hc * hc, axis=-1, keepdims=True)
    return hc * lax.rsqrt(var + LN_EPS) * g + b


def _merge_kernel(na_ref, gq_ref, gate_ref, x_ref, wna_ref, wgq_ref, wout_ref, g_ref, b_ref,
                  wrh_ref, wrl_ref, x1_ref, aff_ref, *, alpha, d_model):
    tm = x_ref.shape[0]
    parts = [slice(r, r + MERGE_PART_ROWS) for r in range(0, tm, MERGE_PART_ROWS)]
    y_na = [jnp.dot(na_ref[rows, :], wna_ref[...], preferred_element_type=F32) for rows in parts]
    y_gq = [jnp.dot(gq_ref[rows, :], wgq_ref[...], preferred_element_type=F32) for rows in parts]
    mixin = [(gate_ref[rows, :d_model] * a + gate_ref[rows, d_model:] * b).astype(BF16)
             for rows, a, b in zip(parts, y_na, y_gq)]
    mix = [jnp.dot(m, wout_ref[...], preferred_element_type=F32) for m in mixin]
    x1 = [_layer_norm(alpha * x_ref[rows, :] + m, g_ref[...], b_ref[...]) for rows, m in zip(parts, mix)]
    wh = wrh_ref[...]
    for rows, x1p in zip(parts, x1):
        x1_ref[rows, :] = x1p
        hi = x1p.astype(BF16)
        lo = (x1p - hi.astype(F32)).astype(BF16)
        logits = (lax.dot_general(wh, hi, _NT, preferred_element_type=F32)
                  + lax.dot_general(wh, lo, _NT, preferred_element_type=F32)
                  + lax.dot_general(wrl_ref[...], hi, _NT, preferred_element_type=F32))
        e = jnp.exp(logits - jnp.max(logits, axis=0, keepdims=True))
        aff_ref[:, rows] = e / jnp.sum(e, axis=0, keepdims=True)


def _merge(na, gq, gates, x2d, wna, wgq, wout, ln_g, ln_b, wr_hi, wr_lo, *, alpha, tm):
    n, d = x2d.shape
    n_exp = wr_hi.shape[0]
    row = lambda i: (i, 0)
    const = lambda i: (0, 0)
    return pl.pallas_call(
        functools.partial(_merge_kernel, alpha=alpha, d_model=d),
        grid=(n // tm,),
        in_specs=[
            pl.BlockSpec((tm, NA_WIDTH), row), pl.BlockSpec((tm, GQA_WIDTH), row),
            pl.BlockSpec((tm, 2 * d), row), pl.BlockSpec((tm, d), row),
            pl.BlockSpec((NA_WIDTH, d), const), pl.BlockSpec((GQA_WIDTH, d), const),
            pl.BlockSpec((d, d), const), pl.BlockSpec((1, d), const), pl.BlockSpec((1, d), const),
            pl.BlockSpec((n_exp, d), const), pl.BlockSpec((n_exp, d), const),
        ],
        out_specs=[pl.BlockSpec((tm, d), row), pl.BlockSpec((n_exp, tm), lambda i: (0, i))],
        out_shape=[jax.ShapeDtypeStruct((n, d), F32), jax.ShapeDtypeStruct((n_exp, n), F32)],
        compiler_params=_params("parallel"),
        name="merge_ln_router",
    )(na, gq, gates, x2d, wna, wgq, wout, ln_g, ln_b, wr_hi, wr_lo)


def _route_kernel(aff_ref, pos_ref, base_ref, cnt_ref, *, cap, tile, count_chunk):
    n_exp, n = aff_ref.shape
    n_tiles = n // tile
    capf = float(cap)

    def bits_at(off, width):
        return lax.bitcast_convert_type(aff_ref[:, pl.ds(off, width)], I32)

    def count_ge(cand):
        def inner(c, acc):
            b = bits_at(pl.multiple_of(c * count_chunk, count_chunk), count_chunk)
            return acc + jnp.where(b >= cand, 1.0, 0.0)
        acc = lax.fori_loop(0, n // count_chunk, inner, jnp.zeros((n_exp, count_chunk), F32))
        return jnp.sum(acc, axis=1, keepdims=True)

    def bisect(i, prefix):
        cand = prefix | jnp.left_shift(jnp.int32(1), 30 - i)
        return jnp.where(count_ge(cand) >= capf, cand, prefix)

    thr = lax.fori_loop(0, 31, bisect, jnp.zeros((n_exp, 1), I32))
    need = capf - count_ge(thr + 1)

    ri = lax.broadcasted_iota(I32, (tile, tile), 0)
    ci = lax.broadcasted_iota(I32, (tile, tile), 1)
    upper = jnp.where(ri < ci, 1.0, 0.0).astype(BF16)
    ones = jnp.ones((tile, tile), BF16)
    tbl_lane = lax.broadcasted_iota(I32, base_ref.shape, 1)

    base_ref[...] = jnp.zeros(base_ref.shape, I32)
    cnt_ref[...] = jnp.zeros(cnt_ref.shape, I32)

    def body(i, carry):
        ceq, base = carry
        off = pl.multiple_of(i * tile, tile)
        b = lax.bitcast_convert_type(aff_ref[:, pl.ds(off, tile)], I32)
        eq = b == thr
        eqb = jnp.where(eq, 1.0, 0.0).astype(BF16)
        eqrank = jnp.dot(eqb, upper, preferred_element_type=F32) + ceq
        sel = (b > thr) | (eq & (eqrank < need))
        selb = jnp.where(sel, 1.0, 0.0).astype(BF16)
        rank_in_tile = jnp.dot(selb, upper, preferred_element_type=F32)
        cnt = jnp.dot(selb, ones, preferred_element_type=F32)
        pos_ref[:, pl.ds(off, tile)] = jnp.where(sel, base + rank_in_tile, float(UNPICKED)).astype(I32)
        base_ref[...] = jnp.where(tbl_lane == i, base[:, :1].astype(I32), base_ref[...])
        cnt_ref[...] = jnp.where(tbl_lane == i, cnt[:, :1].astype(I32), cnt_ref[...])
        padded = jnp.floor((cnt + (SUBLANES - 1.0)) * (1.0 / SUBLANES)) * SUBLANES
        return ceq + jnp.dot(eqb, ones, preferred_element_type=F32), base + padded

    zero = jnp.zeros((n_exp, tile), F32)
    _, used = lax.fori_loop(0, n_tiles, body, (zero, zero))
    base_ref[...] = jnp.where(tbl_lane == n_tiles, used[:, :1].astype(I32), base_ref[...])


def _route(aff, *, cap, tile):
    n_exp, n = aff.shape
    count_chunk = min(2048, n)
    assert n % count_chunk == 0 and n % tile == 0
    tbl_w = pl.cdiv(n // tile + 1, LANES) * LANES
    full = lambda shape: pl.BlockSpec(shape, lambda i: (0, 0))
    return pl.pallas_call(
        functools.partial(_route_kernel, cap=cap, tile=tile, count_chunk=count_chunk),
        grid=(1,),
        in_specs=[full((n_exp, n))],
        out_specs=[full((n_exp, n)), full((n_exp, tbl_w)), full((n_exp, tbl_w))],
        out_shape=[jax.ShapeDtypeStruct((n_exp, n), I32), jax.ShapeDtypeStruct((n_exp, tbl_w), I32),
                   jax.ShapeDtypeStruct((n_exp, tbl_w), I32)],
        compiler_params=_params("arbitrary"),
        name="route",
    )(aff)


def _pad_rows(c):
    return ((c + (SUBLANES - 1)) >> 3) << 3


def _start_pieces(rows, max_rows, make_copy):
    k = 3
    assert SUBLANES == 1 << k
    while (1 << k) <= max_rows:
        size = 1 << k

        @pl.when(((rows >> k) & 1) == 1)
        def _():
            make_copy(pl.multiple_of((rows >> (k + 1)) << (k + 1), SUBLANES), size).start()

        k += 1


def _wait_rows(hbm_ref, rows, sem):
    @pl.when(rows > 0)
    def _():
        view = hbm_ref.at[pl.ds(0, pl.multiple_of(rows, SUBLANES))]
        pltpu.make_async_copy(view, view, sem).wait()


def _dispatch_kernel(base_ref, cnt_ref, x_ref, aff_ref, pos_ref, xe_ref, few_sc, long_sc, few_sems, long_sem,
                     *, tile, cap_pad, n_tiles):
    i = pl.program_id(0)
    n_exp = pos_ref.shape[0]
    d = x_ref.shape[1]
    xb = x_ref[...].astype(BF16)
    a = aff_ref[...]
    hi = a.astype(BF16)
    rest1 = a - hi.astype(F32)
    mid = rest1.astype(BF16)
    lo = (rest1 - mid.astype(F32)).astype(BF16)
    pr = lax.broadcasted_iota(I32, (n_exp, LANES), 0)
    pc = lax.broadcasted_iota(I32, (n_exp, LANES), 1)
    gates = sum(jnp.dot(piece, jnp.where(pc == k * n_exp + pr, 1.0, 0.0).astype(BF16), preferred_element_type=F32)
                for k, piece in enumerate((hi, mid, lo))).astype(BF16)
    few = DISPATCH_FEW_ROWS
    par = i % 2

    def few_rows_total(j):
        total = jnp.int32(0)
        for e in range(n_exp):
            r = _pad_rows(cnt_ref[e, j])
            total = total + jnp.where(r <= few, r, 0)
        return total

    @pl.when(i >= 2)
    def _():
        _wait_rows(xe_ref, few_rows_total(jnp.maximum(i - 2, 0)), few_sems.at[par])

    bases = [base_ref[e, i] for e in range(n_exp)]
    padded = [_pad_rows(cnt_ref[e, i]) for e in range(n_exp)]

    def compact(e, n_rows):
        row_id = lax.broadcasted_iota(I32, (n_rows, tile), 0)
        return jnp.where(row_id == pos_ref[e:e + 1, :] - bases[e], 1.0, 0.0).astype(BF16)

    onehot = jnp.concatenate([compact(e, few) for e in range(n_exp)], axis=0)
    few_sc[par, :, 0:d] = jnp.dot(onehot, xb, preferred_element_type=F32)
    few_sc[par, :, d:d + LANES] = jnp.dot(onehot, gates, preferred_element_type=F32)

    long_total = jnp.int32(0)
    for e in range(n_exp):
        rows = padded[e]
        dst0 = e * cap_pad + bases[e]
        is_long = rows > few

        def few_copy(off, size):
            return pltpu.make_async_copy(few_sc.at[par, pl.ds(pl.multiple_of(e * few + off, SUBLANES), size)],
                                         xe_ref.at[pl.ds(pl.multiple_of(dst0 + off, SUBLANES), size)],
                                         few_sems.at[par])

        _start_pieces(jnp.where(is_long, 0, rows), few, few_copy)

        @pl.when(is_long)
        def _():
            full = compact(e, tile)
            long_sc[e, :, 0:d] = jnp.dot(full, xb, preferred_element_type=F32)
            long_sc[e, :, d:d + LANES] = jnp.dot(full, gates, preferred_element_type=F32)

            def long_copy(off, size):
                return pltpu.make_async_copy(long_sc.at[e, pl.ds(off, size)],
                                             xe_ref.at[pl.ds(pl.multiple_of(dst0 + off, SUBLANES), size)], long_sem)

            _start_pieces(rows, tile, long_copy)

        long_total = long_total + jnp.where(is_long, rows, 0)
    _wait_rows(xe_ref, long_total, long_sem)

    @pl.when(i == n_tiles - 1)
    def _():
        _wait_rows(xe_ref, few_rows_total(i), few_sems.at[par])

        @pl.when(i >= 1)
        def _():
            _wait_rows(xe_ref, few_rows_total(jnp.maximum(i - 1, 0)), few_sems.at[1 - par])

        long_sc[0] = jnp.zeros((tile, long_sc.shape[2]), F32)
        for e in range(n_exp):
            used = base_ref[e, n_tiles]
            tail = cap_pad - used
            dst0 = e * cap_pad + used

            def zero_copy(off, size):
                return pltpu.make_async_copy(long_sc.at[0, pl.ds(0, size)],
                                             xe_ref.at[pl.ds(pl.multiple_of(dst0 + off, SUBLANES), size)], long_sem)

            def whole(j, carry):
                cp = zero_copy(j * tile, tile)
                cp.start()
                cp.wait()
                return carry

            n_whole = tail // tile
            lax.fori_loop(0, n_whole, whole, 0)
            rest = tail - n_whole * tile
            _start_pieces(rest, tile, lambda off, size: zero_copy(n_whole * tile + off, size))
            _wait_rows(xe_ref, rest, long_sem)


def _dispatch(x1, aff_t, pos, base_tbl, cnt_tbl, *, cap_pad, tile):
    n, d = x1.shape
    n_exp = pos.shape[0]
    assert 3 * n_exp <= LANES
    n_tiles = n // tile
    return pl.pallas_call(
        functools.partial(_dispatch_kernel, tile=tile, cap_pad=cap_pad, n_tiles=n_tiles),
        grid_spec=pltpu.PrefetchScalarGridSpec(
            num_scalar_prefetch=2,
            grid=(n_tiles,),
            in_specs=[pl.BlockSpec((tile, d), lambda i, b, c: (i, 0)),
                      pl.BlockSpec((tile, n_exp), lambda i, b, c: (i, 0)),
                      pl.BlockSpec((n_exp, tile), lambda i, b, c: (0, i))],
            out_specs=pl.BlockSpec(memory_space=pl.ANY),
            scratch_shapes=[pltpu.VMEM((2, n_exp * DISPATCH_FEW_ROWS, d + LANES), F32),
                            pltpu.VMEM((n_exp, tile, d + LANES), F32),
                            pltpu.SemaphoreType.DMA((2,)), pltpu.SemaphoreType.DMA(())],
        ),
        out_shape=jax.ShapeDtypeStruct((n_exp * cap_pad, d + LANES), F32),
        compiler_params=_params("arbitrary"),
        name="dispatch",
    )(base_tbl, cnt_tbl, x1, aff_t, pos)


def _ffn_kernel(base_ref, xe_ref, wg_ref, wu_ref, wd_ref, ye_ref, *, f_chunk, n_tiles):
    tm = xe_ref.shape[0]
    d = ye_ref.shape[1]
    n_exp = pl.num_programs(0)
    e = pl.program_id(0)
    used = base_ref[e, n_tiles]
    live = pl.program_id(1) * tm < used

    @pl.when(live)
    def _():
        xb = xe_ref[:, 0:d].astype(BF16)
        lane = lax.broadcasted_iota(I32, (tm, LANES), 1)
        mine = (lane % n_exp == e) & (lane < 3 * n_exp)
        gate = jnp.sum(jnp.where(mine, xe_ref[:, d:d + LANES], 0.0), axis=1, keepdims=True)
        n_chunks = wg_ref.shape[2] // f_chunk

        def gate_up(c):
            cols = slice(c * f_chunk, (c + 1) * f_chunk)
            return (jnp.dot(xb, wg_ref[0, :, cols], preferred_element_type=F32),
                    jnp.dot(xb, wu_ref[0, :, cols], preferred_element_type=F32))

        acc = jnp.zeros(ye_ref.shape, F32)
        g, u = gate_up(0)
        for c in range(n_chunks):
            h = ((g * jax.nn.sigmoid(g)) * u).astype(BF16)
            if c + 1 < n_chunks:
                g, u = gate_up(c + 1)
            acc = acc + jnp.dot(h, wd_ref[0, c * f_chunk:(c + 1) * f_chunk, :], preferred_element_type=F32)
        ye_ref[...] = acc * gate

    @pl.when(jnp.logical_not(live))
    def _():
        ye_ref[...] = jnp.zeros(ye_ref.shape, F32)


def _ffn(xe, base_tbl, wg, wu, wd, *, cap_pad, tm, f_chunk, n_tiles):
    n_slots, d_ext = xe.shape
    n_exp, d, d_ff = wg.shape
    assert cap_pad % tm == 0 and d_ff % f_chunk == 0 and d_ext == d + LANES
    kblocks = cap_pad // tm
    return pl.pallas_call(
        functools.partial(_ffn_kernel, f_chunk=f_chunk, n_tiles=n_tiles),
        grid_spec=pltpu.PrefetchScalarGridSpec(
            num_scalar_prefetch=1,
            grid=(n_exp, kblocks),
            in_specs=[
                pl.BlockSpec((tm, d_ext), lambda e, k, b: (e * kblocks + k, 0)),
                pl.BlockSpec((1, d, d_ff), lambda e, k, b: (e, 0, 0)),
                pl.BlockSpec((1, d, d_ff), lambda e, k, b: (e, 0, 0)),
                pl.BlockSpec((1, d_ff, d), lambda e, k, b: (e, 0, 0)),
            ],
            out_specs=pl.BlockSpec((tm, d), lambda e, k, b: (e * kblocks + k, 0)),
        ),
        out_shape=jax.ShapeDtypeStruct((n_slots, d), F32),
        compiler_params=_params("parallel", "parallel"),
        name="expert_ffn",
    )(base_tbl, xe, wg, wu, wd)


def _combine_kernel(base_ref, cnt_ref, x_ref, pos_ref, g_ref, b_ref, ye_ref, o_ref,
                    stg, hit_sc, sems, *, tile, cap_pad, n_tiles, alpha):
    i = pl.program_id(0)
    n_exp = pos_ref.shape[1]
    half = stg.shape[0] // 2

    @pl.when(i == 0)
    def _():
        stg[...] = jnp.zeros(stg.shape, F32)

    def layout(j):
        segs, off = [], jnp.int32(0)
        for e in range(n_exp):
            rows = _pad_rows(cnt_ref[e, j])
            segs.append((base_ref[e, j], rows, off))
            off = off + rows
        return segs, off

    def first_row(j, total):
        return jnp.where(total > half, 0, (j % 2) * half)

    def fetch(j, segs, row0):
        for e, (base, rows, off) in enumerate(segs):
            src0 = e * cap_pad + base
            dst0 = row0 + off

            def make_copy(o, size):
                return pltpu.make_async_copy(ye_ref.at[pl.ds(pl.multiple_of(src0 + o, SUBLANES), size)],
                                             stg.at[pl.ds(pl.multiple_of(dst0 + o, SUBLANES), size)], sems.at[j % 2])

            _start_pieces(rows, tile, make_copy)

    nxt = jnp.minimum(i + 1, n_tiles - 1)
    segs, total = layout(i)
    segs_next, total_next = layout(nxt)
    _, total_prev = layout(jnp.maximum(i - 1, 0))
    row0 = first_row(i, total)
    small, small_next, small_prev = total <= half, total_next <= half, total_prev <= half

    @pl.when(jnp.logical_not((i > 0) & small_prev & small))
    def _():
        fetch(i, segs, row0)

    _wait_rows(ye_ref, total, sems.at[i % 2])

    @pl.when((i + 1 < n_tiles) & small & small_next)
    def _():
        fetch(nxt, segs_next, first_row(nxt, total_next))

    o_ref[...] = alpha * x_ref[...]
    slot_id = lax.broadcasted_iota(I32, (tile, tile), 1)

    def block(b, carry):
        chunk = stg[pl.ds(pl.multiple_of(row0 + b * tile, tile), tile), :].astype(BF16)
        hit_sc[...] = jnp.zeros(hit_sc.shape, F32)
        for e, (base, rows, off) in enumerate(segs):
            @pl.when((off < (b + 1) * tile) & (off + rows > b * tile))
            def _():
                match = slot_id == (pos_ref[:, e:e + 1] - (base - off + b * tile))
                hit_sc[...] = jnp.where(match, 1.0, hit_sc[...])

        o_ref[...] += jnp.dot(hit_sc[...].astype(BF16), chunk, preferred_element_type=F32)
        return carry

    lax.fori_loop(0, (total + (tile - 1)) // tile, block, 0)
    o_ref[...] = _layer_norm(o_ref[...], g_ref[...], b_ref[...])


def _combine(x1, pos_t, base_tbl, cnt_tbl, ln_g, ln_b, ye, *, cap_pad, alpha, tile):
    n, d = x1.shape
    n_exp = pos_t.shape[1]
    row = lambda i, b, c: (i, 0)
    const = lambda i, b, c: (0, 0)
    stg_rows = n_exp * tile
    return pl.pallas_call(
        functools.partial(_combine_kernel, tile=tile, cap_pad=cap_pad, n_tiles=n // tile, alpha=alpha),
        grid_spec=pltpu.PrefetchScalarGridSpec(
            num_scalar_prefetch=2,
            grid=(n // tile,),
            in_specs=[
                pl.BlockSpec((tile, d), row), pl.BlockSpec((tile, n_exp), row),
                pl.BlockSpec((1, d), const), pl.BlockSpec((1, d), const),
                pl.BlockSpec(memory_space=pl.ANY),
            ],
            out_specs=pl.BlockSpec((tile, d), row),
            scratch_shapes=[pltpu.VMEM((stg_rows, d), F32), pltpu.VMEM((tile, tile), F32),
                            pltpu.SemaphoreType.DMA((2,))],
        ),
        out_shape=jax.ShapeDtypeStruct((n, d), F32),
        compiler_params=_params("arbitrary"),
        name="combine_ln",
    )(base_tbl, cnt_tbl, x1, pos_t, ln_g, ln_b, ye)


def _rope_tables(seq):
    t = jnp.arange(seq)
    row = (t // GRID_W).astype(F32)
    col = (t % GRID_W).astype(F32)
    half = HEAD_DIM // 2
    inv_freq = ROPE_THETA ** (-jnp.arange(0, half, 2, dtype=F32) / half)
    ang_r = row[:, None] * inv_freq[None, :]
    ang_c = col[:, None] * inv_freq[None, :]
    ang = jnp.concatenate([ang_r, ang_r, ang_c, ang_c], axis=-1)
    sign = jnp.where((jnp.arange(HEAD_DIM) % half) < half // 2, -1.0, 1.0).astype(F32)
    reps = LANES // HEAD_DIM
    return jnp.tile(jnp.cos(ang), (1, reps)), jnp.tile(jnp.sin(ang) * sign[None, :], (1, reps))


def _gqa_slot_columns():
    heads = [p + GQA_GROUP * half for p in range(HEAD_PAIRS) for half in range(2)]
    return np.concatenate([np.arange(HEAD_DIM) + HEAD_DIM * h for h in heads])


def _prep_layer(w_in, na_rpb, q_norm, k_norm, w_br_na, w_br_gqa, w_out, ln1_g, ln1_b,
                w_router, w_e_gate, w_e_up, w_e_down, ln2_g, ln2_b):
    d = w_in.shape[0]
    s_na = 3 * NA_WIDTH
    cols = _gqa_slot_columns()
    perm = np.concatenate([np.arange(s_na), s_na + cols, np.arange(s_na + GQA_WIDTH, w_in.shape[1])])
    reps = LANES // HEAD_DIM
    wr_t = w_router.T.astype(F32)
    wr_hi = wr_t.astype(BF16)
    gm = np.kron(np.eye(reps, dtype=np.float32), np.full((HEAD_DIM, HEAD_DIM), 1.0 / HEAD_DIM, np.float32))
    return dict(
        w_in=w_in[:, perm].astype(BF16),
        bias_tbl=_na_bias_table(na_rpb),
        qn=jnp.tile(q_norm.astype(F32), reps)[None, :], kn=jnp.tile(k_norm.astype(F32), reps)[None, :],
        gm=jnp.asarray(gm, BF16),
        wna=w_br_na.astype(BF16), wgq=w_br_gqa[cols].astype(BF16), wout=w_out.astype(BF16),
        ln1_g=ln1_g.astype(F32).reshape(1, d), ln1_b=ln1_b.astype(F32).reshape(1, d),
        wr_hi=wr_hi, wr_lo=(wr_t - wr_hi.astype(F32)).astype(BF16),
        wg=w_e_gate.astype(BF16), wu=w_e_up.astype(BF16), wd=w_e_down.astype(BF16),
        ln2_g=ln2_g.astype(F32).reshape(1, d), ln2_b=ln2_b.astype(F32).reshape(1, d),
    )


def _tiles(seq):
    return dict(tm_proj=512, tq=256, tk=1024, q_groups=4, tm_merge=512, t_moe=256, tm_ffn=512, f_chunk=512)


def _trunk_layer(x2d, p, *, seq, alpha, rope):
    n, d = x2d.shape
    n_exp = p["wg"].shape[0]
    cap = EC_CAPACITY * n // n_exp
    tl = _tiles(seq)
    cos, sin = rope
    naq, nak, nav, gq, gk, gv, gates = _inproj(x2d, p["w_in"], cos, sin, p["qn"], p["kn"], p["gm"],
                                               seq=seq, tm=min(tl["tm_proj"], seq))
    na = _na_attention(naq, nak, nav, p["bias_tbl"], seq=seq)
    nb = n // seq
    ones_rows = jnp.zeros((nb, VT_ROWS - KV_WIDTH, seq), BF16).at[:, 0, :].set(1.0)
    gvt = jnp.concatenate([gv.reshape(nb, seq, KV_WIDTH).transpose(0, 2, 1), ones_rows], axis=1)
    gvt = gvt.reshape(nb * VT_ROWS, seq)
    ga = _gqa_attention(gq, gk, gvt, seq=seq, tq=min(tl["tq"], seq), tk=min(tl["tk"], seq // 2),
                        q_groups=tl["q_groups"])
    x1, aff = _merge(na, ga, gates, x2d, p["wna"], p["wgq"], p["wout"], p["ln1_g"], p["ln1_b"],
                     p["wr_hi"], p["wr_lo"], alpha=alpha, tm=tl["tm_merge"])
    tile = tl["t_moe"]
    n_tiles = n // tile
    tm_ffn = min(tl["tm_ffn"], cap)
    cap_pad = pl.cdiv(cap + (SUBLANES - 1) * n_tiles, tm_ffn) * tm_ffn
    pos, base_tbl, cnt_tbl = _route(aff, cap=cap, tile=tile)
    xe = _dispatch(x1, aff.T, pos, base_tbl, cnt_tbl, cap_pad=cap_pad, tile=tile)
    ye = _ffn(xe, base_tbl, p["wg"], p["wu"], p["wd"], cap_pad=cap_pad, tm=tm_ffn, f_chunk=tl["f_chunk"],
              n_tiles=n_tiles)
    return _combine(x1, pos.T, base_tbl, cnt_tbl, p["ln2_g"], p["ln2_b"], ye,
                    cap_pad=cap_pad, alpha=alpha, tile=tile)


@jax.jit
def kernel(x_prompt, x_sample, w_in, na_rpb, q_norm, k_norm, w_br_na, w_br_gqa, w_out, ln1_g, ln1_b,
           w_router, w_e_gate, w_e_up, w_e_down, ln2_g, ln2_b):
    depth = w_in.shape[0]
    alpha = float((2 * depth) ** 0.25)
    d = x_prompt.shape[-1]
    groups = []
    for x in (x_prompt, x_sample):
        b, s, _ = x.shape
        groups.append(dict(x=x.reshape(b * s, d), shape=x.shape, seq=s, rope=_rope_tables(s)))
    for l in range(depth):
        p = _prep_layer(w_in[l], na_rpb[l], q_norm[l], k_norm[l], w_br_na[l], w_br_gqa[l], w_out[l],
                        ln1_g[l], ln1_b[l], w_router[l], w_e_gate[l], w_e_up[l], w_e_down[l],
                        ln2_g[l], ln2_b[l])
        for g in groups:
            g["x"] = _trunk_layer(g["x"], p, seq=g["seq"], alpha=alpha, rope=g["rope"])
    return tuple(g["x"].reshape(g["shape"]) for g in groups)
```

```python
import functools

import jax
import jax.numpy as jnp
import numpy as np
from jax import lax
from jax.experimental import pallas as pl
from jax.experimental.pallas import tpu as pltpu

F32 = jnp.float32
BF16 = jnp.bfloat16
I32 = jnp.int32

GRID_W = 64
HEAD_DIM = 64
NA_HEADS = 8
NA_WIN_H = 8
NA_WIN_W = 16
GQA_Q_HEADS = 8
GQA_KV_HEADS = 2
GQA_GROUP = GQA_Q_HEADS // GQA_KV_HEADS
ROPE_THETA = 10000.0
EC_CAPACITY = 2
LN_EPS = 1e-5
RMS_EPS = 1e-6

LANES = 128
SUBLANES = 8
NA_WIDTH = NA_HEADS * HEAD_DIM
GQA_WIDTH = GQA_Q_HEADS * HEAD_DIM
KV_WIDTH = GQA_KV_HEADS * HEAD_DIM
HEAD_PAIRS = GQA_WIDTH // LANES
BF16_TILE_ROWS = 2 * SUBLANES
VT_ROWS = KV_WIDTH + BF16_TILE_ROWS
UNPICKED = -(1 << 30)
NA_ROWS_PER_ITER = 4
MERGE_PART_ROWS = 256
DISPATCH_FEW_ROWS = 64
GQA_Q_SCALE = HEAD_DIM ** -0.5 * float(np.log2(np.e))
NEG_BIG = -1e30
VMEM_LIMIT_BYTES = 56 * 1024 * 1024

_NT = (((1,), (1,)), ((), ()))


def _params(*sem):
    return pltpu.CompilerParams(dimension_semantics=sem, vmem_limit_bytes=VMEM_LIMIT_BYTES)


def _inproj_kernel(x_ref, w_ref, cos_ref, sin_ref, qn_ref, kn_ref, gm_ref,
                   naq_ref, nak_ref, nav_ref, gq_ref, gk_ref, gv_ref, gate_ref, *, d_model):
    xb = x_ref[...].astype(BF16)
    s_na = 3 * NA_WIDTH
    s_gq = s_na + GQA_WIDTH
    s_gk = s_gq + KV_WIDTH
    s_gv = s_gk + KV_WIDTH

    def proj(c0, width):
        return jnp.dot(xb, w_ref[:, c0:c0 + width], preferred_element_type=F32)

    naq_ref[...] = (proj(0, NA_WIDTH) * HEAD_DIM ** -0.5).astype(BF16)
    nak_ref[...] = proj(NA_WIDTH, NA_WIDTH).astype(BF16)
    nav_ref[...] = proj(2 * NA_WIDTH, NA_WIDTH).astype(BF16)

    cos = cos_ref[...]
    sin = sin_ref[...]
    gm = gm_ref[...]
    lane = lax.broadcasted_iota(I32, cos.shape, 1)
    first_half = (lane % (HEAD_DIM // 2)) < (HEAD_DIM // 4)

    def norm_rope(a, gain):
        sq = a * a
        hi = sq.astype(BF16)
        lo = (sq - hi.astype(F32)).astype(BF16)
        ms = jnp.dot(hi, gm, preferred_element_type=F32) + jnp.dot(lo, gm, preferred_element_type=F32)
        an = a * lax.rsqrt(ms + RMS_EPS) * gain
        quarter = HEAD_DIM // 4
        rot = jnp.where(first_half, pltpu.roll(an, LANES - quarter, 1), pltpu.roll(an, quarter, 1))
        return an * cos + rot * sin

    gq_raw = [proj(s_na + p * LANES, LANES) for p in range(HEAD_PAIRS)]
    gk_raw = proj(s_gq, KV_WIDTH)
    gv_ref[...] = proj(s_gk, KV_WIDTH).astype(BF16)
    gate_chunk = 512
    for c in range(2 * d_model // gate_chunk):
        g = proj(s_gv + c * gate_chunk, gate_chunk)
        gate_ref[:, c * gate_chunk:(c + 1) * gate_chunk] = jax.nn.sigmoid(g).astype(BF16)
    qn = qn_ref[...]
    for p in range(HEAD_PAIRS):
        gq_ref[:, p * LANES:(p + 1) * LANES] = (norm_rope(gq_raw[p], qn) * GQA_Q_SCALE).astype(BF16)
    gk_ref[...] = norm_rope(gk_raw, kn_ref[...]).astype(BF16)


def _inproj(x2d, w_in_b, cos, sin, qn, kn, gm, *, seq, tm):
    n, d = x2d.shape
    d_in = w_in_b.shape[1]
    assert n % tm == 0 and seq % tm == 0
    sblocks = seq // tm
    row = lambda i: (i, 0)
    const = lambda i: (0, 0)
    pos = lambda i: (i % sblocks, 0)
    out_shape = [
        jax.ShapeDtypeStruct((n, NA_WIDTH), BF16), jax.ShapeDtypeStruct((n, NA_WIDTH), BF16),
        jax.ShapeDtypeStruct((n, NA_WIDTH), BF16), jax.ShapeDtypeStruct((n, GQA_WIDTH), BF16),
        jax.ShapeDtypeStruct((n, KV_WIDTH), BF16), jax.ShapeDtypeStruct((n, KV_WIDTH), BF16),
        jax.ShapeDtypeStruct((n, 2 * d), BF16),
    ]
    out_specs = [
        pl.BlockSpec((tm, NA_WIDTH), row), pl.BlockSpec((tm, NA_WIDTH), row), pl.BlockSpec((tm, NA_WIDTH), row),
        pl.BlockSpec((tm, GQA_WIDTH), row), pl.BlockSpec((tm, KV_WIDTH), row), pl.BlockSpec((tm, KV_WIDTH), row),
        pl.BlockSpec((tm, 2 * d), row),
    ]
    return pl.pallas_call(
        functools.partial(_inproj_kernel, d_model=d),
        grid=(n // tm,),
        in_specs=[
            pl.BlockSpec((tm, d), row), pl.BlockSpec((d, d_in), const),
            pl.BlockSpec((tm, LANES), pos), pl.BlockSpec((tm, LANES), pos),
            pl.BlockSpec((1, LANES), const), pl.BlockSpec((1, LANES), const),
            pl.BlockSpec((LANES, LANES), const),
        ],
        out_specs=out_specs,
        out_shape=out_shape,
        compiler_params=_params("parallel"),
        name="inproj",
    )(x2d, w_in_b, cos, sin, qn, kn, gm)


def _na_kernel(q_ref, kp_ref, kc_ref, kn_ref, vp_ref, vc_ref, vn_ref, bias_ref, o_ref,
               kbuf, vbuf, *, rows, nrb):
    blk = NA_WIN_H * GRID_W
    j = pl.program_id(0) % nrb
    kbuf[0:blk] = kp_ref[...]
    kbuf[blk:2 * blk] = kc_ref[...]
    kbuf[2 * blk:3 * blk] = kn_ref[...]
    vbuf[0:blk] = vp_ref[...]
    vbuf[blk:2 * blk] = vc_ref[...]
    vbuf[2 * blk:3 * blk] = vn_ref[...]
    lane = lax.broadcasted_iota(I32, (GRID_W, LANES), 1)
    low = lane < HEAD_DIM

    def rows_body(it, carry):
        items = []
        for u in range(NA_ROWS_PER_ITER):
            i = it * NA_ROWS_PER_ITER + u
            r = j * NA_WIN_H + i
            rs = jnp.clip(r - NA_WIN_H // 2, 0, rows - NA_WIN_H)
            d0 = rs - r + (NA_WIN_H - 1)
            off = pl.multiple_of((rs - (j - 1) * NA_WIN_H) * GRID_W, GRID_W)
            qoff = pl.multiple_of(i * GRID_W, GRID_W)
            for p in range(NA_WIDTH // LANES):
                items.append((d0, off, qoff, p, slice(p * LANES, (p + 1) * LANES)))
        scores = []
        for d0, off, qoff, p, cols in items:
            q2 = q_ref[pl.ds(qoff, GRID_W), cols]
            zero = jnp.zeros_like(q2)
            qs = jnp.concatenate([jnp.where(low, q2, zero), jnp.where(low, zero, q2)], axis=0)
            k2 = kbuf[pl.ds(off, blk), cols]
            scores.append(lax.dot_general(k2, qs, _NT, preferred_element_type=F32) + bias_ref[d0, p])
        probs = []
        for st in scores:
            e = jnp.exp(st - jnp.max(st, axis=0, keepdims=True))
            probs.append((e * (1.0 / jnp.sum(e, axis=0, keepdims=True))).T.astype(BF16))
        for (d0, off, qoff, p, cols), prob in zip(items, probs):
            o = jnp.dot(prob, vbuf[pl.ds(off, blk), cols], preferred_element_type=F32)
            o_ref[pl.ds(qoff, GRID_W), cols] = jnp.where(low, o[:GRID_W], o[GRID_W:]).astype(BF16)
        return carry

    lax.fori_loop(0, NA_WIN_H // NA_ROWS_PER_ITER, rows_body, 0)


def _na_bias_table(rpb):
    c = jnp.arange(GRID_W)
    cs = jnp.clip(c - NA_WIN_W // 2, 0, GRID_W - NA_WIN_W)
    cc = jnp.arange(GRID_W)
    inwin = (cc[None, :] >= cs[:, None]) & (cc[None, :] < cs[:, None] + NA_WIN_W)
    dc = jnp.clip(cc[None, :] - c[:, None] + (NA_WIN_W - 1), 0, 2 * NA_WIN_W - 2)
    full = jnp.where(inwin[None, None], rpb[:, :, dc].astype(F32), NEG_BIG)
    tbl = jnp.stack([full[:, d0:d0 + NA_WIN_H] for d0 in range(NA_WIN_H)])
    tbl = tbl.reshape(NA_WIN_H, NA_HEADS // 2, 2, NA_WIN_H, GRID_W, GRID_W)
    return tbl.transpose(0, 1, 3, 5, 2, 4).reshape(NA_WIN_H, NA_HEADS // 2, NA_WIN_H * GRID_W, 2 * GRID_W)


def _na_attention(q, k, v, bias_tbl, *, seq):
    n = q.shape[0]
    rows = seq // GRID_W
    assert rows % NA_WIN_H == 0 and rows >= 2 * NA_WIN_H
    nrb = rows // NA_WIN_H
    blk = NA_WIN_H * GRID_W

    def cur(g):
        return (g, 0)

    def prev(g):
        return (g - jnp.where(g % nrb == 0, 0, 1), 0)

    def nxt(g):
        return (g + jnp.where(g % nrb == nrb - 1, 0, 1), 0)

    spec = lambda f: pl.BlockSpec((blk, NA_WIDTH), f)
    return pl.pallas_call(
        functools.partial(_na_kernel, rows=rows, nrb=nrb),
        grid=(n // blk,),
        in_specs=[spec(cur), spec(prev), spec(cur), spec(nxt), spec(prev), spec(cur), spec(nxt),
                  pl.BlockSpec(bias_tbl.shape, lambda g: (0, 0, 0, 0))],
        out_specs=spec(cur),
        out_shape=jax.ShapeDtypeStruct((n, NA_WIDTH), BF16),
        scratch_shapes=[pltpu.VMEM((3 * blk, NA_WIDTH), BF16), pltpu.VMEM((3 * blk, NA_WIDTH), BF16)],
        compiler_params=_params("parallel"),
        name="na_attention",
    )(q, k, k, k, v, v, v, bias_tbl)


def _gqa_kernel(q_ref, k_ref, vt_ref, o_ref, qs_sc, sa_sc, sb_sc, m_sc, acc_sc, *, tq, tk, seq):
    q_groups = qs_sc.shape[0]
    lane = lax.broadcasted_iota(I32, (tq, LANES), 1)
    low = lane < HEAD_DIM
    for g in range(q_groups):
        for p in range(HEAD_PAIRS):
            q2 = q_ref[g * tq:(g + 1) * tq, p * LANES:(p + 1) * LANES]
            zero = jnp.zeros_like(q2)
            qs_sc[g, (2 * p) * tq:(2 * p + 1) * tq, :] = jnp.where(low, q2, zero)
            qs_sc[g, (2 * p + 1) * tq:(2 * p + 2) * tq, :] = jnp.where(low, zero, q2)
    m_sc[...] = jnp.full(m_sc.shape, -jnp.inf, F32)
    acc_sc[...] = jnp.zeros(acc_sc.shape, F32)
    n_chunks = seq // tk
    n_items = q_groups * n_chunks
    assert n_chunks % 2 == 0

    def scores(w, st_ref):
        koff = pl.multiple_of((w % n_chunks) * tk, tk)
        st_ref[...] = lax.dot_general(k_ref[pl.ds(koff, tk), :], qs_sc[w // n_chunks], _NT,
                                      preferred_element_type=F32)

    def consume(w, st_ref):
        g = w // n_chunks
        koff = pl.multiple_of((w % n_chunks) * tk, tk)
        st = st_ref[...]
        m_old = m_sc[g]
        m_new = jnp.maximum(m_old, jnp.max(st, axis=0, keepdims=True))
        alpha = jnp.exp2(m_old - m_new)
        e = jnp.exp2(st - m_new).astype(BF16)
        pv = jnp.dot(vt_ref[:, pl.ds(koff, tk)], e, preferred_element_type=F32)
        acc_sc[g] = alpha * acc_sc[g] + pv
        m_sc[g] = m_new

    scores(0, sa_sc)

    def body(j, carry):
        scores(2 * j + 1, sb_sc)
        consume(2 * j, sa_sc)
        scores(2 * j + 2, sa_sc)
        consume(2 * j + 1, sb_sc)
        return carry

    lax.fori_loop(0, n_items // 2 - 1, body, 0)
    scores(n_items - 1, sb_sc)
    consume(n_items - 2, sa_sc)
    consume(n_items - 1, sb_sc)
    top = lax.broadcasted_iota(I32, (KV_WIDTH, tq), 0) < HEAD_DIM
    for g in range(q_groups):
        o_t = acc_sc[g, 0:KV_WIDTH, :] * (1.0 / acc_sc[g, KV_WIDTH:KV_WIDTH + 1, :])
        for p in range(HEAD_PAIRS):
            a = o_t[:, (2 * p) * tq:(2 * p + 1) * tq]
            b = o_t[:, (2 * p + 1) * tq:(2 * p + 2) * tq]
            o_ref[g * tq:(g + 1) * tq, p * LANES:(p + 1) * LANES] = jnp.where(top, a, b).T.astype(BF16)


def _gqa_attention(q, k, vt, *, seq, tq, tk, q_groups):
    n = q.shape[0]
    rows = q_groups * tq
    assert seq % rows == 0 and seq % tk == 0
    qblocks = seq // rows
    width = GQA_Q_HEADS * tq
    return pl.pallas_call(
        functools.partial(_gqa_kernel, tq=tq, tk=tk, seq=seq),
        grid=(n // seq, qblocks),
        in_specs=[
            pl.BlockSpec((rows, GQA_WIDTH), lambda b, i: (b * qblocks + i, 0)),
            pl.BlockSpec((seq, KV_WIDTH), lambda b, i: (b, 0)),
            pl.BlockSpec((VT_ROWS, seq), lambda b, i: (b, 0)),
        ],
        out_specs=pl.BlockSpec((rows, GQA_WIDTH), lambda b, i: (b * qblocks + i, 0)),
        out_shape=jax.ShapeDtypeStruct((n, GQA_WIDTH), BF16),
        scratch_shapes=[pltpu.VMEM((q_groups, width, KV_WIDTH), BF16), pltpu.VMEM((tk, width), F32),
                        pltpu.VMEM((tk, width), F32), pltpu.VMEM((q_groups, 1, width), F32),
                        pltpu.VMEM((q_groups, VT_ROWS, width), F32)],
        compiler_params=_params("parallel", "parallel"),
        name="gqa_attention",
    )(q, k, vt)


def _layer_norm(h, g, b):
    mu = jnp.mean(h, axis=-1, keepdims=True)
    hc = h - mu
    var = jnp.mean(hc * hc, axis=-1, keepdims=True)
    return hc * lax.rsqrt(var + LN_EPS) * g + b


def _merge_kernel(na_ref, gq_ref, gate_ref, x_ref, wna_ref, wgq_ref, wout_ref, g_ref, b_ref,
                  wrh_ref, wrl_ref, x1_ref, aff_ref, *, alpha, d_model):
    tm = x_ref.shape[0]
    parts = [slice(r, r + MERGE_PART_ROWS) for r in range(0, tm, MERGE_PART_ROWS)]
    y_na = [jnp.dot(na_ref[rows, :], wna_ref[...], preferred_element_type=F32) for rows in parts]
    y_gq = [jnp.dot(gq_ref[rows, :], wgq_ref[...], preferred_element_type=F32) for rows in parts]
    mixin = [(gate_ref[rows, :d_model].astype(F32) * a + gate_ref[rows, d_model:].astype(F32) * b).astype(BF16)
             for rows, a, b in zip(parts, y_na, y_gq)]
    mix = [jnp.dot(m, wout_ref[...], preferred_element_type=F32) for m in mixin]
    x1 = [_layer_norm(alpha * x_ref[rows, :] + m, g_ref[...], b_ref[...]) for rows, m in zip(parts, mix)]
    wh = wrh_ref[...]
    for rows, x1p in zip(parts, x1):
        x1_ref[rows, :] = x1p
        hi = x1p.astype(BF16)
        lo = (x1p - hi.astype(F32)).astype(BF16)
        logits = (lax.dot_general(wh, hi, _NT, preferred_element_type=F32)
                  + lax.dot_general(wh, lo, _NT, preferred_element_type=F32)
                  + lax.dot_general(wrl_ref[...], hi, _NT, preferred_element_type=F32))
        e = jnp.exp(logits - jnp.max(logits, axis=0, keepdims=True))
        aff_ref[:, rows] = e / jnp.sum(e, axis=0, keepdims=True)


def _merge(na, gq, gates, x2d, wna, wgq, wout, ln_g, ln_b, wr_hi, wr_lo, *, alpha, tm):
    n, d = x2d.shape
    n_exp = wr_hi.shape[0]
    row = lambda i: (i, 0)
    const = lambda i: (0, 0)
    return pl.pallas_call(
        functools.partial(_merge_kernel, alpha=alpha, d_model=d),
        grid=(n // tm,),
        in_specs=[
            pl.BlockSpec((tm, NA_WIDTH), row), pl.BlockSpec((tm, GQA_WIDTH), row),
            pl.BlockSpec((tm, 2 * d), row), pl.BlockSpec((tm, d), row),
            pl.BlockSpec((NA_WIDTH, d), const), pl.BlockSpec((GQA_WIDTH, d), const),
            pl.BlockSpec((d, d), const), pl.BlockSpec((1, d), const), pl.BlockSpec((1, d), const),
            pl.BlockSpec((n_exp, d), const), pl.BlockSpec((n_exp, d), const),
        ],
        out_specs=[pl.BlockSpec((tm, d), row), pl.BlockSpec((n_exp, tm), lambda i: (0, i))],
        out_shape=[jax.ShapeDtypeStruct((n, d), F32), jax.ShapeDtypeStruct((n_exp, n), F32)],
        compiler_params=_params("parallel"),
        name="merge_ln_router",
    )(na, gq, gates, x2d, wna, wgq, wout, ln_g, ln_b, wr_hi, wr_lo)


def _route_kernel(aff_ref, pos_ref, base_ref, cnt_ref, *, cap, tile, count_chunk):
    n_exp, n = aff_ref.shape
    n_tiles = n // tile
    capf = float(cap)

    def bits_at(off, width):
        return lax.bitcast_convert_type(aff_ref[:, pl.ds(off, width)], I32)

    def count_ge(cand):
        def inner(c, acc):
            b = bits_at(pl.multiple_of(c * count_chunk, count_chunk), count_chunk)
            return acc + jnp.where(b >= cand, 1.0, 0.0)
        acc = lax.fori_loop(0, n // count_chunk, inner, jnp.zeros((n_exp, count_chunk), F32))
        return jnp.sum(acc, axis=1, keepdims=True)

    def bisect(i, prefix):
        cand = prefix | jnp.left_shift(jnp.int32(1), 30 - i)
        return jnp.where(count_ge(cand) >= capf, cand, prefix)

    thr = lax.fori_loop(0, 31, bisect, jnp.zeros((n_exp, 1), I32))
    need = capf - count_ge(thr + 1)

    ri = lax.broadcasted_iota(I32, (tile, tile), 0)
    ci = lax.broadcasted_iota(I32, (tile, tile), 1)
    upper = jnp.where(ri < ci, 1.0, 0.0).astype(BF16)
    ones = jnp.ones((tile, tile), BF16)
    tbl_lane = lax.broadcasted_iota(I32, base_ref.shape, 1)

    base_ref[...] = jnp.zeros(base_ref.shape, I32)
    cnt_ref[...] = jnp.zeros(cnt_ref.shape, I32)

    def body(i, carry):
        ceq, base = carry
        off = pl.multiple_of(i * tile, tile)
        b = lax.bitcast_convert_type(aff_ref[:, pl.ds(off, tile)], I32)
        eq = b == thr
        eqb = jnp.where(eq, 1.0, 0.0).astype(BF16)
        eqrank = jnp.dot(eqb, upper, preferred_element_type=F32) + ceq
        sel = (b > thr) | (eq & (eqrank < need))
        selb = jnp.where(sel, 1.0, 0.0).astype(BF16)
        rank_in_tile = jnp.dot(selb, upper, preferred_element_type=F32)
        cnt = jnp.dot(selb, ones, preferred_element_type=F32)
        pos_ref[:, pl.ds(off, tile)] = jnp.where(sel, base + rank_in_tile, float(UNPICKED)).astype(I32)
        base_ref[...] = jnp.where(tbl_lane == i, base[:, :1].astype(I32), base_ref[...])
        cnt_ref[...] = jnp.where(tbl_lane == i, cnt[:, :1].astype(I32), cnt_ref[...])
        padded = jnp.floor((cnt + (SUBLANES - 1.0)) * (1.0 / SUBLANES)) * SUBLANES
        return ceq + jnp.dot(eqb, ones, preferred_element_type=F32), base + padded

    zero = jnp.zeros((n_exp, tile), F32)
    _, used = lax.fori_loop(0, n_tiles, body, (zero, zero))
    base_ref[...] = jnp.where(tbl_lane == n_tiles, used[:, :1].astype(I32), base_ref[...])


def _route(aff, *, cap, tile):
    n_exp, n = aff.shape
    count_chunk = min(2048, n)
    assert n % count_chunk == 0 and n % tile == 0
    tbl_w = pl.cdiv(n // tile + 1, LANES) * LANES
    full = lambda shape: pl.BlockSpec(shape, lambda i: (0, 0))
    return pl.pallas_call(
        functools.partial(_route_kernel, cap=cap, tile=tile, count_chunk=count_chunk),
        grid=(1,),
        in_specs=[full((n_exp, n))],
        out_specs=[full((n_exp, n)), full((n_exp, tbl_w)), full((n_exp, tbl_w))],
        out_shape=[jax.ShapeDtypeStruct((n_exp, n), I32), jax.ShapeDtypeStruct((n_exp, tbl_w), I32),
                   jax.ShapeDtypeStruct((n_exp, tbl_w), I32)],
        compiler_params=_params("arbitrary"),
        name="route",
    )(aff)


def _pad_rows(c):
    return ((c + (SUBLANES - 1)) >> 3) << 3


def _start_pieces(rows, max_rows, make_copy):
    k = 3
    assert SUBLANES == 1 << k
    while (1 << k) <= max_rows:
        size = 1 << k

        @pl.when(((rows >> k) & 1) == 1)
        def _():
            make_copy(pl.multiple_of((rows >> (k + 1)) << (k + 1), SUBLANES), size).start()

        k += 1


def _wait_rows(hbm_ref, rows, sem):
    @pl.when(rows > 0)
    def _():
        view = hbm_ref.at[pl.ds(0, pl.multiple_of(rows, SUBLANES))]
        pltpu.make_async_copy(view, view, sem).wait()


def _dispatch_kernel(base_ref, cnt_ref, x_ref, aff_ref, pos_ref, xe_ref, few_sc, long_sc, few_sems, long_sem,
                     *, tile, cap_pad, n_tiles):
    i = pl.program_id(0)
    n_exp = pos_ref.shape[0]
    d = x_ref.shape[1]
    xb = x_ref[...].astype(BF16)
    a = aff_ref[...]
    hi = a.astype(BF16)
    rest1 = a - hi.astype(F32)
    mid = rest1.astype(BF16)
    lo = (rest1 - mid.astype(F32)).astype(BF16)
    pr = lax.broadcasted_iota(I32, (n_exp, LANES), 0)
    pc = lax.broadcasted_iota(I32, (n_exp, LANES), 1)
    gates = sum(jnp.dot(piece, jnp.where(pc == k * n_exp + pr, 1.0, 0.0).astype(BF16), preferred_element_type=F32)
                for k, piece in enumerate((hi, mid, lo))).astype(BF16)
    few = DISPATCH_FEW_ROWS
    par = i % 2

    def few_rows_total(j):
        total = jnp.int32(0)
        for e in range(n_exp):
            r = _pad_rows(cnt_ref[e, j])
            total = total + jnp.where(r <= few, r, 0)
        return total

    @pl.when(i >= 2)
    def _():
        _wait_rows(xe_ref, few_rows_total(jnp.maximum(i - 2, 0)), few_sems.at[par])

    bases = [base_ref[e, i] for e in range(n_exp)]
    padded = [_pad_rows(cnt_ref[e, i]) for e in range(n_exp)]

    def compact(e, n_rows):
        row_id = lax.broadcasted_iota(I32, (n_rows, tile), 0)
        return jnp.where(row_id == pos_ref[e:e + 1, :] - bases[e], 1.0, 0.0).astype(BF16)

    onehot = jnp.concatenate([compact(e, few) for e in range(n_exp)], axis=0)
    few_sc[par, :, 0:d] = jnp.dot(onehot, xb, preferred_element_type=F32)
    few_sc[par, :, d:d + LANES] = jnp.dot(onehot, gates, preferred_element_type=F32)

    long_total = jnp.int32(0)
    for e in range(n_exp):
        rows = padded[e]
        dst0 = e * cap_pad + bases[e]
        is_long = rows > few

        def few_copy(off, size):
            return pltpu.make_async_copy(few_sc.at[par, pl.ds(pl.multiple_of(e * few + off, SUBLANES), size)],
                                         xe_ref.at[pl.ds(pl.multiple_of(dst0 + off, SUBLANES), size)],
                                         few_sems.at[par])

        _start_pieces(jnp.where(is_long, 0, rows), few, few_copy)

        @pl.when(is_long)
        def _():
            full = compact(e, tile)
            long_sc[e, :, 0:d] = jnp.dot(full, xb, preferred_element_type=F32)
            long_sc[e, :, d:d + LANES] = jnp.dot(full, gates, preferred_element_type=F32)

            def long_copy(off, size):
                return pltpu.make_async_copy(long_sc.at[e, pl.ds(off, size)],
                                             xe_ref.at[pl.ds(pl.multiple_of(dst0 + off, SUBLANES), size)], long_sem)

            _start_pieces(rows, tile, long_copy)

        long_total = long_total + jnp.where(is_long, rows, 0)
    _wait_rows(xe_ref, long_total, long_sem)

    @pl.when(i == n_tiles - 1)
    def _():
        _wait_rows(xe_ref, few_rows_total(i), few_sems.at[par])

        @pl.when(i >= 1)
        def _():
            _wait_rows(xe_ref, few_rows_total(jnp.maximum(i - 1, 0)), few_sems.at[1 - par])

        long_sc[0] = jnp.zeros((tile, long_sc.shape[2]), F32)
        for e in range(n_exp):
            used = base_ref[e, n_tiles]
            tail = cap_pad - used
            dst0 = e * cap_pad + used

            def zero_copy(off, size):
                return pltpu.make_async_copy(long_sc.at[0, pl.ds(0, size)],
                                             xe_ref.at[pl.ds(pl.multiple_of(dst0 + off, SUBLANES), size)], long_sem)

            def whole(j, carry):
                cp = zero_copy(j * tile, tile)
                cp.start()
                cp.wait()
                return carry

            n_whole = tail // tile
            lax.fori_loop(0, n_whole, whole, 0)
            rest = tail - n_whole * tile
            _start_pieces(rest, tile, lambda off, size: zero_copy(n_whole * tile + off, size))
            _wait_rows(xe_ref, rest, long_sem)


def _dispatch(x1, aff_t, pos, base_tbl, cnt_tbl, *, cap_pad, tile):
    n, d = x1.shape
    n_exp = pos.shape[0]
    assert 3 * n_exp <= LANES
    n_tiles = n // tile
    return pl.pallas_call(
        functools.partial(_dispatch_kernel, tile=tile, cap_pad=cap_pad, n_tiles=n_tiles),
        grid_spec=pltpu.PrefetchScalarGridSpec(
            num_scalar_prefetch=2,
            grid=(n_tiles,),
            in_specs=[pl.BlockSpec((tile, d), lambda i, b, c: (i, 0)),
                      pl.BlockSpec((tile, n_exp), lambda i, b, c: (i, 0)),
                      pl.BlockSpec((n_exp, tile), lambda i, b, c: (0, i))],
            out_specs=pl.BlockSpec(memory_space=pl.ANY),
            scratch_shapes=[pltpu.VMEM((2, n_exp * DISPATCH_FEW_ROWS, d + LANES), F32),
                            pltpu.VMEM((n_exp, tile, d + LANES), F32),
                            pltpu.SemaphoreType.DMA((2,)), pltpu.SemaphoreType.DMA(())],
        ),
        out_shape=jax.ShapeDtypeStruct((n_exp * cap_pad, d + LANES), F32),
        compiler_params=_params("arbitrary"),
        name="dispatch",
    )(base_tbl, cnt_tbl, x1, aff_t, pos)


def _ffn_kernel(base_ref, xe_ref, wg_ref, wu_ref, wd_ref, ye_ref, *, f_chunk, n_tiles):
    tm = xe_ref.shape[0]
    d = ye_ref.shape[1]
    n_exp = pl.num_programs(0)
    e = pl.program_id(0)
    used = base_ref[e, n_tiles]
    live = pl.program_id(1) * tm < used

    @pl.when(live)
    def _():
        xb = xe_ref[:, 0:d].astype(BF16)
        lane = lax.broadcasted_iota(I32, (tm, LANES), 1)
        mine = (lane % n_exp == e) & (lane < 3 * n_exp)
        gate = jnp.sum(jnp.where(mine, xe_ref[:, d:d + LANES], 0.0), axis=1, keepdims=True)
        n_chunks = wg_ref.shape[2] // f_chunk

        def gate_up(c):
            cols = slice(c * f_chunk, (c + 1) * f_chunk)
            return (jnp.dot(xb, wg_ref[0, :, cols], preferred_element_type=F32),
                    jnp.dot(xb, wu_ref[0, :, cols], preferred_element_type=F32))

        acc = jnp.zeros(ye_ref.shape, F32)
        g, u = gate_up(0)
        for c in range(n_chunks):
            h = ((g * jax.nn.sigmoid(g)) * u).astype(BF16)
            if c + 1 < n_chunks:
                g, u = gate_up(c + 1)
            acc = acc + jnp.dot(h, wd_ref[0, c * f_chunk:(c + 1) * f_chunk, :], preferred_element_type=F32)
        ye_ref[...] = acc * gate

    @pl.when(jnp.logical_not(live))
    def _():
        ye_ref[...] = jnp.zeros(ye_ref.shape, F32)


def _ffn(xe, base_tbl, wg, wu, wd, *, cap_pad, tm, f_chunk, n_tiles):
    n_slots, d_ext = xe.shape
    n_exp, d, d_ff = wg.shape
    assert cap_pad % tm == 0 and d_ff % f_chunk == 0 and d_ext == d + LANES
    kblocks = cap_pad // tm
    return pl.pallas_call(
        functools.partial(_ffn_kernel, f_chunk=f_chunk, n_tiles=n_tiles),
        grid_spec=pltpu.PrefetchScalarGridSpec(
            num_scalar_prefetch=1,
            grid=(n_exp, kblocks),
            in_specs=[
                pl.BlockSpec((tm, d_ext), lambda e, k, b: (e * kblocks + k, 0)),
                pl.BlockSpec((1, d, d_ff), lambda e, k, b: (e, 0, 0)),
                pl.BlockSpec((1, d, d_ff), lambda e, k, b: (e, 0, 0)),
                pl.BlockSpec((1, d_ff, d), lambda e, k, b: (e, 0, 0)),
            ],
            out_specs=pl.BlockSpec((tm, d), lambda e, k, b: (e * kblocks + k, 0)),
        ),
        out_shape=jax.ShapeDtypeStruct((n_slots, d), F32),
        compiler_params=_params("parallel", "parallel"),
        name="expert_ffn",
    )(base_tbl, xe, wg, wu, wd)


def _combine_kernel(base_ref, cnt_ref, x_ref, pos_ref, g_ref, b_ref, ye_ref, o_ref,
                    stg, hit_sc, sems, *, tile, cap_pad, n_tiles, alpha):
    i = pl.program_id(0)
    n_exp = pos_ref.shape[1]
    half = stg.shape[0] // 2

    @pl.when(i == 0)
    def _():
        stg[...] = jnp.zeros(stg.shape, F32)

    def layout(j):
        segs, off = [], jnp.int32(0)
        for e in range(n_exp):
            rows = _pad_rows(cnt_ref[e, j])
            segs.append((base_ref[e, j], rows, off))
            off = off + rows
        return segs, off

    def first_row(j, total):
        return jnp.where(total > half, 0, (j % 2) * half)

    def fetch(j, segs, row0):
        for e, (base, rows, off) in enumerate(segs):
            src0 = e * cap_pad + base
            dst0 = row0 + off

            def make_copy(o, size):
                return pltpu.make_async_copy(ye_ref.at[pl.ds(pl.multiple_of(src0 + o, SUBLANES), size)],
                                             stg.at[pl.ds(pl.multiple_of(dst0 + o, SUBLANES), size)], sems.at[j % 2])

            _start_pieces(rows, tile, make_copy)

    nxt = jnp.minimum(i + 1, n_tiles - 1)
    segs, total = layout(i)
    segs_next, total_next = layout(nxt)
    _, total_prev = layout(jnp.maximum(i - 1, 0))
    row0 = first_row(i, total)
    small, small_next, small_prev = total <= half, total_next <= half, total_prev <= half

    @pl.when(jnp.logical_not((i > 0) & small_prev & small))
    def _():
        fetch(i, segs, row0)

    _wait_rows(ye_ref, total, sems.at[i % 2])

    @pl.when((i + 1 < n_tiles) & small & small_next)
    def _():
        fetch(nxt, segs_next, first_row(nxt, total_next))

    o_ref[...] = alpha * x_ref[...]
    slot_id = lax.broadcasted_iota(I32, (tile, tile), 1)

    def block(b, carry):
        chunk = stg[pl.ds(pl.multiple_of(row0 + b * tile, tile), tile), :].astype(BF16)
        hit_sc[...] = jnp.zeros(hit_sc.shape, F32)
        for e, (base, rows, off) in enumerate(segs):
            @pl.when((off < (b + 1) * tile) & (off + rows > b * tile))
            def _():
                match = slot_id == (pos_ref[:, e:e + 1] - (base - off + b * tile))
                hit_sc[...] = jnp.where(match, 1.0, hit_sc[...])

        o_ref[...] += jnp.dot(hit_sc[...].astype(BF16), chunk, preferred_element_type=F32)
        return carry

    lax.fori_loop(0, (total + (tile - 1)) // tile, block, 0)
    o_ref[...] = _layer_norm(o_ref[...], g_ref[...], b_ref[...])


def _combine(x1, pos_t, base_tbl, cnt_tbl, ln_g, ln_b, ye, *, cap_pad, alpha, tile):
    n, d = x1.shape
    n_exp = pos_t.shape[1]
    row = lambda i, b, c: (i, 0)
    const = lambda i, b, c: (0, 0)
    stg_rows = n_exp * tile
    return pl.pallas_call(
        functools.partial(_combine_kernel, tile=tile, cap_pad=cap_pad, n_tiles=n // tile, alpha=alpha),
        grid_spec=pltpu.PrefetchScalarGridSpec(
            num_scalar_prefetch=2,
            grid=(n // tile,),
            in_specs=[
                pl.BlockSpec((tile, d), row), pl.BlockSpec((tile, n_exp), row),
                pl.BlockSpec((1, d), const), pl.BlockSpec((1, d), const),
                pl.BlockSpec(memory_space=pl.ANY),
            ],
            out_specs=pl.BlockSpec((tile, d), row),
            scratch_shapes=[pltpu.VMEM((stg_rows, d), F32), pltpu.VMEM((tile, tile), F32),
                            pltpu.SemaphoreType.DMA((2,))],
        ),
        out_shape=jax.ShapeDtypeStruct((n, d), F32),
        compiler_params=_params("arbitrary"),
        name="combine_ln",
    )(base_tbl, cnt_tbl, x1, pos_t, ln_g, ln_b, ye)


def _rope_tables(seq):
    t = jnp.arange(seq)
    row = (t // GRID_W).astype(F32)
    col = (t % GRID_W).astype(F32)
    half = HEAD_DIM // 2
    inv_freq = ROPE_THETA ** (-jnp.arange(0, half, 2, dtype=F32) / half)
    ang_r = row[:, None] * inv_freq[None, :]
    ang_c = col[:, None] * inv_freq[None, :]
    ang = jnp.concatenate([ang_r, ang_r, ang_c, ang_c], axis=-1)
    sign = jnp.where((jnp.arange(HEAD_DIM) % half) < half // 2, -1.0, 1.0).astype(F32)
    reps = LANES // HEAD_DIM
    return jnp.tile(jnp.cos(ang), (1, reps)), jnp.tile(jnp.sin(ang) * sign[None, :], (1, reps))


def _gqa_slot_columns():
    heads = [p + GQA_GROUP * half for p in range(HEAD_PAIRS) for half in range(2)]
    return np.concatenate([np.arange(HEAD_DIM) + HEAD_DIM * h for h in heads])


def _prep_layer(w_in, na_rpb, q_norm, k_norm, w_br_na, w_br_gqa, w_out, ln1_g, ln1_b,
                w_router, w_e_gate, w_e_up, w_e_down, ln2_g, ln2_b):
    d = w_in.shape[0]
    s_na = 3 * NA_WIDTH
    cols = _gqa_slot_columns()
    perm = np.concatenate([np.arange(s_na), s_na + cols, np.arange(s_na + GQA_WIDTH, w_in.shape[1])])
    reps = LANES // HEAD_DIM
    wr_t = w_router.T.astype(F32)
    wr_hi = wr_t.astype(BF16)
    gm = np.kron(np.eye(reps, dtype=np.float32), np.full((HEAD_DIM, HEAD_DIM), 1.0 / HEAD_DIM, np.float32))
    return dict(
        w_in=w_in[:, perm].astype(BF16),
        bias_tbl=_na_bias_table(na_rpb),
        qn=jnp.tile(q_norm.astype(F32), reps)[None, :], kn=jnp.tile(k_norm.astype(F32), reps)[None, :],
        gm=jnp.asarray(gm, BF16),
        wna=w_br_na.astype(BF16), wgq=w_br_gqa[cols].astype(BF16), wout=w_out.astype(BF16),
        ln1_g=ln1_g.astype(F32).reshape(1, d), ln1_b=ln1_b.astype(F32).reshape(1, d),
        wr_hi=wr_hi, wr_lo=(wr_t - wr_hi.astype(F32)).astype(BF16),
        wg=w_e_gate.astype(BF16), wu=w_e_up.astype(BF16), wd=w_e_down.astype(BF16),
        ln2_g=ln2_g.astype(F32).reshape(1, d), ln2_b=ln2_b.astype(F32).reshape(1, d),
    )


def _tiles(seq):
    return dict(tm_proj=512, tq=256, tk=1024, q_groups=4, tm_merge=512, t_moe=256, tm_ffn=512, f_chunk=512)


def _trunk_layer(x2d, p, *, seq, alpha, rope):
    n, d = x2d.shape
    n_exp = p["wg"].shape[0]
    cap = EC_CAPACITY * n // n_exp
    tl = _tiles(seq)
    cos, sin = rope
    naq, nak, nav, gq, gk, gv, gates = _inproj(x2d, p["w_in"], cos, sin, p["qn"], p["kn"], p["gm"],
                                               seq=seq, tm=min(tl["tm_proj"], seq))
    na = _na_attention(naq, nak, nav, p["bias_tbl"], seq=seq)
    nb = n // seq
    ones_rows = jnp.zeros((nb, VT_ROWS - KV_WIDTH, seq), BF16).at[:, 0, :].set(1.0)
    gvt = jnp.concatenate([gv.reshape(nb, seq, KV_WIDTH).transpose(0, 2, 1), ones_rows], axis=1)
    gvt = gvt.reshape(nb * VT_ROWS, seq)
    ga = _gqa_attention(gq, gk, gvt, seq=seq, tq=min(tl["tq"], seq), tk=min(tl["tk"], seq // 2),
                        q_groups=tl["q_groups"])
    x1, aff = _merge(na, ga, gates, x2d, p["wna"], p["wgq"], p["wout"], p["ln1_g"], p["ln1_b"],
                     p["wr_hi"], p["wr_lo"], alpha=alpha, tm=tl["tm_merge"])
    tile = tl["t_moe"]
    n_tiles = n // tile
    tm_ffn = min(tl["tm_ffn"], cap)
    cap_pad = pl.cdiv(cap + (SUBLANES - 1) * n_tiles, tm_ffn) * tm_ffn
    pos, base_tbl, cnt_tbl = _route(aff, cap=cap, tile=tile)
    xe = _dispatch(x1, aff.T, pos, base_tbl, cnt_tbl, cap_pad=cap_pad, tile=tile)
    ye = _ffn(xe, base_tbl, p["wg"], p["wu"], p["wd"], cap_pad=cap_pad, tm=tm_ffn, f_chunk=tl["f_chunk"],
              n_tiles=n_tiles)
    return _combine(x1, pos.T, base_tbl, cnt_tbl, p["ln2_g"], p["ln2_b"], ye,
                    cap_pad=cap_pad, alpha=alpha, tile=tile)


@jax.jit
def kernel(x_prompt, x_sample, w_in, na_rpb, q_norm, k_norm, w_br_na, w_br_gqa, w_out, ln1_g, ln1_b,
           w_router, w_e_gate, w_e_up, w_e_down, ln2_g, ln2_b):
    depth = w_in.shape[0]
    alpha = float((2 * depth) ** 0.25)
    d = x_prompt.shape[-1]
    groups = []
    for x in (x_prompt, x_sample):
        b, s, _ = x.shape
        groups.append(dict(x=x.reshape(b * s, d), shape=x.shape, seq=s, rope=_rope_tables(s)))
    for l in range(depth):
        p = _prep_layer(w_in[l], na_rpb[l], q_norm[l], k_norm[l], w_br_na[l], w_br_gqa[l], w_out[l],
                        ln1_g[l], ln1_b[l], w_router[l], w_e_gate[l], w_e_up[l], w_e_down[l],
                        ln2_g[l], ln2_b[l])
        for g in groups:
            g["x"] = _trunk_layer(g["x"], p, seq=g["seq"], alpha=alpha, rope=g["rope"])
    return tuple(g["x"].reshape(g["shape"]) for g in groups)
```
